```python
import math
import jax
import jax.numpy as jnp
from jax import lax
import numpy as np

D_MODEL = 4096
BATCH = 2
SEQ = 8192
DEPTH = 2

GRID_W = 64
CTX_LEN = 256
N_BRANCH = 4
BRANCH_W = D_MODEL // 4
ATT_HEADS = 8
ATT_DH = BRANCH_W // (2 * ATT_HEADS)
ATT_DV = 2 * ATT_DH
ROPE_THETA = 10000.0
ROPE_FREQS = ATT_DH // 4
Q_BLOCK = 128
POOL_WINDOWS = (2, 4, 8, 16)
POOL_GROUPS = len(POOL_WINDOWS)
POOL_GW = BRANCH_W // POOL_GROUPS
FOURIER_GROUPS = 4
FOURIER_GW = BRANCH_W // FOURIER_GROUPS
CONV_K = 3
N_MOD = 6
D_FF = ((8 * D_MODEL // 3 + 255) // 256) * 256
N_EXPERTS = 8
TOP_K = 2
D_FF_EXPERT = D_FF // 4
N_DENSE = (DEPTH + 1) // 2
N_MOE = DEPTH // 2
EPS = 1e-6
Q_OFF = 0
K_OFF = BRANCH_W
V_OFF = 2 * BRANCH_W
POOL_OFF = 3 * BRANCH_W
FOUR_OFF = 4 * BRANCH_W
CB_OFF = 5 * BRANCH_W
CC_OFF = 6 * BRANCH_W
CH_OFF = 7 * BRANCH_W
IN_W = 8 * BRANCH_W

kernel_name = 'hybrid_parallel_mixer_moe_dit_block'


def rmsnorm(x, g):
    xf = x.astype(jnp.float32)
    y = xf * lax.rsqrt(jnp.mean(xf * xf, axis=-1, keepdims=True) + EPS)
    return (y * g.astype(jnp.float32)).astype(x.dtype)


def modulation(cond, w, b, n_chunks):
    cols = n_chunks * D_MODEL
    m = jax.nn.silu(cond) @ w[:, :cols] + b[:cols]
    m = m.reshape(-1, 1, cols)
    return jnp.split(m, n_chunks, axis=-1)


def modulate(xn, shift, scale):
    return xn * (1.0 + scale) + shift


def axial_rope_tables(rows):
    row = jnp.repeat(jnp.arange(rows, dtype=jnp.float32), GRID_W)
    col = jnp.tile(jnp.arange(GRID_W, dtype=jnp.float32), rows)
    freqs = ROPE_THETA ** (-jnp.arange(ROPE_FREQS, dtype=jnp.float32) / ROPE_FREQS)
    ang = jnp.stack([row[:, None] * freqs, col[:, None] * freqs], axis=1)
    return jnp.cos(ang), jnp.sin(ang)


def apply_axial_rope(x, cos, sin):
    shp = x.shape
    xs = x.reshape(shp[:-1] + (2, 2, ROPE_FREQS))
    x1, x2 = xs[..., 0, :], xs[..., 1, :]
    cs = cos[None, :, None, None].astype(x.dtype)
    sn = sin[None, :, None, None].astype(x.dtype)
    out = jnp.stack([x1 * cs - x2 * sn, x2 * cs + x1 * sn], axis=-2)
    return out.reshape(shp)


def split_q(pq, g):
    b, n, _ = pq.shape
    return rmsnorm(pq.reshape(b, n, ATT_HEADS, 2, ATT_DH), g)


def split_kv(pkv, g):
    b, n, _ = pkv.shape
    k = rmsnorm(pkv[..., :BRANCH_W].reshape(b, n, ATT_HEADS, 2, ATT_DH), g)
    v = pkv[..., BRANCH_W:].reshape(b, n, ATT_HEADS, ATT_DV)
    return k, v


def diff_softmax_attend(q, k, v, lam):
    s = jnp.einsum('bqhcd,bkhcd->bchqk', q, k, preferred_element_type=jnp.float32) * (ATT_DH ** -0.5)
    p = jax.nn.softmax(s, axis=-1)
    a = p[:, 0] - lam * p[:, 1]
    return jnp.einsum('bhqk,bkhe->bqhe', a, v)


def latent_diff_attention(q, k_all, v_all, lam):
    b, n = q.shape[0], q.shape[1]
    nb = n // Q_BLOCK
    qb = jnp.swapaxes(q.reshape(b, nb, Q_BLOCK, ATT_HEADS, 2, ATT_DH), 0, 1)
    ob = lax.map(lambda blk: diff_softmax_attend(blk, k_all, v_all, lam), qb)
    return jnp.swapaxes(ob, 0, 1).reshape(b, n, ATT_HEADS, ATT_DV)


def diff_head_out(o, g, lam_init, dtype):
    b, n = o.shape[0], o.shape[1]
    return (rmsnorm(o, g) * (1.0 - lam_init)).reshape(b, n, BRANCH_W).astype(dtype)


def multiscale_pool(u, pool_w, pool_scale):
    b, n, _ = u.shape
    ug = u.reshape(b, n, POOL_GROUPS, POOL_GW).astype(jnp.float32)
    csum = jnp.pad(jnp.cumsum(ug, axis=1), ((0, 0), (1, 0), (0, 0), (0, 0)))
    t = jnp.arange(n)
    outs = []
    for g, w in enumerate(POOL_WINDOWS):
        lo = jnp.clip(t - w // 2, 0, n - 1)
        hi = jnp.clip(t + w - 1 - w // 2, 0, n - 1)
        cnt = (hi - lo + 1).astype(jnp.float32)[None, :, None]
        cg = csum[:, :, g]
        win = (jnp.take(cg, hi + 1, axis=1) - jnp.take(cg, lo, axis=1)) / cnt
        outs.append(win - ug[:, :, g])
    pooled = jnp.stack(outs, axis=2).astype(u.dtype)
    y = jnp.einsum('bngc,gcd->bngd', pooled, pool_w)
    return y.reshape(b, n, BRANCH_W) * pool_scale


def fourier_mix(u):
    b, n, _ = u.shape
    ug = u.reshape(b, n, FOURIER_GROUPS, FOURIER_GW).astype(jnp.float32)
    y = jnp.fft.fft2(ug, axes=(1, 3), norm='ortho').real
    return y.reshape(b, n, BRANCH_W).astype(u.dtype)


def short_conv_mix(bg, cg, hv, conv_w):
    z = cg * hv
    n = z.shape[1]
    pad = CONV_K // 2
    zp = jnp.pad(z, ((0, 0), (pad, CONV_K - 1 - pad), (0, 0)))
    conv = None
    for j in range(CONV_K):
        term = conv_w[j] * zp[:, j:j + n]
        conv = term if conv is None else conv + term
    return bg * conv


def local_branches(p, pool_w, pool_scale, conv_w):
    pool = multiscale_pool(p[..., POOL_OFF:FOUR_OFF], pool_w, pool_scale)
    four = fourier_mix(p[..., FOUR_OFF:CB_OFF])
    conv = short_conv_mix(p[..., CB_OFF:CC_OFF], p[..., CC_OFF:CH_OFF], p[..., CH_OFF:IN_W], conv_w)
    return pool, four, conv


def merge_branches(xm, branches, w_gate, w_branch, w_out):
    gate_logits = xm @ w_gate
    merged = None
    for i, y in enumerate(branches):
        term = jax.nn.sigmoid(gate_logits[..., i * D_MODEL:(i + 1) * D_MODEL]) * (y @ w_branch[i])
        merged = term if merged is None else merged + term
    return merged @ w_out


def swiglu(z, w1, w3, w2):
    return (jax.nn.silu(z @ w1) * (z @ w3)) @ w2


def moe_swiglu(z, router, w1, w3, w2):
    logits = (z @ router).astype(jnp.float32)
    top_v, top_i = lax.top_k(logits, TOP_K)
    wts = jax.nn.softmax(top_v, axis=-1)
    combine = jnp.sum(jax.nn.one_hot(top_i, N_EXPERTS, dtype=jnp.float32) * wts[..., None], axis=-2).astype(z.dtype)
    out = None
    for e in range(N_EXPERTS):
        term = combine[..., e:e + 1] * swiglu(z, w1[e], w3[e], w2[e])
        out = term if out is None else out + term
    return out


def setup_inputs(seed: int = 0) -> dict:
    key = jax.random.key(seed)
    ks = jax.random.split(key, 32)
    D = D_MODEL

    def nrm(k, shape, s):
        return jax.random.normal(k, shape, jnp.float32) * s

    return {
        'x': nrm(ks[0], (BATCH, SEQ, D), 1.0),
        'c': nrm(ks[1], (BATCH, D), 1.0),
        'ctx': nrm(ks[2], (BATCH, CTX_LEN, D), 1.0),
        'c_ctx': nrm(ks[3], (D,), 1.0),
        'w_mod': nrm(ks[4], (DEPTH, D, N_MOD * D), 0.3 * D ** -0.5),
        'b_mod': nrm(ks[5], (DEPTH, N_MOD * D), 0.02),
        'norm_mix': 1.0 + nrm(ks[6], (DEPTH, D), 0.05),
        'norm_ffn': 1.0 + nrm(ks[7], (DEPTH, D), 0.05),
        'w_in': nrm(ks[8], (DEPTH, D, IN_W), D ** -0.5),
        'w_gate': nrm(ks[9], (DEPTH, D, N_BRANCH * D), D ** -0.5),
        'q_norm': 1.0 + nrm(ks[10], (DEPTH, ATT_DH), 0.05),
        'k_norm': 1.0 + nrm(ks[11], (DEPTH, ATT_DH), 0.05),
        'lambda_q1': nrm(ks[12], (DEPTH, ATT_DH), 0.1),
        'lambda_k1': nrm(ks[13], (DEPTH, ATT_DH), 0.1),
        'lambda_q2': nrm(ks[14], (DEPTH, ATT_DH), 0.1),
        'lambda_k2': nrm(ks[15], (DEPTH, ATT_DH), 0.1),
        'subln': 1.0 + nrm(ks[16], (DEPTH, ATT_DV), 0.05),
        'pool_w': nrm(ks[17], (DEPTH, POOL_GROUPS, POOL_GW, POOL_GW), POOL_GW ** -0.5),
        'pool_scale': 1.0 + nrm(ks[18], (DEPTH, BRANCH_W), 0.1),
        'conv_w': nrm(ks[19], (DEPTH, CONV_K, BRANCH_W), CONV_K ** -0.5),
        'w_branch': nrm(ks[20], (DEPTH, N_BRANCH, BRANCH_W, D), BRANCH_W ** -0.5),
        'w_out': nrm(ks[21], (DEPTH, D, D), D ** -0.5),
        'ffn_w1': nrm(ks[22], (N_DENSE, D, D_FF), D ** -0.5),
        'ffn_w3': nrm(ks[23], (N_DENSE, D, D_FF), D ** -0.5),
        'ffn_w2': nrm(ks[24], (N_DENSE, D_FF, D), D_FF ** -0.5),
        'router': nrm(ks[25], (N_MOE, D, N_EXPERTS), D ** -0.5),
        'moe_w1': nrm(ks[26], (N_MOE, N_EXPERTS, D, D_FF_EXPERT), D ** -0.5),
        'moe_w3': nrm(ks[27], (N_MOE, N_EXPERTS, D, D_FF_EXPERT), D ** -0.5),
        'moe_w2': nrm(ks[28], (N_MOE, N_EXPERTS, D_FF_EXPERT, D), D_FF_EXPERT ** -0.5),
    }


def reference(x, c, ctx, c_ctx, w_mod, b_mod, norm_mix, norm_ffn, w_in, w_gate, q_norm, k_norm,
              lambda_q1, lambda_k1, lambda_q2, lambda_k2, subln, pool_w, pool_scale, conv_w,
              w_branch, w_out, ffn_w1, ffn_w3, ffn_w2, router, moe_w1, moe_w3, moe_w2):
    n_lat = x.shape[1]
    rows = n_lat // GRID_W
    cos, sin = axial_rope_tables(rows)
    h, hc = x, ctx
    for l in range(DEPTH):
        last = l == DEPTH - 1
        lam_init = 0.8 - 0.6 * math.exp(-0.3 * l)
        lam = (jnp.exp(jnp.sum(lambda_q1[l].astype(jnp.float32) * lambda_k1[l].astype(jnp.float32)))
               - jnp.exp(jnp.sum(lambda_q2[l].astype(jnp.float32) * lambda_k2[l].astype(jnp.float32)))
               + lam_init)
        sh_m, sc_m, g_m, sh_f, sc_f, g_f = modulation(c, w_mod[l], b_mod[l], N_MOD)
        cm = modulation(c_ctx, w_mod[l], b_mod[l], 2 if last else N_MOD)

        xm = modulate(rmsnorm(h, norm_mix[l]), sh_m, sc_m)
        xcm = modulate(rmsnorm(hc, norm_mix[l]), cm[0], cm[1])

        if last:
            pc_kv = xcm @ w_in[l][:, K_OFF:POOL_OFF]
        else:
            pc = xcm @ w_in[l]
            pc_kv = pc[..., K_OFF:POOL_OFF]
        k_c, v_c = split_kv(pc_kv, k_norm[l])

        p = xm @ w_in[l]
        q = apply_axial_rope(split_q(p[..., Q_OFF:K_OFF], q_norm[l]), cos, sin)
        k, v = split_kv(p[..., K_OFF:POOL_OFF], k_norm[l])
        k = apply_axial_rope(k, cos, sin)
        k_all = jnp.concatenate([k_c, k], axis=1)
        v_all = jnp.concatenate([v_c, v], axis=1).astype(jnp.float32)
        att = diff_head_out(latent_diff_attention(q, k_all, v_all, lam), subln[l], lam_init, x.dtype)
        branches = (att,) + local_branches(p, pool_w[l], pool_scale[l], conv_w[l])
        h = h + g_m * merge_branches(xm, branches, w_gate[l], w_branch[l], w_out[l])

        if not last:
            q_c = split_q(pc[..., Q_OFF:K_OFF], q_norm[l])
            att_c = diff_head_out(diff_softmax_attend(q_c, k_c, v_c.astype(jnp.float32), lam),
                                  subln[l], lam_init, x.dtype)
            branches_c = (att_c,) + local_branches(pc, pool_w[l], pool_scale[l], conv_w[l])
            hc = hc + cm[2] * merge_branches(xcm, branches_c, w_gate[l], w_branch[l], w_out[l])

        def channel_mix(z):
            if l % 2 == 0:
                return swiglu(z, ffn_w1[l // 2], ffn_w3[l // 2], ffn_w2[l // 2])
            return moe_swiglu(z, router[l // 2], moe_w1[l // 2], moe_w3[l // 2], moe_w2[l // 2])

        h = h + g_f * channel_mix(modulate(rmsnorm(h, norm_ffn[l]), sh_f, sc_f))
        if not last:
            hc = hc + cm[5] * channel_mix(modulate(rmsnorm(hc, norm_ffn[l]), cm[3], cm[4]))
    return h
```

```python
import functools
import math

import jax
import jax.numpy as jnp
from jax import lax
from jax.experimental import pallas as pl
from jax.experimental.pallas import tpu as pltpu

F32 = jnp.float32
BF16 = jnp.bfloat16

GRID_W = 64
ROPE_THETA = 10000.0
POOL_WINDOWS = (2, 4, 8, 16)
FOURIER_GROUPS = 4
CONV_K = 3
N_MOD = 6
EPS = 1e-6
LANES = 128
HALO = 16
VMEM_LIMIT = 56 * 1024 * 1024
HI = lax.Precision.HIGHEST


def _cp(*sem, vmem=VMEM_LIMIT):
    return pltpu.CompilerParams(dimension_semantics=sem, vmem_limit_bytes=vmem)


def _pick(n, prefs):
    for t in prefs:
        if n % t == 0:
            return t
    return n


def _sigmoid(x):
    return 1.0 / (1.0 + jnp.exp(-x))


def _mod_kernel(c_ref, w_ref, b_ref, o_ref):
    x = c_ref[...]
    s = x * _sigmoid(x)
    acc = jnp.dot(s.astype(BF16), w_ref[0].astype(BF16), preferred_element_type=F32)
    o_ref[0] = acc + b_ref[0]


def modulation_all(cond8, w_mod, b_mod):
    depth, d, cols = w_mod.shape
    tn = _pick(cols, (1024, 512, 256, 128))
    return pl.pallas_call(
        _mod_kernel,
        grid=(depth, cols // tn),
        in_specs=[
            pl.BlockSpec((8, d), lambda l, j: (0, 0)),
            pl.BlockSpec((1, d, tn), lambda l, j: (l, 0, j)),
            pl.BlockSpec((1, 1, tn), lambda l, j: (l, 0, j)),
        ],
        out_specs=pl.BlockSpec((1, 8, tn), lambda l, j: (l, 0, j)),
        out_shape=jax.ShapeDtypeStruct((depth, 8, cols), F32),
        compiler_params=_cp("parallel", "parallel"),
        name="modulation",
    )(cond8, w_mod, b_mod.reshape(depth, 1, cols))


def _norm_body(h_ref, g_ref, sh_ref, sc_ref):
    x = h_ref[...]
    ms = jnp.mean(x * x, axis=-1, keepdims=True)
    y = x * lax.rsqrt(ms + EPS) * g_ref[...]
    return y * (1.0 + sc_ref[0]) + sh_ref[0]


def _norm_kernel(h_ref, g_ref, sh_ref, sc_ref, o_ref):
    o_ref[...] = _norm_body(h_ref, g_ref, sh_ref, sc_ref).astype(o_ref.dtype)


def _norm_route_kernel(h_ref, g_ref, sh_ref, sc_ref, r_ref, o_ref, comb_ref, *, n_experts):
    z = _norm_body(h_ref, g_ref, sh_ref, sc_ref)
    o_ref[...] = z.astype(o_ref.dtype)
    logits = jnp.dot(z, r_ref[...], preferred_element_type=F32, precision=HI)
    lane = lax.broadcasted_iota(jnp.int32, logits.shape, 1).astype(F32)
    neg = jnp.float32(-jnp.inf)
    lg = jnp.where(lane < n_experts, logits, neg)
    m1 = jnp.max(lg, axis=-1, keepdims=True)
    i1 = jnp.min(jnp.where(lg == m1, lane, float(LANES)), axis=-1, keepdims=True)
    lg2 = jnp.where(lane == i1, neg, lg)
    m2 = jnp.max(lg2, axis=-1, keepdims=True)
    i2 = jnp.min(jnp.where(lg2 == m2, lane, float(LANES)), axis=-1, keepdims=True)
    e = jnp.exp(m2 - m1)
    w1 = 1.0 / (1.0 + e)
    w2 = e / (1.0 + e)
    comb_ref[...] = jnp.where(lane == i1, w1, 0.0) + jnp.where(lane == i2, w2, 0.0)


def mod_norm(h, g, mod3, row_of_tile, sh_chunk, sc_chunk, tm, router=None):
    m, d = h.shape
    in_specs = [
        pl.BlockSpec((tm, d), lambda i: (i, 0)),
        pl.BlockSpec((1, d), lambda i: (0, 0)),
        pl.BlockSpec((1, 1, d), lambda i: (row_of_tile(i), 0, sh_chunk)),
        pl.BlockSpec((1, 1, d), lambda i: (row_of_tile(i), 0, sc_chunk)),
    ]
    args = [h, g.reshape(1, d), mod3, mod3]
    if router is None:
        return pl.pallas_call(
            _norm_kernel,
            grid=(m // tm,),
            in_specs=in_specs,
            out_specs=pl.BlockSpec((tm, d), lambda i: (i, 0)),
            out_shape=jax.ShapeDtypeStruct((m, d), BF16),
            compiler_params=_cp("parallel"),
            name="mod_norm",
        )(*args)
    n_experts = router.shape[1]
    rpad = jnp.pad(router, ((0, 0), (0, LANES - n_experts)))
    in_specs.append(pl.BlockSpec((d, LANES), lambda i: (0, 0)))
    return pl.pallas_call(
        functools.partial(_norm_route_kernel, n_experts=n_experts),
        grid=(m // tm,),
        in_specs=in_specs,
        out_specs=[pl.BlockSpec((tm, d), lambda i: (i, 0)), pl.BlockSpec((tm, LANES), lambda i: (i, 0))],
        out_shape=[jax.ShapeDtypeStruct((m, d), BF16), jax.ShapeDtypeStruct((m, LANES), F32)],
        compiler_params=_cp("parallel"),
        name="mod_norm_route",
    )(*args, rpad)


def _mm_kernel(x_ref, w_ref, o_ref):
    o_ref[...] = jnp.dot(x_ref[...], w_ref[...], preferred_element_type=F32).astype(o_ref.dtype)


def matmul(x, w, out_dtype=BF16, x_col=0, tm=None, tn=None):
    m = x.shape[0]
    k, n = w.shape
    tm = tm or _pick(m, (1024, 512, 256, 128))
    tn = tn or _pick(n, (1024, 512, 256, 128))
    return pl.pallas_call(
        _mm_kernel,
        grid=(m // tm, n // tn),
        in_specs=[
            pl.BlockSpec((tm, k), lambda i, j: (i, x_col)),
            pl.BlockSpec((k, tn), lambda i, j: (0, j)),
        ],
        out_specs=pl.BlockSpec((tm, tn), lambda i, j: (i, j)),
        out_shape=jax.ShapeDtypeStruct((m, n), out_dtype),
        compiler_params=_cp("parallel", "parallel"),
        name="matmul",
    )(x, w)


def _mm_res_kernel(x_ref, w_ref, res_ref, gate_ref, o_ref, acc_ref, *, nk):
    k = pl.program_id(2)
    part = jnp.dot(x_ref[...], w_ref[0], preferred_element_type=F32)
    if nk == 1:
        o_ref[...] = res_ref[...] + gate_ref[0] * part
        return

    @pl.when(k == 0)
    def _():
        acc_ref[...] = part

    @pl.when(jnp.logical_and(k > 0, k < nk - 1))
    def _():
        acc_ref[...] += part

    @pl.when(k == nk - 1)
    def _():
        o_ref[...] = res_ref[...] + gate_ref[0] * (acc_ref[...] + part)


def matmul_residual(x, w3, res, mod3, row_of_tile, gate_chunk, tm, tn, tk):
    m, n = res.shape
    n_e, kf, _ = w3.shape
    kpe = kf // tk
    nk = n_e * kpe
    return pl.pallas_call(
        functools.partial(_mm_res_kernel, nk=nk),
        grid=(m // tm, n // tn, nk),
        in_specs=[
            pl.BlockSpec((tm, tk), lambda i, j, k: (i, k)),
            pl.BlockSpec((1, tk, tn), lambda i, j, k: (k // kpe, k % kpe, j)),
            pl.BlockSpec((tm, tn), lambda i, j, k: (i, j)),
            pl.BlockSpec((1, 1, tn), lambda i, j, k: (row_of_tile(i), 0, gate_chunk * (n // tn) + j)),
        ],
        out_specs=pl.BlockSpec((tm, tn), lambda i, j, k: (i, j)),
        out_shape=jax.ShapeDtypeStruct((m, n), F32),
        scratch_shapes=[pltpu.VMEM((tm, tn), F32)],
        compiler_params=_cp("parallel", "parallel", "arbitrary"),
        name="matmul_residual",
    )(x, w3, res, mod3)


def _swiglu_kernel(z_ref, w1_ref, w3_ref, o_ref):
    z = z_ref[...]
    a = jnp.dot(z, w1_ref[0], preferred_element_type=F32)
    b = jnp.dot(z, w3_ref[0], preferred_element_type=F32)
    o_ref[...] = (a * _sigmoid(a) * b).astype(o_ref.dtype)


def _swiglu_comb_kernel(z_ref, w1_ref, w3_ref, comb_ref, o_ref, *, blocks_per_expert):
    z = z_ref[...]
    a = jnp.dot(z, w1_ref[0], preferred_element_type=F32)
    b = jnp.dot(z, w3_ref[0], preferred_element_type=F32)
    e = pl.program_id(1) // blocks_per_expert
    comb = comb_ref[...]
    lane = lax.broadcasted_iota(jnp.int32, comb.shape, 1)
    scale = jnp.sum(jnp.where(lane == e, comb, 0.0), axis=-1, keepdims=True)
    o_ref[...] = (a * _sigmoid(a) * b * scale).astype(o_ref.dtype)


def swiglu_up(z, w1, w3, tm, tn, comb=None):
    m, k = z.shape
    n_e, _, f = w1.shape
    bpe = f // tn
    in_specs = [
        pl.BlockSpec((tm, k), lambda i, j: (i, 0)),
        pl.BlockSpec((1, k, tn), lambda i, j: (j // bpe, 0, j % bpe)),
        pl.BlockSpec((1, k, tn), lambda i, j: (j // bpe, 0, j % bpe)),
    ]
    args = [z, w1, w3]
    if comb is None:
        kern = _swiglu_kernel
    else:
        kern = functools.partial(_swiglu_comb_kernel, blocks_per_expert=bpe)
        in_specs.append(pl.BlockSpec((tm, LANES), lambda i, j: (i, 0)))
        args.append(comb)
    return pl.pallas_call(
        kern,
        grid=(m // tm, n_e * bpe),
        in_specs=in_specs,
        out_specs=pl.BlockSpec((tm, tn), lambda i, j: (i, j)),
        out_shape=jax.ShapeDtypeStruct((m, n_e * f), BF16),
        compiler_params=_cp("parallel", "parallel"),
        name="swiglu_up",
    )(*args)


def _merge_kernel(xm_ref, wg_ref, y_ref, wb_ref, o_ref, acc_ref, *, n_branch):
    b = pl.program_id(2)
    gl = jnp.dot(xm_ref[...], wg_ref[...], preferred_element_type=F32)
    yb = jnp.dot(y_ref[0], wb_ref[0], preferred_element_type=F32)
    term = _sigmoid(gl) * yb

    @pl.when(b == 0)
    def _():
        acc_ref[...] = term

    @pl.when(jnp.logical_and(b > 0, b < n_branch - 1))
    def _():
        acc_ref[...] += term

    @pl.when(b == n_branch - 1)
    def _():
        o_ref[...] = (acc_ref[...] + term).astype(o_ref.dtype)


def merge_branches(xm, w_gate, ys, w_branch, tm, tn):
    m, d = xm.shape
    n_branch, bw, _ = w_branch.shape
    nj = d // tn
    return pl.pallas_call(
        functools.partial(_merge_kernel, n_branch=n_branch),
        grid=(m // tm, nj, n_branch),
        in_specs=[
            pl.BlockSpec((tm, d), lambda i, j, b: (i, 0)),
            pl.BlockSpec((d, tn), lambda i, j, b: (0, b * nj + j)),
            pl.BlockSpec((1, tm, bw), lambda i, j, b: (b, i, 0)),
            pl.BlockSpec((1, bw, tn), lambda i, j, b: (b, 0, j)),
        ],
        out_specs=pl.BlockSpec((tm, tn), lambda i, j, b: (i, j)),
        out_shape=jax.ShapeDtypeStruct((m, d), BF16),
        scratch_shapes=[pltpu.VMEM((tm, tn), F32)],
        compiler_params=_cp("parallel", "parallel", "arbitrary"),
        name="merge_branches",
    )(xm, w_gate, ys, w_branch)


def _qk_prep_kernel(pq_ref, pk_ref, gq_ref, gk_ref, cos_ref, sin_ref, grp_ref, qo_ref, ko_ref, *, rope, q_scale):
    tm = pq_ref.shape[0]
    lane = lax.broadcasted_iota(jnp.int32, (tm, LANES), 1)
    first_half = (lane & 16) == 0
    for src, g_ref, dst, scale in ((pq_ref, gq_ref, qo_ref, q_scale), (pk_ref, gk_ref, ko_ref, 1.0)):
        for j in range(src.shape[1] // LANES):
            sl = slice(j * LANES, (j + 1) * LANES)
            x = src[:, sl].astype(F32)
            ms = jnp.dot(x * x, grp_ref[...], preferred_element_type=F32, precision=HI)
            y = x * lax.rsqrt(ms + EPS) * g_ref[...]
            if rope:
                partner = jnp.where(first_half, pltpu.roll(y, LANES - 16, 1), pltpu.roll(y, 16, 1))
                y = y * cos_ref[...] + partner * sin_ref[...]
            dst[:, sl] = (y * scale).astype(dst.dtype)


def qk_prep(p, n_seq, bw, gq, gk, cos_t, sin_t, rope, q_scale, tm):
    m = p.shape[0]
    dh = gq.shape[0]
    reps = LANES // dh
    idx = jnp.arange(LANES)
    grp = jnp.where((idx[:, None] // dh) == (idx[None, :] // dh), 1.0 / dh, 0.0).astype(F32)
    tiles_per_seq = n_seq // tm
    return pl.pallas_call(
        functools.partial(_qk_prep_kernel, rope=rope, q_scale=q_scale),
        grid=(m // tm,),
        in_specs=[
            pl.BlockSpec((tm, bw), lambda i: (i, 0)),
            pl.BlockSpec((tm, bw), lambda i: (i, 1)),
            pl.BlockSpec((1, LANES), lambda i: (0, 0)),
            pl.BlockSpec((1, LANES), lambda i: (0, 0)),
            pl.BlockSpec((tm, LANES), lambda i: (i % tiles_per_seq, 0)),
            pl.BlockSpec((tm, LANES), lambda i: (i % tiles_per_seq, 0)),
            pl.BlockSpec((LANES, LANES), lambda i: (0, 0)),
        ],
        out_specs=[pl.BlockSpec((tm, bw), lambda i: (i, 0)), pl.BlockSpec((tm, bw), lambda i: (i, 0))],
        out_shape=[jax.ShapeDtypeStruct((m, bw), BF16), jax.ShapeDtypeStruct((m, bw), BF16)],
        compiler_params=_cp("parallel"),
        name="qk_prep",
    )(p, p, jnp.tile(gq, reps).reshape(1, LANES), jnp.tile(gk, reps).reshape(1, LANES), cos_t, sin_t, grp)


def rope_tables(n_seq, dh):
    nf = dh // 4
    pos = jnp.arange(n_seq)
    row = (pos // GRID_W).astype(F32)
    col = (pos % GRID_W).astype(F32)
    freqs = ROPE_THETA ** (-jnp.arange(nf, dtype=F32) / nf)
    lane = jnp.arange(LANES)
    d = lane % dh
    use_col = (d // (2 * nf)) == 1
    second = ((d // nf) % 2) == 1
    f = freqs[d % nf]
    ang = jnp.where(use_col[None, :], col[:, None], row[:, None]) * f[None, :]
    return jnp.cos(ang), jnp.where(second[None, :], 1.0, -1.0) * jnp.sin(ang)


def _att_kernel(lam_ref, q_ref, kt_ref, v_ref, g_ref, o_ref, m_ref, l_ref, acc_ref, *, n_chunks, ck, out_scale):
    tq = q_ref.shape[1]
    half = LANES // 2
    q = q_ref[0]
    lane = lax.broadcasted_iota(jnp.int32, (tq, LANES), 1)
    zero = jnp.zeros_like(q)
    qs = (jnp.where(lane < half, q, zero), jnp.where(lane >= half, q, zero))
    m_ref[...] = jnp.full(m_ref.shape, -jnp.inf, F32)
    l_ref[...] = jnp.zeros(l_ref.shape, F32)
    acc_ref[...] = jnp.zeros(acc_ref.shape, F32)

    def body(c, carry):
        kt = kt_ref[0, 0, c]
        v = v_ref[0, pl.ds(pl.multiple_of(c * ck, ck), ck), :]
        for ci in range(2):
            s = jnp.dot(qs[ci], kt, preferred_element_type=F32)
            m_old = m_ref[ci]
            m_new = jnp.maximum(m_old, jnp.max(s, axis=-1, keepdims=True))
            alpha = jnp.exp(m_old - m_new)
            p = jnp.exp(s - m_new)
            l_ref[ci] = alpha * l_ref[ci] + jnp.sum(p, axis=-1, keepdims=True)
            acc_ref[ci] = alpha * acc_ref[ci] + jnp.dot(p.astype(BF16), v, preferred_element_type=F32)
            m_ref[ci] = m_new
        return carry

    lax.fori_loop(0, n_chunks, body, 0)
    o = acc_ref[0] / l_ref[0] - lam_ref[0, 0] * (acc_ref[1] / l_ref[1])
    ms = jnp.mean(o * o, axis=-1, keepdims=True)
    o_ref[0] = (o * lax.rsqrt(ms + EPS) * g_ref[...] * out_scale).astype(o_ref.dtype)


def diff_attention(q, k, v, lam, g_sub, out_scale, tq, ck):
    b, nq, bw = q.shape
    nk = k.shape[1]
    heads = bw // LANES
    n_chunks = nk // ck
    kt = k.reshape(b, n_chunks, ck, heads, LANES).transpose(0, 3, 1, 4, 2)
    return pl.pallas_call(
        functools.partial(_att_kernel, n_chunks=n_chunks, ck=ck, out_scale=out_scale),
        grid=(b, heads, nq // tq),
        in_specs=[
            pl.BlockSpec(memory_space=pltpu.SMEM),
            pl.BlockSpec((1, tq, LANES), lambda bi, h, i: (bi, i, h)),
            pl.BlockSpec((1, 1, n_chunks, LANES, ck), lambda bi, h, i: (bi, h, 0, 0, 0)),
            pl.BlockSpec((1, nk, LANES), lambda bi, h, i: (bi, 0, h)),
            pl.BlockSpec((1, LANES), lambda bi, h, i: (0, 0)),
        ],
        out_specs=pl.BlockSpec((1, tq, LANES), lambda bi, h, i: (bi, i, h)),
        out_shape=jax.ShapeDtypeStruct((b, nq, bw), BF16),
        scratch_shapes=[
            pltpu.VMEM((2, tq, 1), F32),
            pltpu.VMEM((2, tq, 1), F32),
            pltpu.VMEM((2, tq, LANES), F32),
        ],
        compiler_params=_cp("parallel", "parallel", "parallel"),
        name="diff_attention",
    )(lam.reshape(1, 1).astype(F32), q, kt, v, g_sub.reshape(1, LANES))


def _local_kernel(pm_ref, pp_ref, pn_ref, cb_ref, cc_ref, ch_ref, ccp_ref, chp_ref, ccn_ref, chn_ref,
                  pw_ref, ps_ref, cw_ref, yp_ref, yc_ref, scr_ref, *, n_seq, tiles_per_seq):
    ts, bw = pm_ref.shape
    it = pl.program_id(0) % tiles_per_seq
    has_prev = jnp.where(it > 0, 1.0, 0.0).astype(F32)
    has_next = jnp.where(it < tiles_per_seq - 1, 1.0, 0.0).astype(F32)
    gw = bw // len(POOL_WINDOWS)
    pos = it * ts + lax.broadcasted_iota(jnp.int32, (ts, 1), 0)

    scr_ref[0:HALO, :] = pp_ref[...].astype(F32) * has_prev
    scr_ref[HALO:HALO + ts, :] = pm_ref[...].astype(F32)
    scr_ref[HALO + ts:2 * HALO + ts, :] = pn_ref[...].astype(F32) * has_next
    for g, w in enumerate(POOL_WINDOWS):
        sl = slice(g * gw, (g + 1) * gw)
        lo, hi = w // 2, w - 1 - w // 2
        tot = None
        for off in range(-lo, hi + 1):
            part = scr_ref[HALO + off:HALO + off + ts, sl]
            tot = part if tot is None else tot + part
        cnt = (jnp.minimum(pos + hi, n_seq - 1) - jnp.maximum(pos - lo, 0) + 1).astype(F32)
        pooled = tot / cnt - scr_ref[HALO:HALO + ts, sl]
        y = jnp.dot(pooled.astype(BF16), pw_ref[g], preferred_element_type=F32)
        yp_ref[:, sl] = (y * ps_ref[:, sl]).astype(yp_ref.dtype)

    scr_ref[0:HALO, :] = ccp_ref[...].astype(F32) * chp_ref[...].astype(F32) * has_prev
    scr_ref[HALO:HALO + ts, :] = cc_ref[...].astype(F32) * ch_ref[...].astype(F32)
    scr_ref[HALO + ts:2 * HALO + ts, :] = ccn_ref[...].astype(F32) * chn_ref[...].astype(F32) * has_next
    conv = None
    for j in range(CONV_K):
        off = j - CONV_K // 2
        term = cw_ref[j:j + 1, :] * scr_ref[HALO + off:HALO + off + ts, :]
        conv = term if conv is None else conv + term
    yc_ref[...] = (cb_ref[...].astype(F32) * conv).astype(yc_ref.dtype)


def local_mixers(p, n_seq, bw, pool_w, pool_scale, conv_w, ts):
    m = p.shape[0]
    tps = n_seq // ts
    r = ts // HALO
    last_halo = m // HALO - 1

    def main(cb):
        return pl.BlockSpec((ts, bw), lambda i: (i, cb))

    def prev(cb):
        return pl.BlockSpec((HALO, bw), lambda i: (jnp.maximum(i * r - 1, 0), cb))

    def nxt(cb):
        return pl.BlockSpec((HALO, bw), lambda i: (jnp.minimum((i + 1) * r, last_halo), cb))

    n_g = len(POOL_WINDOWS)
    return pl.pallas_call(
        functools.partial(_local_kernel, n_seq=n_seq, tiles_per_seq=tps),
        grid=(m // ts,),
        in_specs=[main(3), prev(3), nxt(3), main(5), main(6), main(7), prev(6), prev(7), nxt(6), nxt(7),
                  pl.BlockSpec((n_g, bw // n_g, bw // n_g), lambda i: (0, 0, 0)),
                  pl.BlockSpec((1, bw), lambda i: (0, 0)),
                  pl.BlockSpec((CONV_K, bw), lambda i: (0, 0))],
        out_specs=[pl.BlockSpec((ts, bw), lambda i: (i, 0)), pl.BlockSpec((ts, bw), lambda i: (i, 0))],
        out_shape=[jax.ShapeDtypeStruct((m, bw), BF16), jax.ShapeDtypeStruct((m, bw), BF16)],
        scratch_shapes=[pltpu.VMEM((ts + 2 * HALO, bw), F32)],
        compiler_params=_cp("parallel"),
        name="local_mixers",
    )(p, p, p, p, p, p, p, p, p, p, pool_w.astype(BF16), pool_scale.reshape(1, bw), conv_w)


def _dft_cos_sin(n, scale):
    k = jnp.arange(n, dtype=jnp.int32)
    ang = ((k[:, None] * k[None, :]) % n).astype(F32) * (2.0 * math.pi / n)
    return jnp.cos(ang) * scale, jnp.sin(ang) * scale


def channel_dft_matrix(bw):
    gw = bw // FOURIER_GROUPS
    c, s = _dft_cos_sin(gw, gw ** -0.5)
    eye = jnp.eye(FOURIER_GROUPS, dtype=F32)
    return jnp.concatenate([jnp.kron(eye, c), jnp.kron(eye, s)], axis=1).astype(BF16)


def _fft_stage2_kernel(p_ref, q_ref, cw_ref, sw_ref, f2_ref, o_ref, b_ref):
    n2 = p_ref.shape[2]
    bw = o_ref.shape[2]
    cw = cw_ref[0]
    sw = sw_ref[0]
    for j in range(bw // LANES):
        lo = slice(j * LANES, (j + 1) * LANES)
        hi = slice(bw + j * LANES, bw + (j + 1) * LANES)
        ar = p_ref[0, 0, :, lo].astype(F32) - q_ref[0, 0, :, hi].astype(F32)
        ai = -(p_ref[0, 0, :, hi].astype(F32) + q_ref[0, 0, :, lo].astype(F32))
        b_ref[0:n2, lo] = (ar * cw + ai * sw).astype(BF16)
        b_ref[n2:2 * n2, lo] = (ai * cw - ar * sw).astype(BF16)
    o_ref[0] = jnp.dot(f2_ref[...], b_ref[...], preferred_element_type=F32).astype(o_ref.dtype)


def fourier_seq_two_stage(z, b, n_seq, bw, n1, n2):
    c1, s1 = _dft_cos_sin(n1, n1 ** -0.5)
    f1 = jnp.concatenate([c1, s1], axis=0).astype(BF16)
    c2, s2 = _dft_cos_sin(n2, n2 ** -0.5)
    f2 = jnp.concatenate([c2, s2], axis=1).astype(BF16)
    k1 = jnp.arange(n1, dtype=jnp.int32)
    t2 = jnp.arange(n2, dtype=jnp.int32)
    ang = (k1[:, None] * t2[None, :]).astype(F32) * (2.0 * math.pi / n_seq)
    cw = jnp.broadcast_to(jnp.cos(ang)[:, :, None], (n1, n2, LANES))
    sw = jnp.broadcast_to(jnp.sin(ang)[:, :, None], (n1, n2, LANES))
    zb = z.reshape(b, n1, n2 * 2 * bw)
    tn1 = _pick(n2 * 2 * bw, (8192, 4096, 2048))
    a = jnp.stack([matmul(f1, zb[i], tm=2 * n1, tn=tn1) for i in range(b)])
    a = a.reshape(b, 2 * n1, n2, 2 * bw)
    out = pl.pallas_call(
        _fft_stage2_kernel,
        grid=(b, n1),
        in_specs=[
            pl.BlockSpec((1, 1, n2, 2 * bw), lambda bi, k: (bi, k, 0, 0)),
            pl.BlockSpec((1, 1, n2, 2 * bw), lambda bi, k: (bi, n1 + k, 0, 0)),
            pl.BlockSpec((1, n2, LANES), lambda bi, k: (k, 0, 0)),
            pl.BlockSpec((1, n2, LANES), lambda bi, k: (k, 0, 0)),
            pl.BlockSpec((n2, 2 * n2), lambda bi, k: (0, 0)),
        ],
        out_specs=pl.BlockSpec((1, n2, bw), lambda bi, k: (bi, 0, k)),
        out_shape=jax.ShapeDtypeStruct((b, n2, n1 * bw), BF16),
        scratch_shapes=[pltpu.VMEM((2 * n2, bw), BF16)],
        compiler_params=_cp("parallel", "parallel"),
        name="fft_stage2",
    )(a, a, cw, sw, f2)
    return out.reshape(b * n_seq, bw)


def fourier_seq_dense(z, b, n_seq, bw):
    c, s = _dft_cos_sin(n_seq, n_seq ** -0.5)
    f = jnp.concatenate([c, -s], axis=1).astype(BF16)
    zb = z.reshape(b, n_seq, 2 * bw)
    outs = [matmul(f, jnp.concatenate([zb[i, :, :bw], zb[i, :, bw:]], axis=0)) for i in range(b)]
    return jnp.concatenate(outs, axis=0)


def _fft_factors(n_seq):
    n2 = LANES
    n1 = n_seq // n2
    return n1, n2


def _pad_last(w, mult):
    pad = (-w.shape[-1]) % mult
    return jnp.pad(w, [(0, 0)] * (w.ndim - 1) + [(0, pad)]) if pad else w


def _pad_rows(w, mult):
    pad = (-w.shape[-2]) % mult
    return jnp.pad(w, [(0, 0)] * (w.ndim - 2) + [(0, pad), (0, 0)]) if pad else w


def kernel(x, c, ctx, c_ctx, w_mod, b_mod, norm_mix, norm_ffn, w_in, w_gate, q_norm, k_norm, lambda_q1, lambda_k1,
           lambda_q2, lambda_k2, subln, pool_w, pool_scale, conv_w, w_branch, w_out, ffn_w1, ffn_w3, ffn_w2,
           router, moe_w1, moe_w3, moe_w2):
    b, n_lat, d = x.shape
    n_ctx = ctx.shape[1]
    depth = w_mod.shape[0]
    bw = d // 4
    dh = q_norm.shape[1]
    m_lat = b * n_lat
    m_ctx = b * n_ctx

    cond8 = jnp.zeros((8, d), F32).at[:b].set(c).at[b].set(c_ctx)
    mod3 = modulation_all(cond8, w_mod, b_mod).reshape(depth * 8, 1, N_MOD * d)

    rope_lat = rope_tables(n_lat, dh)
    rope_ctx = rope_tables(n_ctx, dh)
    w_cdft = channel_dft_matrix(bw)
    n1, n2 = _fft_factors(n_lat)

    h = x.reshape(m_lat, d)
    hc = ctx.reshape(m_ctx, d)

    for l in range(depth):
        last = l == depth - 1
        lam_init = 0.8 - 0.6 * math.exp(-0.3 * l)
        lam = (jnp.exp(jnp.sum(lambda_q1[l] * lambda_k1[l])) - jnp.exp(jnp.sum(lambda_q2[l] * lambda_k2[l]))
               + lam_init)
        w_in_l = w_in[l].astype(BF16)
        w_gate_l = w_gate[l].astype(BF16)
        w_branch_l = w_branch[l].astype(BF16)
        w_out_l = w_out[l].astype(BF16)[None]

        def lat_row(tm, l=l):
            return lambda i: l * 8 + i // (n_lat // tm)

        def ctx_row(tm, l=l):
            return lambda i: l * 8 + b

        def mixer_inputs(h2, n_seq, row_fn, rope, tables):
            tm_norm = _pick(n_seq, (256, 128))
            xm_ = mod_norm(h2, norm_mix[l], mod3, row_fn(tm_norm), 0, 1, tm_norm)
            p_ = matmul(xm_, w_in_l, tm=_pick(n_seq, (1024, 512, 256)))
            q_, k_ = qk_prep(p_, n_seq, bw, q_norm[l], k_norm[l], tables[0], tables[1], rope=rope,
                             q_scale=dh ** -0.5, tm=_pick(n_seq, (512, 256)))
            return xm_, p_, q_, k_, p_[:, 2 * bw:3 * bw]

        def mixer_output(h2, n_seq, row_fn, xm_, p_, att_, four_):
            pool_, conv_ = local_mixers(p_, n_seq, bw, pool_w[l], pool_scale[l], conv_w[l],
                                        ts=_pick(n_seq, (512, 256)))
            ys = jnp.stack([att_, pool_, four_, conv_])
            tm = _pick(n_seq, (1024, 512, 256))
            merged = merge_branches(xm_, w_gate_l, ys, w_branch_l, tm=tm, tn=512)
            return matmul_residual(merged, w_out_l, h2, mod3, row_fn(tm), 2, tm=tm, tn=512, tk=d)

        xcm, pc, q_c, k_c, v_c = mixer_inputs(hc, n_ctx, ctx_row, False, rope_ctx)
        xm, p, q, k, v = mixer_inputs(h, n_lat, lat_row, True, rope_lat)

        k_all = jnp.concatenate([k_c.reshape(b, n_ctx, bw), k.reshape(b, n_lat, bw)], axis=1)
        v_all = jnp.concatenate([v_c.reshape(b, n_ctx, bw), v.reshape(b, n_lat, bw)], axis=1)
        nk = n_ctx + n_lat
        ck = _pick(nk, (768, 384, 256, 128))
        att = diff_attention(q.reshape(b, n_lat, bw), k_all, v_all, lam, subln[l], 1.0 - lam_init,
                             tq=_pick(n_lat, (256, 128)), ck=ck).reshape(m_lat, bw)
        z_lat = matmul(p, w_cdft, x_col=4, tm=_pick(n_lat, (1024, 512, 256)))
        four = fourier_seq_two_stage(z_lat, b, n_lat, bw, n1, n2)
        h_new = mixer_output(h, n_lat, lat_row, xm, p, att, four)

        if not last:
            att_c = diff_attention(q_c.reshape(b, n_ctx, bw), k_c.reshape(b, n_ctx, bw), v_c.reshape(b, n_ctx, bw),
                                   lam, subln[l], 1.0 - lam_init, tq=_pick(n_ctx, (256, 128)),
                                   ck=_pick(n_ctx, (256, 128))).reshape(m_ctx, bw)
            z_ctx = matmul(pc, w_cdft, x_col=4, tm=_pick(n_ctx, (256, 128)))
            four_c = fourier_seq_dense(z_ctx, b, n_ctx, bw)
            hc = mixer_output(hc, n_ctx, ctx_row, xcm, pc, att_c, four_c)
        h = h_new

        if l % 2 == 0:
            w1 = _pad_last(ffn_w1[l // 2].astype(BF16), 1024)[None]
            w3 = _pad_last(ffn_w3[l // 2].astype(BF16), 1024)[None]
            w2 = _pad_rows(ffn_w2[l // 2].astype(BF16), 1024)[None]
            rt = None
        else:
            w1 = _pad_last(moe_w1[l // 2].astype(BF16), 256)
            w3 = _pad_last(moe_w3[l // 2].astype(BF16), 256)
            w2 = _pad_rows(moe_w2[l // 2].astype(BF16), 256)
            rt = router[l // 2]
        f_pad = w1.shape[2]
        tn_up = _pick(f_pad, (512, 256))
        tk_dn = _pick(f_pad, (2816, 1024, 512, 256))

        def channel_mix(h2, n_seq, row_fn):
            tm_norm = _pick(n_seq, (256, 128))
            zn = mod_norm(h2, norm_ffn[l], mod3, row_fn(tm_norm), 3, 4, tm_norm, router=rt)
            z_, comb = zn if rt is not None else (zn, None)
            tm = _pick(n_seq, (1024, 512, 256))
            hid = swiglu_up(z_, w1, w3, tm=tm, tn=tn_up, comb=comb)
            return matmul_residual(hid, w2, h2, mod3, row_fn(tm), 5, tm=tm, tn=1024, tk=tk_dn)

        h = channel_mix(h, n_lat, lat_row)
        if not last:
            hc = channel_mix(hc, n_ctx, ctx_row)

    return h.reshape(b, n_lat, d)
```

```python
import functools
import math

import jax
import jax.numpy as jnp
from jax import lax
from jax.experimental import pallas as pl
from jax.experimental.pallas import tpu as pltpu

F32 = jnp.float32
BF16 = jnp.bfloat16

GRID_W = 64
ROPE_THETA = 10000.0
POOL_WINDOWS = (2, 4, 8, 16)
FOURIER_GROUPS = 4
CONV_K = 3
N_MOD = 6
EPS = 1e-6
LANES = 128
HALO = 16
VMEM_LIMIT = 56 * 1024 * 1024
HI = lax.Precision.HIGHEST
LOG2_E = 1.4426950408889634


def _cp(*sem, vmem=VMEM_LIMIT):
    return pltpu.CompilerParams(dimension_semantics=sem, vmem_limit_bytes=vmem)


def _pick(n, prefs):
    for t in prefs:
        if n % t == 0:
            return t
    return n


def _sigmoid(x):
    return 1.0 / (1.0 + jnp.exp(-x))


def _mod_kernel(c_ref, w_ref, b_ref, o_ref):
    x = c_ref[...]
    s = x * _sigmoid(x)
    acc = jnp.dot(s.astype(BF16), w_ref[0].astype(BF16), preferred_element_type=F32)
    o_ref[0] = acc + b_ref[0]


def modulation_all(cond8, w_mod, b_mod):
    depth, d, cols = w_mod.shape
    tn = _pick(cols, (1024, 512, 256, 128))
    return pl.pallas_call(
        _mod_kernel,
        grid=(depth, cols // tn),
        in_specs=[
            pl.BlockSpec((8, d), lambda l, j: (0, 0)),
            pl.BlockSpec((1, d, tn), lambda l, j: (l, 0, j)),
            pl.BlockSpec((1, 1, tn), lambda l, j: (l, 0, j)),
        ],
        out_specs=pl.BlockSpec((1, 8, tn), lambda l, j: (l, 0, j)),
        out_shape=jax.ShapeDtypeStruct((depth, 8, cols), F32),
        compiler_params=_cp("parallel", "parallel"),
        name="modulation",
    )(cond8, w_mod, b_mod.reshape(depth, 1, cols))


def _norm_body(h_ref, g_ref, sh_ref, sc_ref):
    x = h_ref[...]
    ms = jnp.mean(x * x, axis=-1, keepdims=True)
    y = x * lax.rsqrt(ms + EPS) * g_ref[...]
    return y * (1.0 + sc_ref[0]) + sh_ref[0]


def _norm_kernel(h_ref, g_ref, sh_ref, sc_ref, o_ref):
    o_ref[...] = _norm_body(h_ref, g_ref, sh_ref, sc_ref).astype(o_ref.dtype)


def _norm_route_kernel(h_ref, g_ref, sh_ref, sc_ref, r_ref, o_ref, comb_ref, *, n_experts):
    z = _norm_body(h_ref, g_ref, sh_ref, sc_ref)
    o_ref[...] = z.astype(o_ref.dtype)
    logits = jnp.dot(z, r_ref[...], preferred_element_type=F32, precision=HI)
    lane = lax.broadcasted_iota(jnp.int32, logits.shape, 1).astype(F32)
    neg = jnp.float32(-jnp.inf)
    lg = jnp.where(lane < n_experts, logits, neg)
    m1 = jnp.max(lg, axis=-1, keepdims=True)
    i1 = jnp.min(jnp.where(lg == m1, lane, float(LANES)), axis=-1, keepdims=True)
    lg2 = jnp.where(lane == i1, neg, lg)
    m2 = jnp.max(lg2, axis=-1, keepdims=True)
    i2 = jnp.min(jnp.where(lg2 == m2, lane, float(LANES)), axis=-1, keepdims=True)
    e = jnp.exp(m2 - m1)
    w1 = 1.0 / (1.0 + e)
    w2 = e / (1.0 + e)
    comb_ref[...] = jnp.where(lane == i1, w1, 0.0) + jnp.where(lane == i2, w2, 0.0)


def mod_norm(h, g, mod3, row_of_tile, sh_chunk, sc_chunk, tm, router=None):
    m, d = h.shape
    in_specs = [
        pl.BlockSpec((tm, d), lambda i: (i, 0)),
        pl.BlockSpec((1, d), lambda i: (0, 0)),
        pl.BlockSpec((1, 1, d), lambda i: (row_of_tile(i), 0, sh_chunk)),
        pl.BlockSpec((1, 1, d), lambda i: (row_of_tile(i), 0, sc_chunk)),
    ]
    args = [h, g.reshape(1, d), mod3, mod3]
    if router is None:
        return pl.pallas_call(
            _norm_kernel,
            grid=(m // tm,),
            in_specs=in_specs,
            out_specs=pl.BlockSpec((tm, d), lambda i: (i, 0)),
            out_shape=jax.ShapeDtypeStruct((m, d), BF16),
            compiler_params=_cp("parallel"),
            name="mod_norm",
        )(*args)
    n_experts = router.shape[1]
    rpad = jnp.pad(router, ((0, 0), (0, LANES - n_experts)))
    in_specs.append(pl.BlockSpec((d, LANES), lambda i: (0, 0)))
    return pl.pallas_call(
        functools.partial(_norm_route_kernel, n_experts=n_experts),
        grid=(m // tm,),
        in_specs=in_specs,
        out_specs=[pl.BlockSpec((tm, d), lambda i: (i, 0)), pl.BlockSpec((tm, LANES), lambda i: (i, 0))],
        out_shape=[jax.ShapeDtypeStruct((m, d), BF16), jax.ShapeDtypeStruct((m, LANES), F32)],
        compiler_params=_cp("parallel"),
        name="mod_norm_route",
    )(*args, rpad)


def _mm_kernel(x_ref, w_ref, o_ref):
    o_ref[...] = jnp.dot(x_ref[...], w_ref[...], preferred_element_type=F32).astype(o_ref.dtype)


def matmul(x, w, out_dtype=BF16, x_col=0, tm=None, tn=None):
    m = x.shape[0]
    k, n = w.shape
    tm = tm or _pick(m, (1024, 512, 256, 128))
    tn = tn or _pick(n, (1024, 512, 256, 128))
    return pl.pallas_call(
        _mm_kernel,
        grid=(m // tm, n // tn),
        in_specs=[
            pl.BlockSpec((tm, k), lambda i, j: (i, x_col)),
            pl.BlockSpec((k, tn), lambda i, j: (0, j)),
        ],
        out_specs=pl.BlockSpec((tm, tn), lambda i, j: (i, j)),
        out_shape=jax.ShapeDtypeStruct((m, n), out_dtype),
        compiler_params=_cp("parallel", "parallel"),
        name="matmul",
    )(x, w)


def _mm_res_kernel(x_ref, w_ref, res_ref, gate_ref, o_ref, acc_ref, *, nk):
    k = pl.program_id(2)
    part = jnp.dot(x_ref[...], w_ref[0], preferred_element_type=F32)
    if nk == 1:
        o_ref[...] = res_ref[...] + gate_ref[0] * part
        return

    @pl.when(k == 0)
    def _():
        acc_ref[...] = part

    @pl.when(jnp.logical_and(k > 0, k < nk - 1))
    def _():
        acc_ref[...] += part

    @pl.when(k == nk - 1)
    def _():
        o_ref[...] = res_ref[...] + gate_ref[0] * (acc_ref[...] + part)


def matmul_residual(x, w3, res, mod3, row_of_tile, gate_chunk, tm, tn, tk):
    m, n = res.shape
    n_e, kf, _ = w3.shape
    kpe = kf // tk
    nk = n_e * kpe
    return pl.pallas_call(
        functools.partial(_mm_res_kernel, nk=nk),
        grid=(m // tm, n // tn, nk),
        in_specs=[
            pl.BlockSpec((tm, tk), lambda i, j, k: (i, k)),
            pl.BlockSpec((1, tk, tn), lambda i, j, k: (k // kpe, k % kpe, j)),
            pl.BlockSpec((tm, tn), lambda i, j, k: (i, j)),
            pl.BlockSpec((1, 1, tn), lambda i, j, k: (row_of_tile(i), 0, gate_chunk * (n // tn) + j)),
        ],
        out_specs=pl.BlockSpec((tm, tn), lambda i, j, k: (i, j)),
        out_shape=jax.ShapeDtypeStruct((m, n), F32),
        scratch_shapes=[pltpu.VMEM((tm, tn), F32)],
        compiler_params=_cp("parallel", "parallel", "arbitrary"),
        name="matmul_residual",
    )(x, w3, res, mod3)


def _swiglu_kernel(z_ref, w1_ref, w3_ref, o_ref):
    z = z_ref[...]
    a = jnp.dot(z, w1_ref[0], preferred_element_type=F32)
    b = jnp.dot(z, w3_ref[0], preferred_element_type=F32)
    o_ref[...] = (a * _sigmoid(a) * b).astype(o_ref.dtype)


def _swiglu_comb_kernel(z_ref, w1_ref, w3_ref, comb_ref, o_ref, *, blocks_per_expert):
    z = z_ref[...]
    a = jnp.dot(z, w1_ref[0], preferred_element_type=F32)
    b = jnp.dot(z, w3_ref[0], preferred_element_type=F32)
    e = pl.program_id(1) // blocks_per_expert
    comb = comb_ref[...]
    lane = lax.broadcasted_iota(jnp.int32, comb.shape, 1)
    scale = jnp.sum(jnp.where(lane == e, comb, 0.0), axis=-1, keepdims=True)
    o_ref[...] = (a * _sigmoid(a) * b * scale).astype(o_ref.dtype)


def swiglu_up(z, w1, w3, tm, tn, comb=None):
    m, k = z.shape
    n_e, _, f = w1.shape
    bpe = f // tn
    in_specs = [
        pl.BlockSpec((tm, k), lambda i, j: (i, 0)),
        pl.BlockSpec((1, k, tn), lambda i, j: (j // bpe, 0, j % bpe)),
        pl.BlockSpec((1, k, tn), lambda i, j: (j // bpe, 0, j % bpe)),
    ]
    args = [z, w1, w3]
    if comb is None:
        kern = _swiglu_kernel
    else:
        kern = functools.partial(_swiglu_comb_kernel, blocks_per_expert=bpe)
        in_specs.append(pl.BlockSpec((tm, LANES), lambda i, j: (i, 0)))
        args.append(comb)
    return pl.pallas_call(
        kern,
        grid=(m // tm, n_e * bpe),
        in_specs=in_specs,
        out_specs=pl.BlockSpec((tm, tn), lambda i, j: (i, j)),
        out_shape=jax.ShapeDtypeStruct((m, n_e * f), BF16),
        compiler_params=_cp("parallel", "parallel"),
        name="swiglu_up",
    )(*args)


MOE_TILE = 1024
TOP_K = 2
TOKEN_CHUNK = 128


def _gather_kernel(idx_ref, src_ref, dst_ref, sem, *, rows):
    base = pl.program_id(0) * rows

    def issue(r, carry):
        pltpu.make_async_copy(src_ref.at[idx_ref[base + r]], dst_ref.at[base + r], sem).start()
        return carry

    lax.fori_loop(0, rows, issue, 0)
    pltpu.make_async_copy(src_ref.at[pl.ds(0, rows)], dst_ref.at[pl.ds(base, rows)], sem).wait()


def gather_rows(src, idx, rows_per_step):
    n_src, d = src.shape
    n_out = idx.shape[0]
    src3 = src.reshape(n_src, d // TOKEN_CHUNK, TOKEN_CHUNK)
    out = pl.pallas_call(
        functools.partial(_gather_kernel, rows=rows_per_step),
        grid_spec=pltpu.PrefetchScalarGridSpec(
            num_scalar_prefetch=1,
            grid=(n_out // rows_per_step,),
            in_specs=[pl.BlockSpec(memory_space=pl.ANY)],
            out_specs=pl.BlockSpec(memory_space=pl.ANY),
            scratch_shapes=[pltpu.SemaphoreType.DMA(())],
        ),
        out_shape=jax.ShapeDtypeStruct((n_out, d // TOKEN_CHUNK, TOKEN_CHUNK), src.dtype),
        compiler_params=_cp("arbitrary"),
        name="gather_rows",
    )(idx, src3)
    return out.reshape(n_out, d)


def _moe_up_kernel(te_ref, nu_ref, z_ref, w1_ref, w3_ref, ws_ref, o_ref):
    used = pl.program_id(0) < nu_ref[0]

    @pl.when(used)
    def _():
        z = z_ref[...]
        a = jnp.dot(z, w1_ref[0], preferred_element_type=F32)
        b = jnp.dot(z, w3_ref[0], preferred_element_type=F32)
        scale = jnp.concatenate([ws_ref[...]] * (o_ref.shape[1] // LANES), axis=1)
        o_ref[...] = (a * _sigmoid(a) * b * scale).astype(o_ref.dtype)

    @pl.when(jnp.logical_not(used))
    def _():
        o_ref[...] = jnp.zeros(o_ref.shape, o_ref.dtype)


def _moe_down_kernel(te_ref, nu_ref, x_ref, w_ref, o_ref):
    used = pl.program_id(0) < nu_ref[0]

    @pl.when(used)
    def _():
        o_ref[...] = jnp.dot(x_ref[...], w_ref[0], preferred_element_type=F32).astype(o_ref.dtype)

    @pl.when(jnp.logical_not(used))
    def _():
        o_ref[...] = jnp.zeros(o_ref.shape, o_ref.dtype)


def _combine_kernel(h_ref, ya_ref, yb_ref, gate_ref, o_ref):
    o_ref[...] = h_ref[...] + gate_ref[0] * (ya_ref[0].astype(F32) + yb_ref[0].astype(F32))


def moe_sparse(z, comb, n_experts, w1, w3, w2, h2, mod3, row_of_tile, gate_chunk, tm):
    m, d = z.shape
    f = w1.shape[2]
    n_tiles = (TOP_K * m) // MOE_TILE + n_experts
    n_rows = n_tiles * MOE_TILE
    gather_step = _pick(m, (512, 256, 128))

    cw = comb[:, :n_experts]
    sel = cw > 0.0
    seli = sel.astype(jnp.int32)
    rank = jnp.cumsum(seli, axis=0) - seli
    cnt = jnp.sum(seli, axis=0)
    padded = ((cnt + MOE_TILE - 1) // MOE_TILE) * MOE_TILE
    gend = jnp.cumsum(padded)
    slot = (gend - padded)[None, :] + rank
    flat_slot = jnp.where(sel, slot, n_rows).reshape(-1)
    tok = jnp.broadcast_to(jnp.arange(m, dtype=jnp.int32)[:, None], (m, n_experts)).reshape(-1)
    src_row = jnp.zeros((n_rows,), jnp.int32).at[flat_slot].set(tok, mode="drop")
    w_slot = jnp.zeros((n_rows,), F32).at[flat_slot].set(cw.reshape(-1), mode="drop")
    w_slot = jnp.broadcast_to(w_slot[:, None], (n_rows, LANES))
    tile_start = jnp.arange(n_tiles, dtype=jnp.int32) * MOE_TILE
    tile_expert = jnp.minimum(jnp.sum((tile_start[:, None] >= gend[None, :]).astype(jnp.int32), axis=1),
                              n_experts - 1).astype(jnp.int32)
    n_used = (gend[-1:] // MOE_TILE).astype(jnp.int32)
    _, top_e = lax.top_k(cw, TOP_K)
    slot_ab = jnp.take_along_axis(jnp.where(sel, slot, n_rows - 1), top_e, axis=1).astype(jnp.int32)

    zs = gather_rows(z, src_row, gather_step)
    tn_up = _pick(f, (256, 128))
    hid = pl.pallas_call(
        _moe_up_kernel,
        grid_spec=pltpu.PrefetchScalarGridSpec(
            num_scalar_prefetch=2,
            grid=(n_tiles, f // tn_up),
            in_specs=[
                pl.BlockSpec((MOE_TILE, d), lambda i, j, te, nu: (i, 0)),
                pl.BlockSpec((1, d, tn_up), lambda i, j, te, nu: (te[i], 0, j)),
                pl.BlockSpec((1, d, tn_up), lambda i, j, te, nu: (te[i], 0, j)),
                pl.BlockSpec((MOE_TILE, LANES), lambda i, j, te, nu: (i, 0)),
            ],
            out_specs=pl.BlockSpec((MOE_TILE, tn_up), lambda i, j, te, nu: (i, j)),
        ),
        out_shape=jax.ShapeDtypeStruct((n_rows, f), BF16),
        compiler_params=_cp("parallel", "parallel"),
        name="moe_up",
    )(tile_expert, n_used, zs, w1, w3, w_slot)
    tn_dn = _pick(d, (1024, 512, 256))
    ys = pl.pallas_call(
        _moe_down_kernel,
        grid_spec=pltpu.PrefetchScalarGridSpec(
            num_scalar_prefetch=2,
            grid=(n_tiles, d // tn_dn),
            in_specs=[
                pl.BlockSpec((MOE_TILE, f), lambda i, j, te, nu: (i, 0)),
                pl.BlockSpec((1, f, tn_dn), lambda i, j, te, nu: (te[i], 0, j)),
            ],
            out_specs=pl.BlockSpec((MOE_TILE, tn_dn), lambda i, j, te, nu: (i, j)),
        ),
        out_shape=jax.ShapeDtypeStruct((n_rows, d), BF16),
        compiler_params=_cp("parallel", "parallel"),
        name="moe_down",
    )(tile_expert, n_used, hid, w2)
    yab = gather_rows(ys, slot_ab.T.reshape(-1), gather_step).reshape(TOP_K, m, d)
    return pl.pallas_call(
        _combine_kernel,
        grid=(m // tm,),
        in_specs=[
            pl.BlockSpec((tm, d), lambda i: (i, 0)),
            pl.BlockSpec((1, tm, d), lambda i: (0, i, 0)),
            pl.BlockSpec((1, tm, d), lambda i: (1, i, 0)),
            pl.BlockSpec((1, 1, d), lambda i: (row_of_tile(i), 0, gate_chunk)),
        ],
        out_specs=pl.BlockSpec((tm, d), lambda i: (i, 0)),
        out_shape=jax.ShapeDtypeStruct((m, d), F32),
        compiler_params=_cp("parallel"),
        name="moe_combine",
    )(h2, yab, yab, mod3)


def _merge_kernel(xm_ref, wg_ref, y_ref, wb_ref, o_ref, acc_ref, *, n_branch):
    b = pl.program_id(2)
    gl = jnp.dot(xm_ref[...], wg_ref[...], preferred_element_type=F32)
    yb = jnp.dot(y_ref[0], wb_ref[0], preferred_element_type=F32)
    term = _sigmoid(gl) * yb

    @pl.when(b == 0)
    def _():
        acc_ref[...] = term

    @pl.when(jnp.logical_and(b > 0, b < n_branch - 1))
    def _():
        acc_ref[...] += term

    @pl.when(b == n_branch - 1)
    def _():
        o_ref[...] = (acc_ref[...] + term).astype(o_ref.dtype)


def merge_branches(xm, w_gate, ys, w_branch, tm, tn):
    m, d = xm.shape
    n_branch, bw, _ = w_branch.shape
    nj = d // tn
    return pl.pallas_call(
        functools.partial(_merge_kernel, n_branch=n_branch),
        grid=(m // tm, nj, n_branch),
        in_specs=[
            pl.BlockSpec((tm, d), lambda i, j, b: (i, 0)),
            pl.BlockSpec((d, tn), lambda i, j, b: (0, b * nj + j)),
            pl.BlockSpec((1, tm, bw), lambda i, j, b: (b, i, 0)),
            pl.BlockSpec((1, bw, tn), lambda i, j, b: (b, 0, j)),
        ],
        out_specs=pl.BlockSpec((tm, tn), lambda i, j, b: (i, j)),
        out_shape=jax.ShapeDtypeStruct((m, d), BF16),
        scratch_shapes=[pltpu.VMEM((tm, tn), F32)],
        compiler_params=_cp("parallel", "parallel", "arbitrary"),
        name="merge_branches",
    )(xm, w_gate, ys, w_branch)


def _qk_prep_kernel(pq_ref, pk_ref, gq_ref, gk_ref, cos_ref, sin_ref, grp_ref, qo_ref, ko_ref, *, rope, q_scale):
    tm = pq_ref.shape[0]
    lane = lax.broadcasted_iota(jnp.int32, (tm, LANES), 1)
    first_half = (lane & 16) == 0
    for src, g_ref, dst, scale in ((pq_ref, gq_ref, qo_ref, q_scale), (pk_ref, gk_ref, ko_ref, 1.0)):
        for j in range(src.shape[1] // LANES):
            sl = slice(j * LANES, (j + 1) * LANES)
            x = src[:, sl].astype(F32)
            ms = jnp.dot(x * x, grp_ref[...], preferred_element_type=F32, precision=HI)
            y = x * lax.rsqrt(ms + EPS) * g_ref[...]
            if rope:
                partner = jnp.where(first_half, pltpu.roll(y, LANES - 16, 1), pltpu.roll(y, 16, 1))
                y = y * cos_ref[...] + partner * sin_ref[...]
            dst[:, sl] = (y * scale).astype(dst.dtype)


def qk_prep(p, n_seq, bw, gq, gk, cos_t, sin_t, rope, q_scale, tm):
    m = p.shape[0]
    dh = gq.shape[0]
    reps = LANES // dh
    idx = jnp.arange(LANES)
    grp = jnp.where((idx[:, None] // dh) == (idx[None, :] // dh), 1.0 / dh, 0.0).astype(F32)
    tiles_per_seq = n_seq // tm
    return pl.pallas_call(
        functools.partial(_qk_prep_kernel, rope=rope, q_scale=q_scale),
        grid=(m // tm,),
        in_specs=[
            pl.BlockSpec((tm, bw), lambda i: (i, 0)),
            pl.BlockSpec((tm, bw), lambda i: (i, 1)),
            pl.BlockSpec((1, LANES), lambda i: (0, 0)),
            pl.BlockSpec((1, LANES), lambda i: (0, 0)),
            pl.BlockSpec((tm, LANES), lambda i: (i % tiles_per_seq, 0)),
            pl.BlockSpec((tm, LANES), lambda i: (i % tiles_per_seq, 0)),
            pl.BlockSpec((LANES, LANES), lambda i: (0, 0)),
        ],
        out_specs=[pl.BlockSpec((tm, bw), lambda i: (i, 0)), pl.BlockSpec((tm, bw), lambda i: (i, 0))],
        out_shape=[jax.ShapeDtypeStruct((m, bw), BF16), jax.ShapeDtypeStruct((m, bw), BF16)],
        compiler_params=_cp("parallel"),
        name="qk_prep",
    )(p, p, jnp.tile(gq, reps).reshape(1, LANES), jnp.tile(gk, reps).reshape(1, LANES), cos_t, sin_t, grp)


def rope_tables(n_seq, dh):
    nf = dh // 4
    pos = jnp.arange(n_seq)
    row = (pos // GRID_W).astype(F32)
    col = (pos % GRID_W).astype(F32)
    freqs = ROPE_THETA ** (-jnp.arange(nf, dtype=F32) / nf)
    lane = jnp.arange(LANES)
    d = lane % dh
    use_col = (d // (2 * nf)) == 1
    second = ((d // nf) % 2) == 1
    f = freqs[d % nf]
    ang = jnp.where(use_col[None, :], col[:, None], row[:, None]) * f[None, :]
    return jnp.cos(ang), jnp.where(second[None, :], 1.0, -1.0) * jnp.sin(ang)


def _att_kernel(lam_ref, q_ref, kt_ref, v_ref, g_ref, o_ref, m_ref, acc_ref, s_ref, mc_ref, *, n_chunks, ck,
                out_scale):
    tq = q_ref.shape[1]
    half = LANES // 2
    n_tiles = ck // LANES
    q = q_ref[0]
    lane = lax.broadcasted_iota(jnp.int32, (tq, LANES), 1)
    zero = jnp.zeros_like(q)
    qs = (jnp.where(lane < half, q, zero), jnp.where(lane >= half, q, zero))
    m_ref[...] = jnp.full(m_ref.shape, -jnp.inf, F32)
    acc_ref[...] = jnp.zeros(acc_ref.shape, F32)

    def scores(c, slot):
        kt = kt_ref[0, 0, c]
        for ci in range(2):
            s = jnp.dot(qs[ci], kt, preferred_element_type=F32)
            s_ref[slot, ci] = s
            mc = s[:, :LANES]
            for j in range(1, n_tiles):
                mc = jnp.maximum(mc, s[:, j * LANES:(j + 1) * LANES])
            mc_ref[slot, ci] = mc

    def softmax_pv(c, slot):
        v = v_ref[0, pl.ds(pl.multiple_of(c * ck, ck), ck), :]
        for ci in range(2):
            m_old = m_ref[ci]
            m_new = jnp.maximum(m_old, jnp.max(mc_ref[slot, ci], axis=-1, keepdims=True))
            alpha = jnp.exp2(m_old - m_new)
            p = jnp.concatenate(
                [jnp.exp2(s_ref[slot, ci, :, j * LANES:(j + 1) * LANES] - m_new).astype(BF16)
                 for j in range(n_tiles)], axis=1)
            pv = jnp.dot(p, v, preferred_element_type=F32)
            acc_ref[ci] = jnp.concatenate([alpha, alpha], axis=1) * acc_ref[ci] + pv
            m_ref[ci] = m_new

    scores(0, 0)
    n_pairs = (n_chunks - 1) // 2

    def body(i, carry):
        c = 2 * i
        scores(c + 1, 1)
        softmax_pv(c, 0)
        scores(c + 2, 0)
        softmax_pv(c + 1, 1)
        return carry

    if n_pairs > 0:
        lax.fori_loop(0, n_pairs, body, 0)
    if (n_chunks - 1) % 2 == 1:
        scores(n_chunks - 1, 1)
        softmax_pv(n_chunks - 2, 0)
        softmax_pv(n_chunks - 1, 1)
    else:
        softmax_pv(n_chunks - 1, 0)

    a1 = acc_ref[0]
    a2 = acc_ref[1]
    o = a1[:, :LANES] / a1[:, LANES:] - lam_ref[0, 0] * (a2[:, :LANES] / a2[:, LANES:])
    ms = jnp.mean(o * o, axis=-1, keepdims=True)
    o_ref[0] = (o * lax.rsqrt(ms + EPS) * g_ref[...] * out_scale).astype(o_ref.dtype)


def diff_attention(q, k, v, lam, g_sub, out_scale, tq, ck):
    b, nq, bw = q.shape
    nk = k.shape[1]
    heads = bw // LANES
    n_chunks = nk // ck
    kt = k.reshape(b, n_chunks, ck, heads, LANES).transpose(0, 3, 1, 4, 2)
    v4 = v.reshape(b, nk, heads, LANES)
    v_aug = jnp.concatenate([v4, jnp.ones_like(v4)], axis=-1).reshape(b, nk, heads * 2 * LANES)
    return pl.pallas_call(
        functools.partial(_att_kernel, n_chunks=n_chunks, ck=ck, out_scale=out_scale),
        grid=(b, heads, nq // tq),
        in_specs=[
            pl.BlockSpec(memory_space=pltpu.SMEM),
            pl.BlockSpec((1, tq, LANES), lambda bi, h, i: (bi, i, h)),
            pl.BlockSpec((1, 1, n_chunks, LANES, ck), lambda bi, h, i: (bi, h, 0, 0, 0)),
            pl.BlockSpec((1, nk, 2 * LANES), lambda bi, h, i: (bi, 0, h)),
            pl.BlockSpec((1, LANES), lambda bi, h, i: (0, 0)),
        ],
        out_specs=pl.BlockSpec((1, tq, LANES), lambda bi, h, i: (bi, i, h)),
        out_shape=jax.ShapeDtypeStruct((b, nq, bw), BF16),
        scratch_shapes=[
            pltpu.VMEM((2, tq, LANES), F32),
            pltpu.VMEM((2, tq, 2 * LANES), F32),
            pltpu.VMEM((2, 2, tq, ck), F32),
            pltpu.VMEM((2, 2, tq, LANES), F32),
        ],
        compiler_params=_cp("parallel", "parallel", "parallel"),
        name="diff_attention",
    )(lam.reshape(1, 1).astype(F32), q, kt, v_aug, g_sub.reshape(1, LANES))


def _local_kernel(pm_ref, pp_ref, pn_ref, cb_ref, cc_ref, ch_ref, ccp_ref, chp_ref, ccn_ref, chn_ref,
                  pw_ref, ps_ref, cw_ref, yp_ref, yc_ref, scr_ref, *, n_seq, tiles_per_seq):
    ts, bw = pm_ref.shape
    it = pl.program_id(0) % tiles_per_seq
    has_prev = jnp.where(it > 0, 1.0, 0.0).astype(F32)
    has_next = jnp.where(it < tiles_per_seq - 1, 1.0, 0.0).astype(F32)
    gw = bw // len(POOL_WINDOWS)
    pos = it * ts + lax.broadcasted_iota(jnp.int32, (ts, 1), 0)

    scr_ref[0:HALO, :] = pp_ref[...].astype(F32) * has_prev
    scr_ref[HALO:HALO + ts, :] = pm_ref[...].astype(F32)
    scr_ref[HALO + ts:2 * HALO + ts, :] = pn_ref[...].astype(F32) * has_next
    for g, w in enumerate(POOL_WINDOWS):
        sl = slice(g * gw, (g + 1) * gw)
        lo, hi = w // 2, w - 1 - w // 2
        tot = None
        for off in range(-lo, hi + 1):
            part = scr_ref[HALO + off:HALO + off + ts, sl]
            tot = part if tot is None else tot + part
        cnt = (jnp.minimum(pos + hi, n_seq - 1) - jnp.maximum(pos - lo, 0) + 1).astype(F32)
        pooled = tot / cnt - scr_ref[HALO:HALO + ts, sl]
        y = jnp.dot(pooled.astype(BF16), pw_ref[g], preferred_element_type=F32)
        yp_ref[:, sl] = (y * ps_ref[:, sl]).astype(yp_ref.dtype)

    scr_ref[0:HALO, :] = ccp_ref[...].astype(F32) * chp_ref[...].astype(F32) * has_prev
    scr_ref[HALO:HALO + ts, :] = cc_ref[...].astype(F32) * ch_ref[...].astype(F32)
    scr_ref[HALO + ts:2 * HALO + ts, :] = ccn_ref[...].astype(F32) * chn_ref[...].astype(F32) * has_next
    conv = None
    for j in range(CONV_K):
        off = j - CONV_K // 2
        term = cw_ref[j:j + 1, :] * scr_ref[HALO + off:HALO + off + ts, :]
        conv = term if conv is None else conv + term
    yc_ref[...] = (cb_ref[...].astype(F32) * conv).astype(yc_ref.dtype)


def local_mixers(p, n_seq, bw, pool_w, pool_scale, conv_w, ts):
    m = p.shape[0]
    tps = n_seq // ts
    r = ts // HALO
    last_halo = m // HALO - 1

    def main(cb):
        return pl.BlockSpec((ts, bw), lambda i: (i, cb))

    def prev(cb):
        return pl.BlockSpec((HALO, bw), lambda i: (jnp.maximum(i * r - 1, 0), cb))

    def nxt(cb):
        return pl.BlockSpec((HALO, bw), lambda i: (jnp.minimum((i + 1) * r, last_halo), cb))

    n_g = len(POOL_WINDOWS)
    return pl.pallas_call(
        functools.partial(_local_kernel, n_seq=n_seq, tiles_per_seq=tps),
        grid=(m // ts,),
        in_specs=[main(3), prev(3), nxt(3), main(5), main(6), main(7), prev(6), prev(7), nxt(6), nxt(7),
                  pl.BlockSpec((n_g, bw // n_g, bw // n_g), lambda i: (0, 0, 0)),
                  pl.BlockSpec((1, bw), lambda i: (0, 0)),
                  pl.BlockSpec((CONV_K, bw), lambda i: (0, 0))],
        out_specs=[pl.BlockSpec((ts, bw), lambda i: (i, 0)), pl.BlockSpec((ts, bw), lambda i: (i, 0))],
        out_shape=[jax.ShapeDtypeStruct((m, bw), BF16), jax.ShapeDtypeStruct((m, bw), BF16)],
        scratch_shapes=[pltpu.VMEM((ts + 2 * HALO, bw), F32)],
        compiler_params=_cp("parallel"),
        name="local_mixers",
    )(p, p, p, p, p, p, p, p, p, p, pool_w.astype(BF16), pool_scale.reshape(1, bw), conv_w)


def _dft_cos_sin(n, scale):
    k = jnp.arange(n, dtype=jnp.int32)
    ang = ((k[:, None] * k[None, :]) % n).astype(F32) * (2.0 * math.pi / n)
    return jnp.cos(ang) * scale, jnp.sin(ang) * scale


def channel_dft_matrix(bw):
    gw = bw // FOURIER_GROUPS
    c, s = _dft_cos_sin(gw, gw ** -0.5)
    eye = jnp.eye(FOURIER_GROUPS, dtype=F32)
    return jnp.concatenate([jnp.kron(eye, c), jnp.kron(eye, s)], axis=1).astype(BF16)


def _fft_stage2_kernel(p_ref, q_ref, cw_ref, sw_ref, f2_ref, o_ref, b_ref):
    n2 = p_ref.shape[2]
    bw = o_ref.shape[2]
    cw = cw_ref[0]
    sw = sw_ref[0]
    for j in range(bw // LANES):
        lo = slice(j * LANES, (j + 1) * LANES)
        hi = slice(bw + j * LANES, bw + (j + 1) * LANES)
        ar = p_ref[0, 0, :, lo].astype(F32) - q_ref[0, 0, :, hi].astype(F32)
        ai = -(p_ref[0, 0, :, hi].astype(F32) + q_ref[0, 0, :, lo].astype(F32))
        b_ref[0:n2, lo] = (ar * cw + ai * sw).astype(BF16)
        b_ref[n2:2 * n2, lo] = (ai * cw - ar * sw).astype(BF16)
    o_ref[0] = jnp.dot(f2_ref[...], b_ref[...], preferred_element_type=F32).astype(o_ref.dtype)


def fourier_seq_two_stage(z, b, n_seq, bw, n1, n2):
    c1, s1 = _dft_cos_sin(n1, n1 ** -0.5)
    f1 = jnp.concatenate([c1, s1], axis=0).astype(BF16)
    c2, s2 = _dft_cos_sin(n2, n2 ** -0.5)
    f2 = jnp.concatenate([c2, s2], axis=1).astype(BF16)
    k1 = jnp.arange(n1, dtype=jnp.int32)
    t2 = jnp.arange(n2, dtype=jnp.int32)
    ang = (k1[:, None] * t2[None, :]).astype(F32) * (2.0 * math.pi / n_seq)
    cw = jnp.broadcast_to(jnp.cos(ang)[:, :, None], (n1, n2, LANES))
    sw = jnp.broadcast_to(jnp.sin(ang)[:, :, None], (n1, n2, LANES))
    zb = z.reshape(b, n1, n2 * 2 * bw)
    tn1 = _pick(n2 * 2 * bw, (8192, 4096, 2048))
    a = jnp.stack([matmul(f1, zb[i], tm=2 * n1, tn=tn1) for i in range(b)])
    a = a.reshape(b, 2 * n1, n2, 2 * bw)
    out = pl.pallas_call(
        _fft_stage2_kernel,
        grid=(b, n1),
        in_specs=[
            pl.BlockSpec((1, 1, n2, 2 * bw), lambda bi, k: (bi, k, 0, 0)),
            pl.BlockSpec((1, 1, n2, 2 * bw), lambda bi, k: (bi, n1 + k, 0, 0)),
            pl.BlockSpec((1, n2, LANES), lambda bi, k: (k, 0, 0)),
            pl.BlockSpec((1, n2, LANES), lambda bi, k: (k, 0, 0)),
            pl.BlockSpec((n2, 2 * n2), lambda bi, k: (0, 0)),
        ],
        out_specs=pl.BlockSpec((1, n2, bw), lambda bi, k: (bi, 0, k)),
        out_shape=jax.ShapeDtypeStruct((b, n2, n1 * bw), BF16),
        scratch_shapes=[pltpu.VMEM((2 * n2, bw), BF16)],
        compiler_params=_cp("parallel", "parallel"),
        name="fft_stage2",
    )(a, a, cw, sw, f2)
    return out.reshape(b * n_seq, bw)


def fourier_seq_dense(z, b, n_seq, bw):
    c, s = _dft_cos_sin(n_seq, n_seq ** -0.5)
    f = jnp.concatenate([c, -s], axis=1).astype(BF16)
    zb = z.reshape(b, n_seq, 2 * bw)
    outs = [matmul(f, jnp.concatenate([zb[i, :, :bw], zb[i, :, bw:]], axis=0)) for i in range(b)]
    return jnp.concatenate(outs, axis=0)


def _fft_factors(n_seq):
    n2 = LANES
    n1 = n_seq // n2
    return n1, n2


def _pad_last(w, mult):
    pad = (-w.shape[-1]) % mult
    return jnp.pad(w, [(0, 0)] * (w.ndim - 1) + [(0, pad)]) if pad else w


def _pad_rows(w, mult):
    pad = (-w.shape[-2]) % mult
    return jnp.pad(w, [(0, 0)] * (w.ndim - 2) + [(0, pad), (0, 0)]) if pad else w


def kernel(x, c, ctx, c_ctx, w_mod, b_mod, norm_mix, norm_ffn, w_in, w_gate, q_norm, k_norm, lambda_q1, lambda_k1,
           lambda_q2, lambda_k2, subln, pool_w, pool_scale, conv_w, w_branch, w_out, ffn_w1, ffn_w3, ffn_w2,
           router, moe_w1, moe_w3, moe_w2):
    b, n_lat, d = x.shape
    n_ctx = ctx.shape[1]
    depth = w_mod.shape[0]
    bw = d // 4
    dh = q_norm.shape[1]
    m_lat = b * n_lat
    m_ctx = b * n_ctx

    cond8 = jnp.zeros((8, d), F32).at[:b].set(c).at[b].set(c_ctx)
    mod3 = modulation_all(cond8, w_mod, b_mod).reshape(depth * 8, 1, N_MOD * d)

    rope_lat = rope_tables(n_lat, dh)
    rope_ctx = rope_tables(n_ctx, dh)
    w_cdft = channel_dft_matrix(bw)
    n1, n2 = _fft_factors(n_lat)

    h = x.reshape(m_lat, d)
    hc = ctx.reshape(m_ctx, d)

    for l in range(depth):
        last = l == depth - 1
        lam_init = 0.8 - 0.6 * math.exp(-0.3 * l)
        lam = (jnp.exp(jnp.sum(lambda_q1[l] * lambda_k1[l])) - jnp.exp(jnp.sum(lambda_q2[l] * lambda_k2[l]))
               + lam_init)
        w_in_l = w_in[l].astype(BF16)
        w_gate_l = w_gate[l].astype(BF16)
        w_branch_l = w_branch[l].astype(BF16)
        w_out_l = w_out[l].astype(BF16)[None]

        def lat_row(tm, l=l):
            return lambda i: l * 8 + i // (n_lat // tm)

        def ctx_row(tm, l=l):
            return lambda i: l * 8 + b

        def mixer_inputs(h2, n_seq, row_fn, rope, tables):
            tm_norm = _pick(n_seq, (256, 128))
            xm_ = mod_norm(h2, norm_mix[l], mod3, row_fn(tm_norm), 0, 1, tm_norm)
            p_ = matmul(xm_, w_in_l, tm=_pick(n_seq, (1024, 512, 256)))
            q_, k_ = qk_prep(p_, n_seq, bw, q_norm[l], k_norm[l], tables[0], tables[1], rope=rope,
                             q_scale=LOG2_E * dh ** -0.5, tm=_pick(n_seq, (512, 256)))
            return xm_, p_, q_, k_, p_[:, 2 * bw:3 * bw]

        def mixer_output(h2, n_seq, row_fn, xm_, p_, att_, four_):
            pool_, conv_ = local_mixers(p_, n_seq, bw, pool_w[l], pool_scale[l], conv_w[l],
                                        ts=_pick(n_seq, (512, 256)))
            ys = jnp.stack([att_, pool_, four_, conv_])
            tm = _pick(n_seq, (1024, 512, 256))
            merged = merge_branches(xm_, w_gate_l, ys, w_branch_l, tm=tm, tn=512)
            return matmul_residual(merged, w_out_l, h2, mod3, row_fn(tm), 2, tm=tm, tn=512, tk=d)

        xcm, pc, q_c, k_c, v_c = mixer_inputs(hc, n_ctx, ctx_row, False, rope_ctx)
        xm, p, q, k, v = mixer_inputs(h, n_lat, lat_row, True, rope_lat)

        k_all = jnp.concatenate([k_c.reshape(b, n_ctx, bw), k.reshape(b, n_lat, bw)], axis=1)
        v_all = jnp.concatenate([v_c.reshape(b, n_ctx, bw), v.reshape(b, n_lat, bw)], axis=1)
        nk = n_ctx + n_lat
        ck = _pick(nk, (384, 256, 128))
        att = diff_attention(q.reshape(b, n_lat, bw), k_all, v_all, lam, subln[l], 1.0 - lam_init,
                             tq=_pick(n_lat, (512, 256, 128)), ck=ck).reshape(m_lat, bw)
        z_lat = matmul(p, w_cdft, x_col=4, tm=_pick(n_lat, (1024, 512, 256)))
        four = fourier_seq_two_stage(z_lat, b, n_lat, bw, n1, n2)
        h_new = mixer_output(h, n_lat, lat_row, xm, p, att, four)

        if not last:
            att_c = diff_attention(q_c.reshape(b, n_ctx, bw), k_c.reshape(b, n_ctx, bw), v_c.reshape(b, n_ctx, bw),
                                   lam, subln[l], 1.0 - lam_init, tq=_pick(n_ctx, (256, 128)),
                                   ck=_pick(n_ctx, (256, 128))).reshape(m_ctx, bw)
            z_ctx = matmul(pc, w_cdft, x_col=4, tm=_pick(n_ctx, (256, 128)))
            four_c = fourier_seq_dense(z_ctx, b, n_ctx, bw)
            hc = mixer_output(hc, n_ctx, ctx_row, xcm, pc, att_c, four_c)
        h = h_new

        if l % 2 == 0:
            w1 = _pad_last(ffn_w1[l // 2].astype(BF16), 1024)[None]
            w3 = _pad_last(ffn_w3[l // 2].astype(BF16), 1024)[None]
            w2 = _pad_rows(ffn_w2[l // 2].astype(BF16), 1024)[None]
            rt = None
        else:
            w1 = _pad_last(moe_w1[l // 2].astype(BF16), 256)
            w3 = _pad_last(moe_w3[l // 2].astype(BF16), 256)
            w2 = _pad_rows(moe_w2[l // 2].astype(BF16), 256)
            rt = router[l // 2]
        f_pad = w1.shape[2]
        tn_up = _pick(f_pad, (512, 256))
        tk_dn = _pick(f_pad, (2816, 1024, 512, 256))

        def channel_mix(h2, n_seq, row_fn):
            tm_norm = _pick(n_seq, (256, 128))
            zn = mod_norm(h2, norm_ffn[l], mod3, row_fn(tm_norm), 3, 4, tm_norm, router=rt)
            z_, comb = zn if rt is not None else (zn, None)
            tm = _pick(n_seq, (1024, 512, 256))
            if rt is not None and (TOP_K * h2.shape[0]) % MOE_TILE == 0:
                return moe_sparse(z_, comb, rt.shape[1], w1, w3, w2, h2, mod3, row_fn(tm_norm), 5, tm_norm)
            hid = swiglu_up(z_, w1, w3, tm=tm, tn=tn_up, comb=comb)
            return matmul_residual(hid, w2, h2, mod3, row_fn(tm), 5, tm=tm, tn=1024, tk=tk_dn)

        h = channel_mix(h, n_lat, lat_row)
        if not last:
            hc = channel_mix(hc, n_ctx, ctx_row)

    return h.reshape(b, n_lat, d)
```

```python
import functools
import math

import jax
import jax.numpy as jnp
from jax import lax
from jax.experimental import pallas as pl
from jax.experimental.pallas import tpu as pltpu

F32 = jnp.float32
BF16 = jnp.bfloat16

GRID_W = 64
ROPE_THETA = 10000.0
POOL_WINDOWS = (2, 4, 8, 16)
FOURIER_GROUPS = 4
CONV_K = 3
N_MOD = 6
EPS = 1e-6
LANES = 128
HALO = 16
VMEM_LIMIT = 56 * 1024 * 1024
HI = lax.Precision.HIGHEST
LOG2_E = 1.4426950408889634


def _cp(*sem, vmem=VMEM_LIMIT):
    return pltpu.CompilerParams(dimension_semantics=sem, vmem_limit_bytes=vmem)


def _pick(n, prefs):
    for t in prefs:
        if n % t == 0:
            return t
    return n


def _sigmoid(x):
    return 1.0 / (1.0 + jnp.exp(-x))


def _mod_kernel(c_ref, w_ref, b_ref, o_ref):
    x = c_ref[...]
    s = x * _sigmoid(x)
    acc = jnp.dot(s.astype(BF16), w_ref[0].astype(BF16), preferred_element_type=F32)
    o_ref[0] = acc + b_ref[0]


def modulation_all(cond8, w_mod, b_mod):
    depth, d, cols = w_mod.shape
    tn = _pick(cols, (1024, 512, 256, 128))
    return pl.pallas_call(
        _mod_kernel,
        grid=(depth, cols // tn),
        in_specs=[
            pl.BlockSpec((8, d), lambda l, j: (0, 0)),
            pl.BlockSpec((1, d, tn), lambda l, j: (l, 0, j)),
            pl.BlockSpec((1, 1, tn), lambda l, j: (l, 0, j)),
        ],
        out_specs=pl.BlockSpec((1, 8, tn), lambda l, j: (l, 0, j)),
        out_shape=jax.ShapeDtypeStruct((depth, 8, cols), F32),
        compiler_params=_cp("parallel", "parallel"),
        name="modulation",
    )(cond8, w_mod, b_mod.reshape(depth, 1, cols))


def _norm_body(h_ref, g_ref, sh_ref, sc_ref):
    x = h_ref[...]
    ms = jnp.mean(x * x, axis=-1, keepdims=True)
    y = x * lax.rsqrt(ms + EPS) * g_ref[...]
    return y * (1.0 + sc_ref[0]) + sh_ref[0]


def _norm_kernel(h_ref, g_ref, sh_ref, sc_ref, o_ref):
    o_ref[...] = _norm_body(h_ref, g_ref, sh_ref, sc_ref).astype(o_ref.dtype)


def _norm_route_kernel(h_ref, g_ref, sh_ref, sc_ref, r_ref, o_ref, comb_ref, *, n_experts):
    z = _norm_body(h_ref, g_ref, sh_ref, sc_ref)
    o_ref[...] = z.astype(o_ref.dtype)
    logits = jnp.dot(z, r_ref[...], preferred_element_type=F32, precision=HI)
    lane = lax.broadcasted_iota(jnp.int32, logits.shape, 1).astype(F32)
    neg = jnp.float32(-jnp.inf)
    lg = jnp.where(lane < n_experts, logits, neg)
    m1 = jnp.max(lg, axis=-1, keepdims=True)
    i1 = jnp.min(jnp.where(lg == m1, lane, float(LANES)), axis=-1, keepdims=True)
    lg2 = jnp.where(lane == i1, neg, lg)
    m2 = jnp.max(lg2, axis=-1, keepdims=True)
    i2 = jnp.min(jnp.where(lg2 == m2, lane, float(LANES)), axis=-1, keepdims=True)
    e = jnp.exp(m2 - m1)
    w1 = 1.0 / (1.0 + e)
    w2 = e / (1.0 + e)
    comb_ref[...] = jnp.where(lane == i1, w1, 0.0) + jnp.where(lane == i2, w2, 0.0)


def mod_norm(h, g, mod3, row_of_tile, sh_chunk, sc_chunk, tm, router=None):
    m, d = h.shape
    in_specs = [
        pl.BlockSpec((tm, d), lambda i: (i, 0)),
        pl.BlockSpec((1, d), lambda i: (0, 0)),
        pl.BlockSpec((1, 1, d), lambda i: (row_of_tile(i), 0, sh_chunk)),
        pl.BlockSpec((1, 1, d), lambda i: (row_of_tile(i), 0, sc_chunk)),
    ]
    args = [h, g.reshape(1, d), mod3, mod3]
    if router is None:
        return pl.pallas_call(
            _norm_kernel,
            grid=(m // tm,),
            in_specs=in_specs,
            out_specs=pl.BlockSpec((tm, d), lambda i: (i, 0)),
            out_shape=jax.ShapeDtypeStruct((m, d), BF16),
            compiler_params=_cp("parallel"),
            name="mod_norm",
        )(*args)
    n_experts = router.shape[1]
    rpad = jnp.pad(router, ((0, 0), (0, LANES - n_experts)))
    in_specs.append(pl.BlockSpec((d, LANES), lambda i: (0, 0)))
    return pl.pallas_call(
        functools.partial(_norm_route_kernel, n_experts=n_experts),
        grid=(m // tm,),
        in_specs=in_specs,
        out_specs=[pl.BlockSpec((tm, d), lambda i: (i, 0)), pl.BlockSpec((tm, LANES), lambda i: (i, 0))],
        out_shape=[jax.ShapeDtypeStruct((m, d), BF16), jax.ShapeDtypeStruct((m, LANES), F32)],
        compiler_params=_cp("parallel"),
        name="mod_norm_route",
    )(*args, rpad)


def _mm_kernel(x_ref, w_ref, o_ref):
    o_ref[...] = jnp.dot(x_ref[...], w_ref[...], preferred_element_type=F32).astype(o_ref.dtype)


def matmul(x, w, out_dtype=BF16, x_col=0, tm=None, tn=None):
    m = x.shape[0]
    k, n = w.shape
    tm = tm or _pick(m, (1024, 512, 256, 128))
    tn = tn or _pick(n, (1024, 512, 256, 128))
    return pl.pallas_call(
        _mm_kernel,
        grid=(m // tm, n // tn),
        in_specs=[
            pl.BlockSpec((tm, k), lambda i, j: (i, x_col)),
            pl.BlockSpec((k, tn), lambda i, j: (0, j)),
        ],
        out_specs=pl.BlockSpec((tm, tn), lambda i, j: (i, j)),
        out_shape=jax.ShapeDtypeStruct((m, n), out_dtype),
        compiler_params=_cp("parallel", "parallel"),
        name="matmul",
    )(x, w)


def _mm_res_kernel(x_ref, w_ref, res_ref, gate_ref, o_ref, acc_ref, *, nk):
    k = pl.program_id(2)
    part = jnp.dot(x_ref[...], w_ref[0], preferred_element_type=F32)
    if nk == 1:
        o_ref[...] = res_ref[...] + gate_ref[0] * part
        return

    @pl.when(k == 0)
    def _():
        acc_ref[...] = part

    @pl.when(jnp.logical_and(k > 0, k < nk - 1))
    def _():
        acc_ref[...] += part

    @pl.when(k == nk - 1)
    def _():
        o_ref[...] = res_ref[...] + gate_ref[0] * (acc_ref[...] + part)


def matmul_residual(x, w3, res, mod3, row_of_tile, gate_chunk, tm, tn, tk):
    m, n = res.shape
    n_e, kf, _ = w3.shape
    kpe = kf // tk
    nk = n_e * kpe
    return pl.pallas_call(
        functools.partial(_mm_res_kernel, nk=nk),
        grid=(m // tm, n // tn, nk),
        in_specs=[
            pl.BlockSpec((tm, tk), lambda i, j, k: (i, k)),
            pl.BlockSpec((1, tk, tn), lambda i, j, k: (k // kpe, k % kpe, j)),
            pl.BlockSpec((tm, tn), lambda i, j, k: (i, j)),
            pl.BlockSpec((1, 1, tn), lambda i, j, k: (row_of_tile(i), 0, gate_chunk * (n // tn) + j)),
        ],
        out_specs=pl.BlockSpec((tm, tn), lambda i, j, k: (i, j)),
        out_shape=jax.ShapeDtypeStruct((m, n), F32),
        scratch_shapes=[pltpu.VMEM((tm, tn), F32)],
        compiler_params=_cp("parallel", "parallel", "arbitrary"),
        name="matmul_residual",
    )(x, w3, res, mod3)


def _swiglu_kernel(z_ref, w1_ref, w3_ref, o_ref):
    z = z_ref[...]
    a = jnp.dot(z, w1_ref[0], preferred_element_type=F32)
    b = jnp.dot(z, w3_ref[0], preferred_element_type=F32)
    o_ref[...] = (a * _sigmoid(a) * b).astype(o_ref.dtype)


def _swiglu_comb_kernel(z_ref, w1_ref, w3_ref, comb_ref, o_ref, *, blocks_per_expert):
    z = z_ref[...]
    a = jnp.dot(z, w1_ref[0], preferred_element_type=F32)
    b = jnp.dot(z, w3_ref[0], preferred_element_type=F32)
    e = pl.program_id(1) // blocks_per_expert
    comb = comb_ref[...]
    lane = lax.broadcasted_iota(jnp.int32, comb.shape, 1)
    scale = jnp.sum(jnp.where(lane == e, comb, 0.0), axis=-1, keepdims=True)
    o_ref[...] = (a * _sigmoid(a) * b * scale).astype(o_ref.dtype)


def swiglu_up(z, w1, w3, tm, tn, comb=None):
    m, k = z.shape
    n_e, _, f = w1.shape
    bpe = f // tn
    in_specs = [
        pl.BlockSpec((tm, k), lambda i, j: (i, 0)),
        pl.BlockSpec((1, k, tn), lambda i, j: (j // bpe, 0, j % bpe)),
        pl.BlockSpec((1, k, tn), lambda i, j: (j // bpe, 0, j % bpe)),
    ]
    args = [z, w1, w3]
    if comb is None:
        kern = _swiglu_kernel
    else:
        kern = functools.partial(_swiglu_comb_kernel, blocks_per_expert=bpe)
        in_specs.append(pl.BlockSpec((tm, LANES), lambda i, j: (i, 0)))
        args.append(comb)
    return pl.pallas_call(
        kern,
        grid=(m // tm, n_e * bpe),
        in_specs=in_specs,
        out_specs=pl.BlockSpec((tm, tn), lambda i, j: (i, j)),
        out_shape=jax.ShapeDtypeStruct((m, n_e * f), BF16),
        compiler_params=_cp("parallel", "parallel"),
        name="swiglu_up",
    )(*args)


MOE_TILE = 1024
TOP_K = 2
TOKEN_CHUNK = 128


def _gather_kernel(idx_ref, src_ref, o_ref, sem, *, rows):
    base = pl.program_id(0) * rows

    def issue(r, carry):
        pltpu.make_async_copy(src_ref.at[idx_ref[base + r]], o_ref.at[r], sem).start()
        return carry

    lax.fori_loop(0, rows, issue, 0)
    pltpu.make_async_copy(src_ref.at[pl.ds(0, rows)], o_ref, sem).wait()


def gather_rows(src, idx, rows_per_step):
    n_src, d = src.shape
    n_out = idx.shape[0]
    src3 = src.reshape(n_src, d // TOKEN_CHUNK, TOKEN_CHUNK)
    out = pl.pallas_call(
        functools.partial(_gather_kernel, rows=rows_per_step),
        grid_spec=pltpu.PrefetchScalarGridSpec(
            num_scalar_prefetch=1,
            grid=(n_out // rows_per_step,),
            in_specs=[pl.BlockSpec(memory_space=pl.ANY)],
            out_specs=pl.BlockSpec((rows_per_step, d // TOKEN_CHUNK, TOKEN_CHUNK), lambda i, idx_ref: (i, 0, 0)),
            scratch_shapes=[pltpu.SemaphoreType.DMA(())],
        ),
        out_shape=jax.ShapeDtypeStruct((n_out, d // TOKEN_CHUNK, TOKEN_CHUNK), src.dtype),
        compiler_params=_cp("arbitrary"),
        name="gather_rows",
    )(idx, src3)
    return out.reshape(n_out, d)


def _moe_up_kernel(te_ref, nu_ref, z_ref, w1_ref, w3_ref, ws_ref, o_ref):
    used = pl.program_id(0) < nu_ref[0]

    @pl.when(used)
    def _():
        z = z_ref[...]
        a = jnp.dot(z, w1_ref[0], preferred_element_type=F32)
        b = jnp.dot(z, w3_ref[0], preferred_element_type=F32)
        scale = jnp.concatenate([ws_ref[...]] * (o_ref.shape[1] // LANES), axis=1)
        o_ref[...] = (a * _sigmoid(a) * b * scale).astype(o_ref.dtype)

    @pl.when(jnp.logical_not(used))
    def _():
        o_ref[...] = jnp.zeros(o_ref.shape, o_ref.dtype)


def _moe_down_kernel(te_ref, nu_ref, x_ref, w_ref, o_ref):
    used = pl.program_id(0) < nu_ref[0]

    @pl.when(used)
    def _():
        o_ref[...] = jnp.dot(x_ref[...], w_ref[0], preferred_element_type=F32).astype(o_ref.dtype)

    @pl.when(jnp.logical_not(used))
    def _():
        o_ref[...] = jnp.zeros(o_ref.shape, o_ref.dtype)


def _combine_kernel(h_ref, ya_ref, yb_ref, gate_ref, o_ref):
    o_ref[...] = h_ref[...] + gate_ref[0] * (ya_ref[0].astype(F32) + yb_ref[0].astype(F32))


def moe_sparse(z, comb, n_experts, w1, w3, w2, h2, mod3, row_of_tile, gate_chunk, tm):
    m, d = z.shape
    f = w1.shape[2]
    n_tiles = (TOP_K * m) // MOE_TILE + n_experts
    n_rows = n_tiles * MOE_TILE
    gather_step = _pick(m, (512, 256, 128))

    cw = comb[:, :n_experts]
    sel = cw > 0.0
    seli = sel.astype(jnp.int32)
    rank = jnp.cumsum(seli, axis=0) - seli
    cnt = jnp.sum(seli, axis=0)
    padded = ((cnt + MOE_TILE - 1) // MOE_TILE) * MOE_TILE
    gend = jnp.cumsum(padded)
    slot = (gend - padded)[None, :] + rank
    top_w, top_e = lax.top_k(cw, TOP_K)
    slot_ab = jnp.take_along_axis(jnp.where(sel, slot, n_rows - 1), top_e, axis=1).astype(jnp.int32)
    pair_id = jnp.arange(m * TOP_K, dtype=jnp.int32)
    row_pair = jnp.full((n_rows,), -1, jnp.int32).at[slot_ab.reshape(-1)].set(pair_id)
    row_valid = row_pair >= 0
    src_row = jnp.where(row_valid, row_pair // TOP_K, 0)
    w_slot = jnp.where(row_valid, top_w.reshape(-1)[jnp.maximum(row_pair, 0)], 0.0)
    w_slot = jnp.broadcast_to(w_slot[:, None], (n_rows, LANES))
    tile_start = jnp.arange(n_tiles, dtype=jnp.int32) * MOE_TILE
    tile_expert = jnp.minimum(jnp.sum((tile_start[:, None] >= gend[None, :]).astype(jnp.int32), axis=1),
                              n_experts - 1).astype(jnp.int32)
    n_used = (gend[-1:] // MOE_TILE).astype(jnp.int32)

    zs = gather_rows(z, src_row, gather_step)
    tn_up = _pick(f, (256, 128))
    hid = pl.pallas_call(
        _moe_up_kernel,
        grid_spec=pltpu.PrefetchScalarGridSpec(
            num_scalar_prefetch=2,
            grid=(n_tiles, f // tn_up),
            in_specs=[
                pl.BlockSpec((MOE_TILE, d), lambda i, j, te, nu: (i, 0)),
                pl.BlockSpec((1, d, tn_up), lambda i, j, te, nu: (te[i], 0, j)),
                pl.BlockSpec((1, d, tn_up), lambda i, j, te, nu: (te[i], 0, j)),
                pl.BlockSpec((MOE_TILE, LANES), lambda i, j, te, nu: (i, 0)),
            ],
            out_specs=pl.BlockSpec((MOE_TILE, tn_up), lambda i, j, te, nu: (i, j)),
        ),
        out_shape=jax.ShapeDtypeStruct((n_rows, f), BF16),
        compiler_params=_cp("parallel", "parallel"),
        name="moe_up",
    )(tile_expert, n_used, zs, w1, w3, w_slot)
    tn_dn = _pick(d, (1024, 512, 256))
    ys = pl.pallas_call(
        _moe_down_kernel,
        grid_spec=pltpu.PrefetchScalarGridSpec(
            num_scalar_prefetch=2,
            grid=(n_tiles, d // tn_dn),
            in_specs=[
                pl.BlockSpec((MOE_TILE, f), lambda i, j, te, nu: (i, 0)),
                pl.BlockSpec((1, f, tn_dn), lambda i, j, te, nu: (te[i], 0, j)),
            ],
            out_specs=pl.BlockSpec((MOE_TILE, tn_dn), lambda i, j, te, nu: (i, j)),
        ),
        out_shape=jax.ShapeDtypeStruct((n_rows, d), BF16),
        compiler_params=_cp("parallel", "parallel"),
        name="moe_down",
    )(tile_expert, n_used, hid, w2)
    yab = gather_rows(ys, slot_ab.T.reshape(-1), gather_step).reshape(TOP_K, m, d)
    return pl.pallas_call(
        _combine_kernel,
        grid=(m // tm,),
        in_specs=[
            pl.BlockSpec((tm, d), lambda i: (i, 0)),
            pl.BlockSpec((1, tm, d), lambda i: (0, i, 0)),
            pl.BlockSpec((1, tm, d), lambda i: (1, i, 0)),
            pl.BlockSpec((1, 1, d), lambda i: (row_of_tile(i), 0, gate_chunk)),
        ],
        out_specs=pl.BlockSpec((tm, d), lambda i: (i, 0)),
        out_shape=jax.ShapeDtypeStruct((m, d), F32),
        compiler_params=_cp("parallel"),
        name="moe_combine",
    )(h2, yab, yab, mod3)


def _merge_kernel(xm_ref, wg_ref, *rest, n_branch):
    y_refs = rest[:n_branch]
    wb_ref, o_ref, acc_ref = rest[n_branch:]
    b = pl.program_id(2)
    gate = _sigmoid(jnp.dot(xm_ref[...], wg_ref[...], preferred_element_type=F32))
    for bi, y_ref in enumerate(y_refs):

        @pl.when(b == bi)
        def _(bi=bi, y_ref=y_ref):
            term = gate * jnp.dot(y_ref[...], wb_ref[0], preferred_element_type=F32)
            if bi == 0:
                acc_ref[...] = term
            elif bi < n_branch - 1:
                acc_ref[...] += term
            else:
                o_ref[...] = (acc_ref[...] + term).astype(o_ref.dtype)


def merge_branches(xm, w_gate, ys, w_branch, tm, tn):
    m, d = xm.shape
    n_branch, bw, _ = w_branch.shape
    nj = d // tn
    return pl.pallas_call(
        functools.partial(_merge_kernel, n_branch=n_branch),
        grid=(m // tm, nj, n_branch),
        in_specs=[
            pl.BlockSpec((tm, d), lambda i, j, b: (i, 0)),
            pl.BlockSpec((d, tn), lambda i, j, b: (0, b * nj + j)),
        ] + [pl.BlockSpec((tm, bw), lambda i, j, b: (i, 0))] * n_branch + [
            pl.BlockSpec((1, bw, tn), lambda i, j, b: (b, 0, j)),
        ],
        out_specs=pl.BlockSpec((tm, tn), lambda i, j, b: (i, j)),
        out_shape=jax.ShapeDtypeStruct((m, d), BF16),
        scratch_shapes=[pltpu.VMEM((tm, tn), F32)],
        compiler_params=_cp("parallel", "parallel", "arbitrary"),
        name="merge_branches",
    )(xm, w_gate, *ys, w_branch)


def _qk_prep_kernel(pq_ref, pk_ref, gq_ref, gk_ref, cos_ref, sin_ref, grp_ref, qo_ref, ko_ref, *, rope, q_scale):
    tm = pq_ref.shape[0]
    lane = lax.broadcasted_iota(jnp.int32, (tm, LANES), 1)
    first_half = (lane & 16) == 0
    for src, g_ref, dst, scale in ((pq_ref, gq_ref, qo_ref, q_scale), (pk_ref, gk_ref, ko_ref, 1.0)):
        for j in range(src.shape[1] // LANES):
            sl = slice(j * LANES, (j + 1) * LANES)
            x = src[:, sl].astype(F32)
            ms = jnp.dot(x * x, grp_ref[...], preferred_element_type=F32, precision=HI)
            y = x * lax.rsqrt(ms + EPS) * g_ref[...]
            if rope:
                partner = jnp.where(first_half, pltpu.roll(y, LANES - 16, 1), pltpu.roll(y, 16, 1))
                y = y * cos_ref[...] + partner * sin_ref[...]
            dst[:, sl] = (y * scale).astype(dst.dtype)


def qk_prep(p, n_seq, bw, gq, gk, cos_t, sin_t, rope, q_scale, tm):
    m = p.shape[0]
    dh = gq.shape[0]
    reps = LANES // dh
    idx = jnp.arange(LANES)
    grp = jnp.where((idx[:, None] // dh) == (idx[None, :] // dh), 1.0 / dh, 0.0).astype(F32)
    tiles_per_seq = n_seq // tm
    return pl.pallas_call(
        functools.partial(_qk_prep_kernel, rope=rope, q_scale=q_scale),
        grid=(m // tm,),
        in_specs=[
            pl.BlockSpec((tm, bw), lambda i: (i, 0)),
            pl.BlockSpec((tm, bw), lambda i: (i, 1)),
            pl.BlockSpec((1, LANES), lambda i: (0, 0)),
            pl.BlockSpec((1, LANES), lambda i: (0, 0)),
            pl.BlockSpec((tm, LANES), lambda i: (i % tiles_per_seq, 0)),
            pl.BlockSpec((tm, LANES), lambda i: (i % tiles_per_seq, 0)),
            pl.BlockSpec((LANES, LANES), lambda i: (0, 0)),
        ],
        out_specs=[pl.BlockSpec((tm, bw), lambda i: (i, 0)), pl.BlockSpec((tm, bw), lambda i: (i, 0))],
        out_shape=[jax.ShapeDtypeStruct((m, bw), BF16), jax.ShapeDtypeStruct((m, bw), BF16)],
        compiler_params=_cp("parallel"),
        name="qk_prep",
    )(p, p, jnp.tile(gq, reps).reshape(1, LANES), jnp.tile(gk, reps).reshape(1, LANES), cos_t, sin_t, grp)


def rope_tables(n_seq, dh):
    nf = dh // 4
    pos = jnp.arange(n_seq)
    row = (pos // GRID_W).astype(F32)
    col = (pos % GRID_W).astype(F32)
    freqs = ROPE_THETA ** (-jnp.arange(nf, dtype=F32) / nf)
    lane = jnp.arange(LANES)
    d = lane % dh
    use_col = (d // (2 * nf)) == 1
    second = ((d // nf) % 2) == 1
    f = freqs[d % nf]
    ang = jnp.where(use_col[None, :], col[:, None], row[:, None]) * f[None, :]
    return jnp.cos(ang), jnp.where(second[None, :], 1.0, -1.0) * jnp.sin(ang)


def _att_kernel(lam_ref, q_ref, kt_ref, v_ref, g_ref, o_ref, m_ref, acc_ref, s_ref, mc_ref, *, n_chunks, ck,
                out_scale):
    tq = q_ref.shape[1]
    half = LANES // 2
    n_tiles = ck // LANES
    q = q_ref[0]
    lane = lax.broadcasted_iota(jnp.int32, (tq, LANES), 1)
    zero = jnp.zeros_like(q)
    qs = (jnp.where(lane < half, q, zero), jnp.where(lane >= half, q, zero))
    m_ref[...] = jnp.full(m_ref.shape, -jnp.inf, F32)
    acc_ref[...] = jnp.zeros(acc_ref.shape, F32)
    ones = jnp.ones((ck, LANES), BF16)

    def scores(c, slot):
        kt = kt_ref[0, 0, c]
        for ci in range(2):
            s = jnp.dot(qs[ci], kt, preferred_element_type=F32)
            s_ref[slot, ci] = s
            mc = s[:, :LANES]
            for j in range(1, n_tiles):
                mc = jnp.maximum(mc, s[:, j * LANES:(j + 1) * LANES])
            mc_ref[slot, ci] = mc

    def softmax_pv(c, slot):
        v = jnp.concatenate([v_ref[0, pl.ds(pl.multiple_of(c * ck, ck), ck), :], ones], axis=1)
        for ci in range(2):
            m_old = m_ref[ci]
            m_new = jnp.maximum(m_old, jnp.max(mc_ref[slot, ci], axis=-1, keepdims=True))
            alpha = jnp.exp2(m_old - m_new)
            p = jnp.concatenate(
                [jnp.exp2(s_ref[slot, ci, :, j * LANES:(j + 1) * LANES] - m_new).astype(BF16)
                 for j in range(n_tiles)], axis=1)
            pv = jnp.dot(p, v, preferred_element_type=F32)
            acc_ref[ci] = jnp.concatenate([alpha, alpha], axis=1) * acc_ref[ci] + pv
            m_ref[ci] = m_new

    scores(0, 0)
    n_pairs = (n_chunks - 1) // 2

    def body(i, carry):
        c = 2 * i
        scores(c + 1, 1)
        softmax_pv(c, 0)
        scores(c + 2, 0)
        softmax_pv(c + 1, 1)
        return carry

    if n_pairs > 0:
        lax.fori_loop(0, n_pairs, body, 0)
    if (n_chunks - 1) % 2 == 1:
        scores(n_chunks - 1, 1)
        softmax_pv(n_chunks - 2, 0)
        softmax_pv(n_chunks - 1, 1)
    else:
        softmax_pv(n_chunks - 1, 0)

    a1 = acc_ref[0]
    a2 = acc_ref[1]
    o = a1[:, :LANES] / a1[:, LANES:] - lam_ref[0, 0] * (a2[:, :LANES] / a2[:, LANES:])
    ms = jnp.mean(o * o, axis=-1, keepdims=True)
    o_ref[0] = (o * lax.rsqrt(ms + EPS) * g_ref[...] * out_scale).astype(o_ref.dtype)


def diff_attention(q, k, v, lam, g_sub, out_scale, tq, ck):
    b, nq, bw = q.shape
    nk = k.shape[1]
    heads = bw // LANES
    n_chunks = nk // ck
    kt = k.reshape(b, n_chunks, ck, heads, LANES).transpose(0, 3, 1, 4, 2)
    return pl.pallas_call(
        functools.partial(_att_kernel, n_chunks=n_chunks, ck=ck, out_scale=out_scale),
        grid=(b, heads, nq // tq),
        in_specs=[
            pl.BlockSpec(memory_space=pltpu.SMEM),
            pl.BlockSpec((1, tq, LANES), lambda bi, h, i: (bi, i, h)),
            pl.BlockSpec((1, 1, n_chunks, LANES, ck), lambda bi, h, i: (bi, h, 0, 0, 0)),
            pl.BlockSpec((1, nk, LANES), lambda bi, h, i: (bi, 0, h)),
            pl.BlockSpec((1, LANES), lambda bi, h, i: (0, 0)),
        ],
        out_specs=pl.BlockSpec((1, tq, LANES), lambda bi, h, i: (bi, i, h)),
        out_shape=jax.ShapeDtypeStruct((b, nq, bw), BF16),
        scratch_shapes=[
            pltpu.VMEM((2, tq, LANES), F32),
            pltpu.VMEM((2, tq, 2 * LANES), F32),
            pltpu.VMEM((2, 2, tq, ck), F32),
            pltpu.VMEM((2, 2, tq, LANES), F32),
        ],
        compiler_params=_cp("parallel", "parallel", "parallel"),
        name="diff_attention",
    )(lam.reshape(1, 1).astype(F32), q, kt, v, g_sub.reshape(1, LANES))


def _local_kernel(pm_ref, pp_ref, pn_ref, cb_ref, cc_ref, ch_ref, ccp_ref, chp_ref, ccn_ref, chn_ref,
                  pw_ref, ps_ref, cw_ref, yp_ref, yc_ref, scr_ref, *, n_seq, tiles_per_seq):
    ts, bw = pm_ref.shape
    it = pl.program_id(0) % tiles_per_seq
    has_prev = jnp.where(it > 0, 1.0, 0.0).astype(F32)
    has_next = jnp.where(it < tiles_per_seq - 1, 1.0, 0.0).astype(F32)
    gw = bw // len(POOL_WINDOWS)
    pos = it * ts + lax.broadcasted_iota(jnp.int32, (ts, 1), 0)

    scr_ref[0:HALO, :] = pp_ref[...].astype(F32) * has_prev
    scr_ref[HALO:HALO + ts, :] = pm_ref[...].astype(F32)
    scr_ref[HALO + ts:2 * HALO + ts, :] = pn_ref[...].astype(F32) * has_next
    for g, w in enumerate(POOL_WINDOWS):
        sl = slice(g * gw, (g + 1) * gw)
        lo, hi = w // 2, w - 1 - w // 2
        tot = None
        for off in range(-lo, hi + 1):
            part = scr_ref[HALO + off:HALO + off + ts, sl]
            tot = part if tot is None else tot + part
        cnt = (jnp.minimum(pos + hi, n_seq - 1) - jnp.maximum(pos - lo, 0) + 1).astype(F32)
        pooled = tot / cnt - scr_ref[HALO:HALO + ts, sl]
        y = jnp.dot(pooled.astype(BF16), pw_ref[g], preferred_element_type=F32)
        yp_ref[:, sl] = (y * ps_ref[:, sl]).astype(yp_ref.dtype)

    scr_ref[0:HALO, :] = ccp_ref[...].astype(F32) * chp_ref[...].astype(F32) * has_prev
    scr_ref[HALO:HALO + ts, :] = cc_ref[...].astype(F32) * ch_ref[...].astype(F32)
    scr_ref[HALO + ts:2 * HALO + ts, :] = ccn_ref[...].astype(F32) * chn_ref[...].astype(F32) * has_next
    conv = None
    for j in range(CONV_K):
        off = j - CONV_K // 2
        term = cw_ref[j:j + 1, :] * scr_ref[HALO + off:HALO + off + ts, :]
        conv = term if conv is None else conv + term
    yc_ref[...] = (cb_ref[...].astype(F32) * conv).astype(yc_ref.dtype)


def local_mixers(p, n_seq, bw, pool_w, pool_scale, conv_w, ts):
    m = p.shape[0]
    tps = n_seq // ts
    r = ts // HALO
    last_halo = m // HALO - 1

    def main(cb):
        return pl.BlockSpec((ts, bw), lambda i: (i, cb))

    def prev(cb):
        return pl.BlockSpec((HALO, bw), lambda i: (jnp.maximum(i * r - 1, 0), cb))

    def nxt(cb):
        return pl.BlockSpec((HALO, bw), lambda i: (jnp.minimum((i + 1) * r, last_halo), cb))

    n_g = len(POOL_WINDOWS)
    return pl.pallas_call(
        functools.partial(_local_kernel, n_seq=n_seq, tiles_per_seq=tps),
        grid=(m // ts,),
        in_specs=[main(3), prev(3), nxt(3), main(5), main(6), main(7), prev(6), prev(7), nxt(6), nxt(7),
                  pl.BlockSpec((n_g, bw // n_g, bw // n_g), lambda i: (0, 0, 0)),
                  pl.BlockSpec((1, bw), lambda i: (0, 0)),
                  pl.BlockSpec((CONV_K, bw), lambda i: (0, 0))],
        out_specs=[pl.BlockSpec((ts, bw), lambda i: (i, 0)), pl.BlockSpec((ts, bw), lambda i: (i, 0))],
        out_shape=[jax.ShapeDtypeStruct((m, bw), BF16), jax.ShapeDtypeStruct((m, bw), BF16)],
        scratch_shapes=[pltpu.VMEM((ts + 2 * HALO, bw), F32)],
        compiler_params=_cp("parallel"),
        name="local_mixers",
    )(p, p, p, p, p, p, p, p, p, p, pool_w.astype(BF16), pool_scale.reshape(1, bw), conv_w)


def _dft_cos_sin(n, scale):
    k = jnp.arange(n, dtype=jnp.int32)
    ang = ((k[:, None] * k[None, :]) % n).astype(F32) * (2.0 * math.pi / n)
    return jnp.cos(ang) * scale, jnp.sin(ang) * scale


def channel_dft_matrix(bw):
    gw = bw // FOURIER_GROUPS
    c, s = _dft_cos_sin(gw, gw ** -0.5)
    eye = jnp.eye(FOURIER_GROUPS, dtype=F32)
    return jnp.concatenate([jnp.kron(eye, c), jnp.kron(eye, s)], axis=1).astype(BF16)


def _fft_stage2_kernel(p_ref, q_ref, cw_ref, sw_ref, f2_ref, o_ref, b_ref):
    n2 = p_ref.shape[2]
    bw = o_ref.shape[2]
    cw = cw_ref[0]
    sw = sw_ref[0]
    for j in range(bw // LANES):
        lo = slice(j * LANES, (j + 1) * LANES)
        hi = slice(bw + j * LANES, bw + (j + 1) * LANES)
        ar = p_ref[0, 0, :, lo].astype(F32) - q_ref[0, 0, :, hi].astype(F32)
        ai = -(p_ref[0, 0, :, hi].astype(F32) + q_ref[0, 0, :, lo].astype(F32))
        b_ref[0:n2, lo] = (ar * cw + ai * sw).astype(BF16)
        b_ref[n2:2 * n2, lo] = (ai * cw - ar * sw).astype(BF16)
    o_ref[0] = jnp.dot(f2_ref[...], b_ref[...], preferred_element_type=F32).astype(o_ref.dtype)


def fourier_seq_two_stage(z, b, n_seq, bw, n1, n2):
    c1, s1 = _dft_cos_sin(n1, n1 ** -0.5)
    f1 = jnp.concatenate([c1, s1], axis=0).astype(BF16)
    c2, s2 = _dft_cos_sin(n2, n2 ** -0.5)
    f2 = jnp.concatenate([c2, s2], axis=1).astype(BF16)
    k1 = jnp.arange(n1, dtype=jnp.int32)
    t2 = jnp.arange(n2, dtype=jnp.int32)
    ang = (k1[:, None] * t2[None, :]).astype(F32) * (2.0 * math.pi / n_seq)
    cw = jnp.broadcast_to(jnp.cos(ang)[:, :, None], (n1, n2, LANES))
    sw = jnp.broadcast_to(jnp.sin(ang)[:, :, None], (n1, n2, LANES))
    zb = z.reshape(b, n1, n2 * 2 * bw)
    tn1 = _pick(n2 * 2 * bw, (8192, 4096, 2048))
    a = jnp.stack([matmul(f1, zb[i], tm=2 * n1, tn=tn1) for i in range(b)])
    a = a.reshape(b, 2 * n1, n2, 2 * bw)
    out = pl.pallas_call(
        _fft_stage2_kernel,
        grid=(b, n1),
        in_specs=[
            pl.BlockSpec((1, 1, n2, 2 * bw), lambda bi, k: (bi, k, 0, 0)),
            pl.BlockSpec((1, 1, n2, 2 * bw), lambda bi, k: (bi, n1 + k, 0, 0)),
            pl.BlockSpec((1, n2, LANES), lambda bi, k: (k, 0, 0)),
            pl.BlockSpec((1, n2, LANES), lambda bi, k: (k, 0, 0)),
            pl.BlockSpec((n2, 2 * n2), lambda bi, k: (0, 0)),
        ],
        out_specs=pl.BlockSpec((1, n2, bw), lambda bi, k: (bi, 0, k)),
        out_shape=jax.ShapeDtypeStruct((b, n2, n1 * bw), BF16),
        scratch_shapes=[pltpu.VMEM((2 * n2, bw), BF16)],
        compiler_params=_cp("parallel", "parallel"),
        name="fft_stage2",
    )(a, a, cw, sw, f2)
    return out.reshape(b * n_seq, bw)


def fourier_seq_dense(z, b, n_seq, bw):
    c, s = _dft_cos_sin(n_seq, n_seq ** -0.5)
    f = jnp.concatenate([c, -s], axis=1).astype(BF16)
    zb = z.reshape(b, n_seq, 2 * bw)
    outs = [matmul(f, jnp.concatenate([zb[i, :, :bw], zb[i, :, bw:]], axis=0)) for i in range(b)]
    return jnp.concatenate(outs, axis=0)


def _fft_factors(n_seq):
    n2 = LANES
    n1 = n_seq // n2
    return n1, n2


def _pad_last(w, mult):
    pad = (-w.shape[-1]) % mult
    return jnp.pad(w, [(0, 0)] * (w.ndim - 1) + [(0, pad)]) if pad else w


def _pad_rows(w, mult):
    pad = (-w.shape[-2]) % mult
    return jnp.pad(w, [(0, 0)] * (w.ndim - 2) + [(0, pad), (0, 0)]) if pad else w


def kernel(x, c, ctx, c_ctx, w_mod, b_mod, norm_mix, norm_ffn, w_in, w_gate, q_norm, k_norm, lambda_q1, lambda_k1,
           lambda_q2, lambda_k2, subln, pool_w, pool_scale, conv_w, w_branch, w_out, ffn_w1, ffn_w3, ffn_w2,
           router, moe_w1, moe_w3, moe_w2):
    b, n_lat, d = x.shape
    n_ctx = ctx.shape[1]
    depth = w_mod.shape[0]
    bw = d // 4
    dh = q_norm.shape[1]
    m_lat = b * n_lat
    m_ctx = b * n_ctx

    cond8 = jnp.zeros((8, d), F32).at[:b].set(c).at[b].set(c_ctx)
    mod3 = modulation_all(cond8, w_mod, b_mod).reshape(depth * 8, 1, N_MOD * d)

    rope_lat = rope_tables(n_lat, dh)
    rope_ctx = rope_tables(n_ctx, dh)
    w_cdft = channel_dft_matrix(bw)
    n1, n2 = _fft_factors(n_lat)

    h = x.reshape(m_lat, d)
    hc = ctx.reshape(m_ctx, d)

    for l in range(depth):
        last = l == depth - 1
        lam_init = 0.8 - 0.6 * math.exp(-0.3 * l)
        lam = (jnp.exp(jnp.sum(lambda_q1[l] * lambda_k1[l])) - jnp.exp(jnp.sum(lambda_q2[l] * lambda_k2[l]))
               + lam_init)
        w_in_l = w_in[l].astype(BF16)
        w_gate_l = w_gate[l].astype(BF16)
        w_branch_l = w_branch[l].astype(BF16)
        w_out_l = w_out[l].astype(BF16)[None]

        def lat_row(tm, l=l):
            return lambda i: l * 8 + i // (n_lat // tm)

        def ctx_row(tm, l=l):
            return lambda i: l * 8 + b

        def mixer_inputs(h2, n_seq, row_fn, rope, tables):
            tm_norm = _pick(n_seq, (256, 128))
            xm_ = mod_norm(h2, norm_mix[l], mod3, row_fn(tm_norm), 0, 1, tm_norm)
            p_ = matmul(xm_, w_in_l, tm=_pick(n_seq, (1024, 512, 256)))
            q_, k_ = qk_prep(p_, n_seq, bw, q_norm[l], k_norm[l], tables[0], tables[1], rope=rope,
                             q_scale=LOG2_E * dh ** -0.5, tm=_pick(n_seq, (512, 256)))
            return xm_, p_, q_, k_, p_[:, 2 * bw:3 * bw]

        def mixer_output(h2, n_seq, row_fn, xm_, p_, att_, four_):
            pool_, conv_ = local_mixers(p_, n_seq, bw, pool_w[l], pool_scale[l], conv_w[l],
                                        ts=_pick(n_seq, (512, 256)))
            ys = (att_, pool_, four_, conv_)
            tm = _pick(n_seq, (1024, 512, 256))
            merged = merge_branches(xm_, w_gate_l, ys, w_branch_l, tm=tm, tn=512)
            return matmul_residual(merged, w_out_l, h2, mod3, row_fn(tm), 2, tm=tm, tn=512, tk=d)

        xcm, pc, q_c, k_c, v_c = mixer_inputs(hc, n_ctx, ctx_row, False, rope_ctx)
        xm, p, q, k, v = mixer_inputs(h, n_lat, lat_row, True, rope_lat)

        k_all = jnp.concatenate([k_c.reshape(b, n_ctx, bw), k.reshape(b, n_lat, bw)], axis=1)
        v_all = jnp.concatenate([v_c.reshape(b, n_ctx, bw), v.reshape(b, n_lat, bw)], axis=1)
        nk = n_ctx + n_lat
        ck = _pick(nk, (384, 256, 128))
        att = diff_attention(q.reshape(b, n_lat, bw), k_all, v_all, lam, subln[l], 1.0 - lam_init,
                             tq=_pick(n_lat, (512, 256, 128)), ck=ck).reshape(m_lat, bw)
        z_lat = matmul(p, w_cdft, x_col=4, tm=_pick(n_lat, (1024, 512, 256)))
        four = fourier_seq_two_stage(z_lat, b, n_lat, bw, n1, n2)
        h_new = mixer_output(h, n_lat, lat_row, xm, p, att, four)

        if not last:
            att_c = diff_attention(q_c.reshape(b, n_ctx, bw), k_c.reshape(b, n_ctx, bw), v_c.reshape(b, n_ctx, bw),
                                   lam, subln[l], 1.0 - lam_init, tq=_pick(n_ctx, (256, 128)),
                                   ck=_pick(n_ctx, (256, 128))).reshape(m_ctx, bw)
            z_ctx = matmul(pc, w_cdft, x_col=4, tm=_pick(n_ctx, (256, 128)))
            four_c = fourier_seq_dense(z_ctx, b, n_ctx, bw)
            hc = mixer_output(hc, n_ctx, ctx_row, xcm, pc, att_c, four_c)
        h = h_new

        if l % 2 == 0:
            w1 = _pad_last(ffn_w1[l // 2].astype(BF16), 1024)[None]
            w3 = _pad_last(ffn_w3[l // 2].astype(BF16), 1024)[None]
            w2 = _pad_rows(ffn_w2[l // 2].astype(BF16), 1024)[None]
            rt = None
        else:
            w1 = _pad_last(moe_w1[l // 2].astype(BF16), 256)
            w3 = _pad_last(moe_w3[l // 2].astype(BF16), 256)
            w2 = _pad_rows(moe_w2[l // 2].astype(BF16), 256)
            rt = router[l // 2]
        f_pad = w1.shape[2]
        tn_up = _pick(f_pad, (512, 256))
        tk_dn = _pick(f_pad, (2816, 1024, 512, 256))

        def channel_mix(h2, n_seq, row_fn):
            tm_norm = _pick(n_seq, (256, 128))
            zn = mod_norm(h2, norm_ffn[l], mod3, row_fn(tm_norm), 3, 4, tm_norm, router=rt)
            z_, comb = zn if rt is not None else (zn, None)
            tm = _pick(n_seq, (1024, 512, 256))
            if rt is not None and (TOP_K * h2.shape[0]) % MOE_TILE == 0:
                return moe_sparse(z_, comb, rt.shape[1], w1, w3, w2, h2, mod3, row_fn(tm_norm), 5, tm_norm)
            hid = swiglu_up(z_, w1, w3, tm=tm, tn=tn_up, comb=comb)
            return matmul_residual(hid, w2, h2, mod3, row_fn(tm), 5, tm=tm, tn=1024, tk=tk_dn)

        h = channel_mix(h, n_lat, lat_row)
        if not last:
            hc = channel_mix(hc, n_ctx, ctx_row)

    return h.reshape(b, n_lat, d)
```

```python
import functools
import math

import jax
import jax.numpy as jnp
from jax import lax
from jax.experimental import pallas as pl
from jax.experimental.pallas import tpu as pltpu

F32 = jnp.float32
BF16 = jnp.bfloat16

GRID_W = 64
ROPE_THETA = 10000.0
POOL_WINDOWS = (2, 4, 8, 16)
FOURIER_GROUPS = 4
CONV_K = 3
N_MOD = 6
EPS = 1e-6
LANES = 128
HALO = 16
VMEM_LIMIT = 56 * 1024 * 1024
HI = lax.Precision.HIGHEST
LOG2_E = 1.4426950408889634


def _cp(*sem, vmem=VMEM_LIMIT):
    return pltpu.CompilerParams(dimension_semantics=sem, vmem_limit_bytes=vmem)


def _pick(n, prefs):
    for t in prefs:
        if n % t == 0:
            return t
    return n


def _sigmoid(x):
    return 1.0 / (1.0 + jnp.exp(-x))


def _mod_kernel(c_ref, w_ref, b_ref, o_ref):
    x = c_ref[...]
    s = x * _sigmoid(x)
    acc = jnp.dot(s.astype(BF16), w_ref[0].astype(BF16), preferred_element_type=F32)
    o_ref[0] = acc + b_ref[0]


def modulation_all(cond8, w_mod, b_mod):
    depth, d, cols = w_mod.shape
    tn = _pick(cols, (1024, 512, 256, 128))
    return pl.pallas_call(
        _mod_kernel,
        grid=(depth, cols // tn),
        in_specs=[
            pl.BlockSpec((8, d), lambda l, j: (0, 0)),
            pl.BlockSpec((1, d, tn), lambda l, j: (l, 0, j)),
            pl.BlockSpec((1, 1, tn), lambda l, j: (l, 0, j)),
        ],
        out_specs=pl.BlockSpec((1, 8, tn), lambda l, j: (l, 0, j)),
        out_shape=jax.ShapeDtypeStruct((depth, 8, cols), F32),
        compiler_params=_cp("parallel", "parallel"),
        name="modulation",
    )(cond8, w_mod, b_mod.reshape(depth, 1, cols))


def _norm_body(h_ref, g_ref, sh_ref, sc_ref):
    x = h_ref[...]
    ms = jnp.mean(x * x, axis=-1, keepdims=True)
    y = x * lax.rsqrt(ms + EPS) * g_ref[...]
    return y * (1.0 + sc_ref[0]) + sh_ref[0]


def _norm_kernel(h_ref, g_ref, sh_ref, sc_ref, o_ref):
    o_ref[...] = _norm_body(h_ref, g_ref, sh_ref, sc_ref).astype(o_ref.dtype)


def _norm_route_kernel(h_ref, g_ref, sh_ref, sc_ref, r_ref, o_ref, comb_ref, *, n_experts):
    z = _norm_body(h_ref, g_ref, sh_ref, sc_ref)
    o_ref[...] = z.astype(o_ref.dtype)
    logits = jnp.dot(z, r_ref[...], preferred_element_type=F32, precision=HI)
    lane = lax.broadcasted_iota(jnp.int32, logits.shape, 1).astype(F32)
    neg = jnp.float32(-jnp.inf)
    lg = jnp.where(lane < n_experts, logits, neg)
    m1 = jnp.max(lg, axis=-1, keepdims=True)
    i1 = jnp.min(jnp.where(lg == m1, lane, float(LANES)), axis=-1, keepdims=True)
    lg2 = jnp.where(lane == i1, neg, lg)
    m2 = jnp.max(lg2, axis=-1, keepdims=True)
    i2 = jnp.min(jnp.where(lg2 == m2, lane, float(LANES)), axis=-1, keepdims=True)
    e = jnp.exp(m2 - m1)
    w1 = 1.0 / (1.0 + e)
    w2 = e / (1.0 + e)
    comb_ref[...] = jnp.where(lane == i1, w1, 0.0) + jnp.where(lane == i2, w2, 0.0)


def mod_norm(h, g, mod3, row_of_tile, sh_chunk, sc_chunk, tm, router=None):
    m, d = h.shape
    in_specs = [
        pl.BlockSpec((tm, d), lambda i: (i, 0)),
        pl.BlockSpec((1, d), lambda i: (0, 0)),
        pl.BlockSpec((1, 1, d), lambda i: (row_of_tile(i), 0, sh_chunk)),
        pl.BlockSpec((1, 1, d), lambda i: (row_of_tile(i), 0, sc_chunk)),
    ]
    args = [h, g.reshape(1, d), mod3, mod3]
    if router is None:
        return pl.pallas_call(
            _norm_kernel,
            grid=(m // tm,),
            in_specs=in_specs,
            out_specs=pl.BlockSpec((tm, d), lambda i: (i, 0)),
            out_shape=jax.ShapeDtypeStruct((m, d), BF16),
            compiler_params=_cp("parallel"),
            name="mod_norm",
        )(*args)
    n_experts = router.shape[1]
    rpad = jnp.pad(router, ((0, 0), (0, LANES - n_experts)))
    in_specs.append(pl.BlockSpec((d, LANES), lambda i: (0, 0)))
    return pl.pallas_call(
        functools.partial(_norm_route_kernel, n_experts=n_experts),
        grid=(m // tm,),
        in_specs=in_specs,
        out_specs=[pl.BlockSpec((tm, d), lambda i: (i, 0)), pl.BlockSpec((tm, LANES), lambda i: (i, 0))],
        out_shape=[jax.ShapeDtypeStruct((m, d), BF16), jax.ShapeDtypeStruct((m, LANES), F32)],
        compiler_params=_cp("parallel"),
        name="mod_norm_route",
    )(*args, rpad)


def _mm_kernel(x_ref, w_ref, o_ref):
    o_ref[...] = jnp.dot(x_ref[...], w_ref[0], preferred_element_type=F32).astype(o_ref.dtype)


def _mm_batched_kernel(x_ref, w_ref, o_ref):
    o_ref[0] = jnp.dot(x_ref[...], w_ref[0], preferred_element_type=F32).astype(o_ref.dtype)


def matmul(x, w, out_dtype=BF16, x_col=0, tm=None, tn=None, layer=0):
    m = x.shape[0]
    if w.ndim == 2:
        w = w[None]
    _, k, n = w.shape
    tm = tm or _pick(m, (1024, 512, 256, 128))
    tn = tn or _pick(n, (1024, 512, 256, 128))
    return pl.pallas_call(
        _mm_kernel,
        grid=(m // tm, n // tn),
        in_specs=[
            pl.BlockSpec((tm, k), lambda i, j: (i, x_col)),
            pl.BlockSpec((1, k, tn), lambda i, j: (layer, 0, j)),
        ],
        out_specs=pl.BlockSpec((tm, tn), lambda i, j: (i, j)),
        out_shape=jax.ShapeDtypeStruct((m, n), out_dtype),
        compiler_params=_cp("parallel", "parallel"),
        name="matmul",
    )(x, w)


def _mm_res_kernel(x_ref, w_ref, res_ref, gate_ref, o_ref, acc_ref, *, nk):
    k = pl.program_id(2)
    part = jnp.dot(x_ref[...], w_ref[0], preferred_element_type=F32)
    if nk == 1:
        o_ref[...] = res_ref[...] + gate_ref[0] * part
        return

    @pl.when(k == 0)
    def _():
        acc_ref[...] = part

    @pl.when(jnp.logical_and(k > 0, k < nk - 1))
    def _():
        acc_ref[...] += part

    @pl.when(k == nk - 1)
    def _():
        o_ref[...] = res_ref[...] + gate_ref[0] * (acc_ref[...] + part)


def matmul_residual(x, w3, res, mod3, row_of_tile, gate_chunk, tm, tn, tk, first=0, count=None):
    m, n = res.shape
    kf = w3.shape[1]
    n_e = w3.shape[0] - first if count is None else count
    kpe = kf // tk
    nk = n_e * kpe
    return pl.pallas_call(
        functools.partial(_mm_res_kernel, nk=nk),
        grid=(m // tm, n // tn, nk),
        in_specs=[
            pl.BlockSpec((tm, tk), lambda i, j, k: (i, k)),
            pl.BlockSpec((1, tk, tn), lambda i, j, k: (first + k // kpe, k % kpe, j)),
            pl.BlockSpec((tm, tn), lambda i, j, k: (i, j)),
            pl.BlockSpec((1, 1, tn), lambda i, j, k: (row_of_tile(i), 0, gate_chunk * (n // tn) + j)),
        ],
        out_specs=pl.BlockSpec((tm, tn), lambda i, j, k: (i, j)),
        out_shape=jax.ShapeDtypeStruct((m, n), F32),
        scratch_shapes=[pltpu.VMEM((tm, tn), F32)],
        compiler_params=_cp("parallel", "parallel", "arbitrary"),
        name="matmul_residual",
    )(x, w3, res, mod3)


def _swiglu_kernel(z_ref, w1_ref, w3_ref, o_ref):
    z = z_ref[...]
    a = jnp.dot(z, w1_ref[0], preferred_element_type=F32)
    b = jnp.dot(z, w3_ref[0], preferred_element_type=F32)
    o_ref[...] = (a * _sigmoid(a) * b).astype(o_ref.dtype)


def _swiglu_comb_kernel(z_ref, w1_ref, w3_ref, comb_ref, o_ref, *, blocks_per_expert):
    z = z_ref[...]
    a = jnp.dot(z, w1_ref[0], preferred_element_type=F32)
    b = jnp.dot(z, w3_ref[0], preferred_element_type=F32)
    e = pl.program_id(1) // blocks_per_expert
    comb = comb_ref[...]
    lane = lax.broadcasted_iota(jnp.int32, comb.shape, 1)
    scale = jnp.sum(jnp.where(lane == e, comb, 0.0), axis=-1, keepdims=True)
    o_ref[...] = (a * _sigmoid(a) * b * scale).astype(o_ref.dtype)


def swiglu_up(z, w1, w3, tm, tn, comb=None):
    m, k = z.shape
    n_e, _, f = w1.shape
    bpe = f // tn
    in_specs = [
        pl.BlockSpec((tm, k), lambda i, j: (i, 0)),
        pl.BlockSpec((1, k, tn), lambda i, j: (j // bpe, 0, j % bpe)),
        pl.BlockSpec((1, k, tn), lambda i, j: (j // bpe, 0, j % bpe)),
    ]
    args = [z, w1, w3]
    if comb is None:
        kern = _swiglu_kernel
    else:
        kern = functools.partial(_swiglu_comb_kernel, blocks_per_expert=bpe)
        in_specs.append(pl.BlockSpec((tm, LANES), lambda i, j: (i, 0)))
        args.append(comb)
    return pl.pallas_call(
        kern,
        grid=(m // tm, n_e * bpe),
        in_specs=in_specs,
        out_specs=pl.BlockSpec((tm, tn), lambda i, j: (i, j)),
        out_shape=jax.ShapeDtypeStruct((m, n_e * f), BF16),
        compiler_params=_cp("parallel", "parallel"),
        name="swiglu_up",
    )(*args)


MOE_TILE = 1024
TOP_K = 2
TOKEN_CHUNK = 128


def _gather_kernel(idx_ref, src_ref, o_ref, sem, *, rows):
    base = pl.program_id(0) * rows

    def issue(r, carry):
        pltpu.make_async_copy(src_ref.at[idx_ref[base + r]], o_ref.at[r], sem).start()
        return carry

    lax.fori_loop(0, rows, issue, 0)
    pltpu.make_async_copy(src_ref.at[pl.ds(0, rows)], o_ref, sem).wait()


def gather_rows(src, idx, rows_per_step):
    n_src, d = src.shape
    n_out = idx.shape[0]
    src3 = src.reshape(n_src, d // TOKEN_CHUNK, TOKEN_CHUNK)
    out = pl.pallas_call(
        functools.partial(_gather_kernel, rows=rows_per_step),
        grid_spec=pltpu.PrefetchScalarGridSpec(
            num_scalar_prefetch=1,
            grid=(n_out // rows_per_step,),
            in_specs=[pl.BlockSpec(memory_space=pl.ANY)],
            out_specs=pl.BlockSpec((rows_per_step, d // TOKEN_CHUNK, TOKEN_CHUNK), lambda i, idx_ref: (i, 0, 0)),
            scratch_shapes=[pltpu.SemaphoreType.DMA(())],
        ),
        out_shape=jax.ShapeDtypeStruct((n_out, d // TOKEN_CHUNK, TOKEN_CHUNK), src.dtype),
        compiler_params=_cp("arbitrary"),
        name="gather_rows",
    )(idx, src3)
    return out.reshape(n_out, d)


def _moe_up_kernel(te_ref, nu_ref, z_ref, w1_ref, w3_ref, ws_ref, o_ref, *, f_valid):
    used = pl.program_id(0) < nu_ref[0]
    tn = o_ref.shape[1]
    nt = (((1,), (1,)), ((), ()))

    @pl.when(used)
    def _():
        z = z_ref[...]
        a = lax.dot_general(z, w1_ref[0], nt, preferred_element_type=F32)
        b = lax.dot_general(z, w3_ref[0], nt, preferred_element_type=F32)
        scale = jnp.concatenate([ws_ref[...]] * (tn // LANES), axis=1)
        col = pl.program_id(1) * tn + lax.broadcasted_iota(jnp.int32, a.shape, 1)
        val = jnp.where(col < f_valid, a * _sigmoid(a) * b * scale, 0.0)
        o_ref[...] = val.astype(o_ref.dtype)

    @pl.when(jnp.logical_not(used))
    def _():
        o_ref[...] = jnp.zeros(o_ref.shape, o_ref.dtype)


def _moe_down_kernel(te_ref, nu_ref, x_ref, w_ref, o_ref):
    used = pl.program_id(0) < nu_ref[0]

    @pl.when(used)
    def _():
        o_ref[...] = jnp.dot(x_ref[...], w_ref[0], preferred_element_type=F32).astype(o_ref.dtype)

    @pl.when(jnp.logical_not(used))
    def _():
        o_ref[...] = jnp.zeros(o_ref.shape, o_ref.dtype)


def _combine_kernel(h_ref, ya_ref, yb_ref, gate_ref, o_ref):
    o_ref[...] = h_ref[...] + gate_ref[0] * (ya_ref[0].astype(F32) + yb_ref[0].astype(F32))


def moe_sparse(z, comb, n_experts, w1t, w3t, w2, h2, mod3, row_of_tile, gate_chunk, tm):
    m, d = z.shape
    f_valid = w1t.shape[1]
    f = w2.shape[1]
    n_tiles = (TOP_K * m) // MOE_TILE + n_experts
    n_rows = n_tiles * MOE_TILE
    gather_step = _pick(m, (512, 256, 128))

    cw = comb[:, :n_experts]
    sel = cw > 0.0
    seli = sel.astype(jnp.int32)
    rank = jnp.cumsum(seli, axis=0) - seli
    cnt = jnp.sum(seli, axis=0)
    padded = ((cnt + MOE_TILE - 1) // MOE_TILE) * MOE_TILE
    gend = jnp.cumsum(padded)
    slot = (gend - padded)[None, :] + rank
    top_w, top_e = lax.top_k(cw, TOP_K)
    slot_ab = jnp.take_along_axis(jnp.where(sel, slot, n_rows - 1), top_e, axis=1).astype(jnp.int32)
    pair_id = jnp.arange(m * TOP_K, dtype=jnp.int32)
    row_pair = jnp.full((n_rows,), -1, jnp.int32).at[slot_ab.reshape(-1)].set(pair_id)
    row_valid = row_pair >= 0
    src_row = jnp.where(row_valid, row_pair // TOP_K, 0)
    w_slot = jnp.where(row_valid, top_w.reshape(-1)[jnp.maximum(row_pair, 0)], 0.0)
    w_slot = jnp.broadcast_to(w_slot[:, None], (n_rows, LANES))
    tile_start = jnp.arange(n_tiles, dtype=jnp.int32) * MOE_TILE
    tile_expert = jnp.minimum(jnp.sum((tile_start[:, None] >= gend[None, :]).astype(jnp.int32), axis=1),
                              n_experts - 1).astype(jnp.int32)
    n_used = (gend[-1:] // MOE_TILE).astype(jnp.int32)

    zs = gather_rows(z, src_row, gather_step)
    tn_up = _pick(f, (256, 128))
    hid = pl.pallas_call(
        functools.partial(_moe_up_kernel, f_valid=f_valid),
        grid_spec=pltpu.PrefetchScalarGridSpec(
            num_scalar_prefetch=2,
            grid=(n_tiles, f // tn_up),
            in_specs=[
                pl.BlockSpec((MOE_TILE, d), lambda i, j, te, nu: (i, 0)),
                pl.BlockSpec((1, tn_up, d), lambda i, j, te, nu: (te[i], j, 0)),
                pl.BlockSpec((1, tn_up, d), lambda i, j, te, nu: (te[i], j, 0)),
                pl.BlockSpec((MOE_TILE, LANES), lambda i, j, te, nu: (i, 0)),
            ],
            out_specs=pl.BlockSpec((MOE_TILE, tn_up), lambda i, j, te, nu: (i, j)),
        ),
        out_shape=jax.ShapeDtypeStruct((n_rows, f), BF16),
        compiler_params=_cp("parallel", "parallel"),
        name="moe_up",
    )(tile_expert, n_used, zs, w1t, w3t, w_slot)
    tn_dn = _pick(d, (1024, 512, 256))
    ys = pl.pallas_call(
        _moe_down_kernel,
        grid_spec=pltpu.PrefetchScalarGridSpec(
            num_scalar_prefetch=2,
            grid=(n_tiles, d // tn_dn),
            in_specs=[
                pl.BlockSpec((MOE_TILE, f), lambda i, j, te, nu: (i, 0)),
                pl.BlockSpec((1, f, tn_dn), lambda i, j, te, nu: (te[i], 0, j)),
            ],
            out_specs=pl.BlockSpec((MOE_TILE, tn_dn), lambda i, j, te, nu: (i, j)),
        ),
        out_shape=jax.ShapeDtypeStruct((n_rows, d), BF16),
        compiler_params=_cp("parallel", "parallel"),
        name="moe_down",
    )(tile_expert, n_used, hid, w2)
    yab = gather_rows(ys, slot_ab.T.reshape(-1), gather_step).reshape(TOP_K, m, d)
    return pl.pallas_call(
        _combine_kernel,
        grid=(m // tm,),
        in_specs=[
            pl.BlockSpec((tm, d), lambda i: (i, 0)),
            pl.BlockSpec((1, tm, d), lambda i: (0, i, 0)),
            pl.BlockSpec((1, tm, d), lambda i: (1, i, 0)),
            pl.BlockSpec((1, 1, d), lambda i: (row_of_tile(i), 0, gate_chunk)),
        ],
        out_specs=pl.BlockSpec((tm, d), lambda i: (i, 0)),
        out_shape=jax.ShapeDtypeStruct((m, d), F32),
        compiler_params=_cp("parallel"),
        name="moe_combine",
    )(h2, yab, yab, mod3)


def _merge_kernel(xm_ref, wg_ref, *rest, n_branch):
    y_refs = rest[:n_branch]
    wb_ref, o_ref, acc_ref = rest[n_branch:]
    b = pl.program_id(2)
    y = y_refs[n_branch - 1][...]
    for bi in range(n_branch - 2, -1, -1):
        y = jnp.where(b == bi, y_refs[bi][...], y)
    gate = _sigmoid(jnp.dot(xm_ref[...], wg_ref[0], preferred_element_type=F32))
    term = gate * jnp.dot(y, wb_ref[0, 0], preferred_element_type=F32)

    @pl.when(b == 0)
    def _():
        acc_ref[...] = term

    @pl.when(jnp.logical_and(b > 0, b < n_branch - 1))
    def _():
        acc_ref[...] += term

    @pl.when(b == n_branch - 1)
    def _():
        o_ref[...] = (acc_ref[...] + term).astype(o_ref.dtype)


def merge_branches(xm, w_gate, ys, w_branch, tm, tn, layer):
    m, d = xm.shape
    _, n_branch, bw, _ = w_branch.shape
    nj = d // tn
    return pl.pallas_call(
        functools.partial(_merge_kernel, n_branch=n_branch),
        grid=(m // tm, nj, n_branch),
        in_specs=[
            pl.BlockSpec((tm, d), lambda i, j, b: (i, 0)),
            pl.BlockSpec((1, d, tn), lambda i, j, b: (layer, 0, b * nj + j)),
        ] + [pl.BlockSpec((tm, bw), lambda i, j, b: (i, 0))] * n_branch + [
            pl.BlockSpec((1, 1, bw, tn), lambda i, j, b: (layer, b, 0, j)),
        ],
        out_specs=pl.BlockSpec((tm, tn), lambda i, j, b: (i, j)),
        out_shape=jax.ShapeDtypeStruct((m, d), BF16),
        scratch_shapes=[pltpu.VMEM((tm, tn), F32)],
        compiler_params=_cp("parallel", "parallel", "arbitrary"),
        name="merge_branches",
    )(xm, w_gate, *ys, w_branch)


def _qk_prep_kernel(pq_ref, pk_ref, gq_ref, gk_ref, cos_ref, sin_ref, grp_ref, qo_ref, ko_ref, *, rope, q_scale):
    tm = pq_ref.shape[0]
    lane = lax.broadcasted_iota(jnp.int32, (tm, LANES), 1)
    first_half = (lane & 16) == 0
    for src, g_ref, dst, scale in ((pq_ref, gq_ref, qo_ref, q_scale), (pk_ref, gk_ref, ko_ref, 1.0)):
        for j in range(src.shape[1] // LANES):
            sl = slice(j * LANES, (j + 1) * LANES)
            x = src[:, sl].astype(F32)
            ms = jnp.dot(x * x, grp_ref[...], preferred_element_type=F32, precision=HI)
            y = x * lax.rsqrt(ms + EPS) * g_ref[...]
            if rope:
                partner = jnp.where(first_half, pltpu.roll(y, LANES - 16, 1), pltpu.roll(y, 16, 1))
                y = y * cos_ref[...] + partner * sin_ref[...]
            dst[:, sl] = (y * scale).astype(dst.dtype)


def qk_prep(p, n_seq, bw, gq, gk, cos_t, sin_t, rope, q_scale, tm):
    m = p.shape[0]
    dh = gq.shape[0]
    reps = LANES // dh
    idx = jnp.arange(LANES)
    grp = jnp.where((idx[:, None] // dh) == (idx[None, :] // dh), 1.0 / dh, 0.0).astype(F32)
    tiles_per_seq = n_seq // tm
    return pl.pallas_call(
        functools.partial(_qk_prep_kernel, rope=rope, q_scale=q_scale),
        grid=(m // tm,),
        in_specs=[
            pl.BlockSpec((tm, bw), lambda i: (i, 0)),
            pl.BlockSpec((tm, bw), lambda i: (i, 1)),
            pl.BlockSpec((1, LANES), lambda i: (0, 0)),
            pl.BlockSpec((1, LANES), lambda i: (0, 0)),
            pl.BlockSpec((tm, LANES), lambda i: (i % tiles_per_seq, 0)),
            pl.BlockSpec((tm, LANES), lambda i: (i % tiles_per_seq, 0)),
            pl.BlockSpec((LANES, LANES), lambda i: (0, 0)),
        ],
        out_specs=[pl.BlockSpec((tm, bw), lambda i: (i, 0)), pl.BlockSpec((tm, bw), lambda i: (i, 0))],
        out_shape=[jax.ShapeDtypeStruct((m, bw), BF16), jax.ShapeDtypeStruct((m, bw), BF16)],
        compiler_params=_cp("parallel"),
        name="qk_prep",
    )(p, p, jnp.tile(gq, reps).reshape(1, LANES), jnp.tile(gk, reps).reshape(1, LANES), cos_t, sin_t, grp)


def rope_tables(n_seq, dh):
    nf = dh // 4
    pos = jnp.arange(n_seq)
    row = (pos // GRID_W).astype(F32)
    col = (pos % GRID_W).astype(F32)
    freqs = ROPE_THETA ** (-jnp.arange(nf, dtype=F32) / nf)
    lane = jnp.arange(LANES)
    d = lane % dh
    use_col = (d // (2 * nf)) == 1
    second = ((d // nf) % 2) == 1
    f = freqs[d % nf]
    ang = jnp.where(use_col[None, :], col[:, None], row[:, None]) * f[None, :]
    return jnp.cos(ang), jnp.where(second[None, :], 1.0, -1.0) * jnp.sin(ang)


def _att_kernel(lam_ref, q_ref, kt_ref, v_ref, g_ref, o_ref, m_ref, acc_ref, s_ref, mc_ref, *, n_chunks, ck,
                out_scale):
    tq = q_ref.shape[1]
    half = LANES // 2
    n_tiles = ck // LANES
    q = q_ref[0]
    lane = lax.broadcasted_iota(jnp.int32, (tq, LANES), 1)
    zero = jnp.zeros_like(q)
    qs = (jnp.where(lane < half, q, zero), jnp.where(lane >= half, q, zero))
    m_ref[...] = jnp.full(m_ref.shape, -jnp.inf, F32)
    acc_ref[...] = jnp.zeros(acc_ref.shape, F32)
    ones = jnp.ones((ck, LANES), BF16)

    def scores(c, slot):
        kt = kt_ref[0, 0, c]
        for ci in range(2):
            s = jnp.dot(qs[ci], kt, preferred_element_type=F32)
            s_ref[slot, ci] = s
            mc = s[:, :LANES]
            for j in range(1, n_tiles):
                mc = jnp.maximum(mc, s[:, j * LANES:(j + 1) * LANES])
            mc_ref[slot, ci] = mc

    def softmax_pv(c, slot):
        v = jnp.concatenate([v_ref[0, pl.ds(pl.multiple_of(c * ck, ck), ck), :], ones], axis=1)
        for ci in range(2):
            m_old = m_ref[ci]
            m_new = jnp.maximum(m_old, jnp.max(mc_ref[slot, ci], axis=-1, keepdims=True))
            alpha = jnp.exp2(m_old - m_new)
            p = jnp.concatenate(
                [jnp.exp2(s_ref[slot, ci, :, j * LANES:(j + 1) * LANES] - m_new).astype(BF16)
                 for j in range(n_tiles)], axis=1)
            pv = jnp.dot(p, v, preferred_element_type=F32)
            acc_ref[ci] = jnp.concatenate([alpha, alpha], axis=1) * acc_ref[ci] + pv
            m_ref[ci] = m_new

    scores(0, 0)
    n_pairs = (n_chunks - 1) // 2

    def body(i, carry):
        c = 2 * i
        scores(c + 1, 1)
        softmax_pv(c, 0)
        scores(c + 2, 0)
        softmax_pv(c + 1, 1)
        return carry

    if n_pairs > 0:
        lax.fori_loop(0, n_pairs, body, 0)
    if (n_chunks - 1) % 2 == 1:
        scores(n_chunks - 1, 1)
        softmax_pv(n_chunks - 2, 0)
        softmax_pv(n_chunks - 1, 1)
    else:
        softmax_pv(n_chunks - 1, 0)

    a1 = acc_ref[0]
    a2 = acc_ref[1]
    o = a1[:, :LANES] / a1[:, LANES:] - lam_ref[0, 0] * (a2[:, :LANES] / a2[:, LANES:])
    ms = jnp.mean(o * o, axis=-1, keepdims=True)
    o_ref[0] = (o * lax.rsqrt(ms + EPS) * g_ref[...] * out_scale).astype(o_ref.dtype)


def diff_attention(q, k, v, lam, g_sub, out_scale, tq, ck):
    b, nq, bw = q.shape
    nk = k.shape[1]
    heads = bw // LANES
    n_chunks = nk // ck
    kt = k.reshape(b, n_chunks, ck, heads, LANES).transpose(0, 3, 1, 4, 2)
    return pl.pallas_call(
        functools.partial(_att_kernel, n_chunks=n_chunks, ck=ck, out_scale=out_scale),
        grid=(b, heads, nq // tq),
        in_specs=[
            pl.BlockSpec(memory_space=pltpu.SMEM),
            pl.BlockSpec((1, tq, LANES), lambda bi, h, i: (bi, i, h)),
            pl.BlockSpec((1, 1, n_chunks, LANES, ck), lambda bi, h, i: (bi, h, 0, 0, 0)),
            pl.BlockSpec((1, nk, LANES), lambda bi, h, i: (bi, 0, h)),
            pl.BlockSpec((1, LANES), lambda bi, h, i: (0, 0)),
        ],
        out_specs=pl.BlockSpec((1, tq, LANES), lambda bi, h, i: (bi, i, h)),
        out_shape=jax.ShapeDtypeStruct((b, nq, bw), BF16),
        scratch_shapes=[
            pltpu.VMEM((2, tq, LANES), F32),
            pltpu.VMEM((2, tq, 2 * LANES), F32),
            pltpu.VMEM((2, 2, tq, ck), F32),
            pltpu.VMEM((2, 2, tq, LANES), F32),
        ],
        compiler_params=_cp("parallel", "parallel", "parallel"),
        name="diff_attention",
    )(lam.reshape(1, 1).astype(F32), q, kt, v, g_sub.reshape(1, LANES))


def _local_kernel(pm_ref, pp_ref, pn_ref, cb_ref, cc_ref, ch_ref, ccp_ref, chp_ref, ccn_ref, chn_ref,
                  pw_ref, ps_ref, cw_ref, yp_ref, yc_ref, scr_ref, *, n_seq, tiles_per_seq):
    ts, bw = pm_ref.shape
    it = pl.program_id(0) % tiles_per_seq
    has_prev = jnp.where(it > 0, 1.0, 0.0).astype(F32)
    has_next = jnp.where(it < tiles_per_seq - 1, 1.0, 0.0).astype(F32)
    gw = bw // len(POOL_WINDOWS)
    pos = it * ts + lax.broadcasted_iota(jnp.int32, (ts, 1), 0)

    scr_ref[0:HALO, :] = pp_ref[...].astype(F32) * has_prev
    scr_ref[HALO:HALO + ts, :] = pm_ref[...].astype(F32)
    scr_ref[HALO + ts:2 * HALO + ts, :] = pn_ref[...].astype(F32) * has_next
    for g, w in enumerate(POOL_WINDOWS):
        sl = slice(g * gw, (g + 1) * gw)
        lo, hi = w // 2, w - 1 - w // 2
        tot = None
        for off in range(-lo, hi + 1):
            part = scr_ref[HALO + off:HALO + off + ts, sl]
            tot = part if tot is None else tot + part
        cnt = (jnp.minimum(pos + hi, n_seq - 1) - jnp.maximum(pos - lo, 0) + 1).astype(F32)
        pooled = tot / cnt - scr_ref[HALO:HALO + ts, sl]
        y = jnp.dot(pooled.astype(BF16), pw_ref[g], preferred_element_type=F32)
        yp_ref[:, sl] = (y * ps_ref[:, sl]).astype(yp_ref.dtype)

    scr_ref[0:HALO, :] = ccp_ref[...].astype(F32) * chp_ref[...].astype(F32) * has_prev
    scr_ref[HALO:HALO + ts, :] = cc_ref[...].astype(F32) * ch_ref[...].astype(F32)
    scr_ref[HALO + ts:2 * HALO + ts, :] = ccn_ref[...].astype(F32) * chn_ref[...].astype(F32) * has_next
    conv = None
    for j in range(CONV_K):
        off = j - CONV_K // 2
        term = cw_ref[j:j + 1, :] * scr_ref[HALO + off:HALO + off + ts, :]
        conv = term if conv is None else conv + term
    yc_ref[...] = (cb_ref[...].astype(F32) * conv).astype(yc_ref.dtype)


def local_mixers(p, n_seq, bw, pool_w, pool_scale, conv_w, ts):
    m = p.shape[0]
    tps = n_seq // ts
    r = ts // HALO
    last_halo = m // HALO - 1

    def main(cb):
        return pl.BlockSpec((ts, bw), lambda i: (i, cb))

    def prev(cb):
        return pl.BlockSpec((HALO, bw), lambda i: (jnp.maximum(i * r - 1, 0), cb))

    def nxt(cb):
        return pl.BlockSpec((HALO, bw), lambda i: (jnp.minimum((i + 1) * r, last_halo), cb))

    n_g = len(POOL_WINDOWS)
    return pl.pallas_call(
        functools.partial(_local_kernel, n_seq=n_seq, tiles_per_seq=tps),
        grid=(m // ts,),
        in_specs=[main(3), prev(3), nxt(3), main(5), main(6), main(7), prev(6), prev(7), nxt(6), nxt(7),
                  pl.BlockSpec((n_g, bw // n_g, bw // n_g), lambda i: (0, 0, 0)),
                  pl.BlockSpec((1, bw), lambda i: (0, 0)),
                  pl.BlockSpec((CONV_K, bw), lambda i: (0, 0))],
        out_specs=[pl.BlockSpec((ts, bw), lambda i: (i, 0)), pl.BlockSpec((ts, bw), lambda i: (i, 0))],
        out_shape=[jax.ShapeDtypeStruct((m, bw), BF16), jax.ShapeDtypeStruct((m, bw), BF16)],
        scratch_shapes=[pltpu.VMEM((ts + 2 * HALO, bw), F32)],
        compiler_params=_cp("parallel"),
        name="local_mixers",
    )(p, p, p, p, p, p, p, p, p, p, pool_w.astype(BF16), pool_scale.reshape(1, bw), conv_w)


def _dft_cos_sin(n, scale):
    k = jnp.arange(n, dtype=jnp.int32)
    ang = ((k[:, None] * k[None, :]) % n).astype(F32) * (2.0 * math.pi / n)
    return jnp.cos(ang) * scale, jnp.sin(ang) * scale


def channel_dft_matrix(bw):
    gw = bw // FOURIER_GROUPS
    c, s = _dft_cos_sin(gw, gw ** -0.5)
    eye = jnp.eye(FOURIER_GROUPS, dtype=F32)
    return jnp.concatenate([jnp.kron(eye, c), jnp.kron(eye, s)], axis=1).astype(BF16)


def _fft_stage2_kernel(p_ref, q_ref, cw_ref, sw_ref, f2_ref, o_ref, b_ref):
    n2 = p_ref.shape[2]
    bw = o_ref.shape[2]
    cw = cw_ref[0]
    sw = sw_ref[0]
    for j in range(bw // LANES):
        lo = slice(j * LANES, (j + 1) * LANES)
        hi = slice(bw + j * LANES, bw + (j + 1) * LANES)
        ar = p_ref[0, 0, :, lo].astype(F32) - q_ref[0, 0, :, hi].astype(F32)
        ai = -(p_ref[0, 0, :, hi].astype(F32) + q_ref[0, 0, :, lo].astype(F32))
        b_ref[0:n2, lo] = (ar * cw + ai * sw).astype(BF16)
        b_ref[n2:2 * n2, lo] = (ai * cw - ar * sw).astype(BF16)
    o_ref[0] = jnp.dot(f2_ref[...], b_ref[...], preferred_element_type=F32).astype(o_ref.dtype)


def fourier_seq_two_stage(z, b, n_seq, bw, n1, n2):
    c1, s1 = _dft_cos_sin(n1, n1 ** -0.5)
    f1 = jnp.concatenate([c1, s1], axis=0).astype(BF16)
    c2, s2 = _dft_cos_sin(n2, n2 ** -0.5)
    f2 = jnp.concatenate([c2, s2], axis=1).astype(BF16)
    k1 = jnp.arange(n1, dtype=jnp.int32)
    t2 = jnp.arange(n2, dtype=jnp.int32)
    ang = (k1[:, None] * t2[None, :]).astype(F32) * (2.0 * math.pi / n_seq)
    cw = jnp.broadcast_to(jnp.cos(ang)[:, :, None], (n1, n2, LANES))
    sw = jnp.broadcast_to(jnp.sin(ang)[:, :, None], (n1, n2, LANES))
    zb = z.reshape(b, n1, n2 * 2 * bw)
    tn1 = _pick(n2 * 2 * bw, (8192, 4096, 2048))
    a = pl.pallas_call(
        _mm_batched_kernel,
        grid=(b, (n2 * 2 * bw) // tn1),
        in_specs=[
            pl.BlockSpec((2 * n1, n1), lambda bi, j: (0, 0)),
            pl.BlockSpec((1, n1, tn1), lambda bi, j: (bi, 0, j)),
        ],
        out_specs=pl.BlockSpec((1, 2 * n1, tn1), lambda bi, j: (bi, 0, j)),
        out_shape=jax.ShapeDtypeStruct((b, 2 * n1, n2 * 2 * bw), BF16),
        compiler_params=_cp("parallel", "parallel"),
        name="fft_stage1",
    )(f1, zb)
    a = a.reshape(b, 2 * n1, n2, 2 * bw)
    out = pl.pallas_call(
        _fft_stage2_kernel,
        grid=(b, n1),
        in_specs=[
            pl.BlockSpec((1, 1, n2, 2 * bw), lambda bi, k: (bi, k, 0, 0)),
            pl.BlockSpec((1, 1, n2, 2 * bw), lambda bi, k: (bi, n1 + k, 0, 0)),
            pl.BlockSpec((1, n2, LANES), lambda bi, k: (k, 0, 0)),
            pl.BlockSpec((1, n2, LANES), lambda bi, k: (k, 0, 0)),
            pl.BlockSpec((n2, 2 * n2), lambda bi, k: (0, 0)),
        ],
        out_specs=pl.BlockSpec((1, n2, bw), lambda bi, k: (bi, 0, k)),
        out_shape=jax.ShapeDtypeStruct((b, n2, n1 * bw), BF16),
        scratch_shapes=[pltpu.VMEM((2 * n2, bw), BF16)],
        compiler_params=_cp("parallel", "parallel"),
        name="fft_stage2",
    )(a, a, cw, sw, f2)
    return out.reshape(b * n_seq, bw)


def fourier_seq_dense(z, b, n_seq, bw):
    c, s = _dft_cos_sin(n_seq, n_seq ** -0.5)
    f = jnp.concatenate([c, -s], axis=1).astype(BF16)
    zb = z.reshape(b, n_seq, 2 * bw)
    outs = [matmul(f, jnp.concatenate([zb[i, :, :bw], zb[i, :, bw:]], axis=0)) for i in range(b)]
    return jnp.concatenate(outs, axis=0)


def _fft_factors(n_seq):
    n2 = LANES
    n1 = n_seq // n2
    return n1, n2


def _pad_last(w, mult):
    pad = (-w.shape[-1]) % mult
    return jnp.pad(w, [(0, 0)] * (w.ndim - 1) + [(0, pad)]) if pad else w


def _pad_rows(w, mult):
    pad = (-w.shape[-2]) % mult
    return jnp.pad(w, [(0, 0)] * (w.ndim - 2) + [(0, pad), (0, 0)]) if pad else w


def kernel(x, c, ctx, c_ctx, w_mod, b_mod, norm_mix, norm_ffn, w_in, w_gate, q_norm, k_norm, lambda_q1, lambda_k1,
           lambda_q2, lambda_k2, subln, pool_w, pool_scale, conv_w, w_branch, w_out, ffn_w1, ffn_w3, ffn_w2,
           router, moe_w1, moe_w3, moe_w2):
    b, n_lat, d = x.shape
    n_ctx = ctx.shape[1]
    depth = w_mod.shape[0]
    bw = d // 4
    dh = q_norm.shape[1]
    m_lat = b * n_lat
    m_ctx = b * n_ctx

    cond8 = jnp.zeros((8, d), F32).at[:b].set(c).at[b].set(c_ctx)
    mod3 = modulation_all(cond8, w_mod, b_mod).reshape(depth * 8, 1, N_MOD * d)

    rope_lat = rope_tables(n_lat, dh)
    rope_ctx = rope_tables(n_ctx, dh)
    w_cdft = channel_dft_matrix(bw)
    n1, n2 = _fft_factors(n_lat)

    h = x.reshape(m_lat, d)
    hc = ctx.reshape(m_ctx, d)

    w_in_b = w_in.astype(BF16)
    w_gate_b = w_gate.astype(BF16)
    w_branch_b = w_branch.astype(BF16)
    w_out_b = w_out.astype(BF16)

    for l in range(depth):
        last = l == depth - 1
        lam_init = 0.8 - 0.6 * math.exp(-0.3 * l)
        lam = (jnp.exp(jnp.sum(lambda_q1[l] * lambda_k1[l])) - jnp.exp(jnp.sum(lambda_q2[l] * lambda_k2[l]))
               + lam_init)

        def lat_row(tm, l=l):
            return lambda i: l * 8 + i // (n_lat // tm)

        def ctx_row(tm, l=l):
            return lambda i: l * 8 + b

        def mixer_inputs(h2, n_seq, row_fn, rope, tables):
            tm_norm = _pick(n_seq, (256, 128))
            xm_ = mod_norm(h2, norm_mix[l], mod3, row_fn(tm_norm), 0, 1, tm_norm)
            p_ = matmul(xm_, w_in_b, tm=_pick(n_seq, (1024, 512, 256)), layer=l)
            q_, k_ = qk_prep(p_, n_seq, bw, q_norm[l], k_norm[l], tables[0], tables[1], rope=rope,
                             q_scale=LOG2_E * dh ** -0.5, tm=_pick(n_seq, (512, 256)))
            return xm_, p_, q_, k_, p_[:, 2 * bw:3 * bw]

        def mixer_output(h2, n_seq, row_fn, xm_, p_, att_, four_):
            pool_, conv_ = local_mixers(p_, n_seq, bw, pool_w[l], pool_scale[l], conv_w[l],
                                        ts=_pick(n_seq, (512, 256)))
            ys = (att_, pool_, four_, conv_)
            tm = _pick(n_seq, (1024, 512, 256))
            merged = merge_branches(xm_, w_gate_b, ys, w_branch_b, tm=tm, tn=512, layer=l)
            return matmul_residual(merged, w_out_b, h2, mod3, row_fn(tm), 2, tm=tm, tn=512, tk=d, first=l, count=1)

        xcm, pc, q_c, k_c, v_c = mixer_inputs(hc, n_ctx, ctx_row, False, rope_ctx)
        xm, p, q, k, v = mixer_inputs(h, n_lat, lat_row, True, rope_lat)

        k_all = jnp.concatenate([k_c.reshape(b, n_ctx, bw), k.reshape(b, n_lat, bw)], axis=1)
        v_all = jnp.concatenate([v_c.reshape(b, n_ctx, bw), v.reshape(b, n_lat, bw)], axis=1)
        nk = n_ctx + n_lat
        ck = _pick(nk, (1408, 768, 384, 256, 128))
        att = diff_attention(q.reshape(b, n_lat, bw), k_all, v_all, lam, subln[l], 1.0 - lam_init,
                             tq=_pick(n_lat, (512, 256, 128)), ck=ck).reshape(m_lat, bw)
        z_lat = matmul(p, w_cdft, x_col=4, tm=_pick(n_lat, (1024, 512, 256)))
        four = fourier_seq_two_stage(z_lat, b, n_lat, bw, n1, n2)
        h_new = mixer_output(h, n_lat, lat_row, xm, p, att, four)

        if not last:
            att_c = diff_attention(q_c.reshape(b, n_ctx, bw), k_c.reshape(b, n_ctx, bw), v_c.reshape(b, n_ctx, bw),
                                   lam, subln[l], 1.0 - lam_init, tq=_pick(n_ctx, (256, 128)),
                                   ck=_pick(n_ctx, (256, 128))).reshape(m_ctx, bw)
            z_ctx = matmul(pc, w_cdft, x_col=4, tm=_pick(n_ctx, (256, 128)))
            four_c = fourier_seq_dense(z_ctx, b, n_ctx, bw)
            hc = mixer_output(hc, n_ctx, ctx_row, xcm, pc, att_c, four_c)
        h = h_new

        if l % 2 == 0:
            w1 = _pad_last(ffn_w1[l // 2].astype(BF16), 1024)[None]
            w3 = _pad_last(ffn_w3[l // 2].astype(BF16), 1024)[None]
            w2 = _pad_rows(ffn_w2[l // 2].astype(BF16), 1024)[None]
            rt = None
        else:
            w1 = w3 = None
            w2 = _pad_rows(moe_w2[l // 2].astype(BF16), 256)
            rt = router[l // 2]
        f_pad = w2.shape[1]
        tn_up = _pick(f_pad, (512, 256))
        tk_dn = _pick(f_pad, (2816, 1024, 512, 256))

        def channel_mix(h2, n_seq, row_fn):
            tm_norm = _pick(n_seq, (256, 128))
            zn = mod_norm(h2, norm_ffn[l], mod3, row_fn(tm_norm), 3, 4, tm_norm, router=rt)
            z_, comb = zn if rt is not None else (zn, None)
            tm = _pick(n_seq, (1024, 512, 256))
            if rt is not None and (TOP_K * h2.shape[0]) % MOE_TILE == 0:
                w1t = jnp.swapaxes(moe_w1[l // 2], 1, 2).astype(BF16)
                w3t = jnp.swapaxes(moe_w3[l // 2], 1, 2).astype(BF16)
                return moe_sparse(z_, comb, rt.shape[1], w1t, w3t, w2, h2, mod3, row_fn(tm_norm), 5, tm_norm)
            if rt is not None:
                w1_, w3_ = (_pad_last(w[l // 2].astype(BF16), 256) for w in (moe_w1, moe_w3))
            else:
                w1_, w3_ = w1, w3
            hid = swiglu_up(z_, w1_, w3_, tm=tm, tn=tn_up, comb=comb)
            return matmul_residual(hid, w2, h2, mod3, row_fn(tm), 5, tm=tm, tn=1024, tk=tk_dn)

        h = channel_mix(h, n_lat, lat_row)
        if not last:
            hc = channel_mix(hc, n_ctx, ctx_row)

    return h.reshape(b, n_lat, d)
```

```python
import functools
import math

import jax
import jax.numpy as jnp
from jax import lax
from jax.experimental import pallas as pl
from jax.experimental.pallas import tpu as pltpu

F32 = jnp.float32
BF16 = jnp.bfloat16

GRID_W = 64
ROPE_THETA = 10000.0
POOL_WINDOWS = (2, 4, 8, 16)
FOURIER_GROUPS = 4
CONV_K = 3
N_MOD = 6
EPS = 1e-6
LANES = 128
HALO = 16
VMEM_LIMIT = 56 * 1024 * 1024
HI = lax.Precision.HIGHEST
LOG2_E = 1.4426950408889634
SUB_N = 256


def _cp(*sem, vmem=VMEM_LIMIT):
    return pltpu.CompilerParams(dimension_semantics=sem, vmem_limit_bytes=vmem)


def _pick(n, prefs):
    for t in prefs:
        if n % t == 0:
            return t
    return n


def _sigmoid(x):
    return 1.0 / (1.0 + jnp.exp(-x))


def _mod_kernel(c_ref, w_ref, b_ref, o_ref):
    x = c_ref[...]
    s = x * _sigmoid(x)
    acc = jnp.dot(s.astype(BF16), w_ref[0].astype(BF16), preferred_element_type=F32)
    o_ref[0] = acc + b_ref[0]


def modulation_all(cond8, w_mod, b_mod):
    depth, d, cols = w_mod.shape
    tn = _pick(cols, (1024, 512, 256, 128))
    return pl.pallas_call(
        _mod_kernel,
        grid=(depth, cols // tn),
        in_specs=[
            pl.BlockSpec((8, d), lambda l, j: (0, 0)),
            pl.BlockSpec((1, d, tn), lambda l, j: (l, 0, j)),
            pl.BlockSpec((1, 1, tn), lambda l, j: (l, 0, j)),
        ],
        out_specs=pl.BlockSpec((1, 8, tn), lambda l, j: (l, 0, j)),
        out_shape=jax.ShapeDtypeStruct((depth, 8, cols), F32),
        compiler_params=_cp("parallel", "parallel"),
        name="modulation",
    )(cond8, w_mod, b_mod.reshape(depth, 1, cols))


def _norm_body(h_ref, g_ref, sh_ref, sc_ref):
    x = h_ref[...]
    ms = jnp.mean(x * x, axis=-1, keepdims=True)
    y = x * lax.rsqrt(ms + EPS) * g_ref[...]
    return y * (1.0 + sc_ref[0]) + sh_ref[0]


def _norm_kernel(h_ref, g_ref, sh_ref, sc_ref, o_ref):
    o_ref[...] = _norm_body(h_ref, g_ref, sh_ref, sc_ref).astype(o_ref.dtype)


def _norm_route_kernel(h_ref, g_ref, sh_ref, sc_ref, r_ref, o_ref, comb_ref, *, n_experts):
    z = _norm_body(h_ref, g_ref, sh_ref, sc_ref)
    o_ref[...] = z.astype(o_ref.dtype).reshape(o_ref.shape)
    logits = jnp.dot(z, r_ref[...], preferred_element_type=F32, precision=HI)
    lane = lax.broadcasted_iota(jnp.int32, logits.shape, 1).astype(F32)
    neg = jnp.float32(-jnp.inf)
    lg = jnp.where(lane < n_experts, logits, neg)
    m1 = jnp.max(lg, axis=-1, keepdims=True)
    i1 = jnp.min(jnp.where(lg == m1, lane, float(LANES)), axis=-1, keepdims=True)
    lg2 = jnp.where(lane == i1, neg, lg)
    m2 = jnp.max(lg2, axis=-1, keepdims=True)
    i2 = jnp.min(jnp.where(lg2 == m2, lane, float(LANES)), axis=-1, keepdims=True)
    e = jnp.exp(m2 - m1)
    w1 = 1.0 / (1.0 + e)
    w2 = e / (1.0 + e)
    comb_ref[...] = jnp.where(lane == i1, w1, 0.0) + jnp.where(lane == i2, w2, 0.0)


def mod_norm(h, g, mod3, row_of_tile, sh_chunk, sc_chunk, tm, router=None):
    m, d = h.shape
    in_specs = [
        pl.BlockSpec((tm, d), lambda i: (i, 0)),
        pl.BlockSpec((1, d), lambda i: (0, 0)),
        pl.BlockSpec((1, 1, d), lambda i: (row_of_tile(i), 0, sh_chunk)),
        pl.BlockSpec((1, 1, d), lambda i: (row_of_tile(i), 0, sc_chunk)),
    ]
    args = [h, g.reshape(1, d), mod3, mod3]
    if router is None:
        return pl.pallas_call(
            _norm_kernel,
            grid=(m // tm,),
            in_specs=in_specs,
            out_specs=pl.BlockSpec((tm, d), lambda i: (i, 0)),
            out_shape=jax.ShapeDtypeStruct((m, d), BF16),
            compiler_params=_cp("parallel"),
            name="mod_norm",
        )(*args)
    n_experts = router.shape[1]
    rpad = jnp.pad(router, ((0, 0), (0, LANES - n_experts)))
    in_specs.append(pl.BlockSpec((d, LANES), lambda i: (0, 0)))
    return pl.pallas_call(
        functools.partial(_norm_route_kernel, n_experts=n_experts),
        grid=(m // tm,),
        in_specs=in_specs,
        out_specs=[pl.BlockSpec((tm, d // LANES, LANES), lambda i: (i, 0, 0)),
                   pl.BlockSpec((tm, LANES), lambda i: (i, 0))],
        out_shape=[jax.ShapeDtypeStruct((m, d // LANES, LANES), BF16), jax.ShapeDtypeStruct((m, LANES), F32)],
        compiler_params=_cp("parallel"),
        name="mod_norm_route",
    )(*args, rpad)


def _mm_kernel(x_ref, w_ref, o_ref):
    o_ref[...] = jnp.dot(x_ref[...], w_ref[0], preferred_element_type=F32).astype(o_ref.dtype)


def _mm_batched_kernel(x_ref, w_ref, o_ref):
    o_ref[0] = jnp.dot(x_ref[...], w_ref[0], preferred_element_type=F32).astype(o_ref.dtype)


def matmul(x, w, out_dtype=BF16, x_col=0, tm=None, tn=None, layer=0):
    m = x.shape[0]
    if w.ndim == 2:
        w = w[None]
    _, k, n = w.shape
    tm = tm or _pick(m, (1024, 512, 256, 128))
    tn = tn or _pick(n, (1024, 512, 256, 128))
    return pl.pallas_call(
        _mm_kernel,
        grid=(m // tm, n // tn),
        in_specs=[
            pl.BlockSpec((tm, k), lambda i, j: (i, x_col)),
            pl.BlockSpec((1, k, tn), lambda i, j: (layer, 0, j)),
        ],
        out_specs=pl.BlockSpec((tm, tn), lambda i, j: (i, j)),
        out_shape=jax.ShapeDtypeStruct((m, n), out_dtype),
        compiler_params=_cp("parallel", "parallel"),
        name="matmul",
    )(x, w)


def _mm_res_kernel(x_ref, w_ref, res_ref, gate_ref, o_ref, acc_ref, *, nk):
    k = pl.program_id(2)
    sub = _pick(o_ref.shape[1], (SUB_N, LANES))
    if nk > 1:

        @pl.when(k == 0)
        def _():
            acc_ref[...] = jnp.zeros(acc_ref.shape, F32)

    for s in range(o_ref.shape[1] // sub):
        sl = slice(s * sub, (s + 1) * sub)
        part = jnp.dot(x_ref[...], w_ref[0, :, sl], preferred_element_type=F32)
        if nk > 1:
            part = acc_ref[:, sl] + part
            acc_ref[:, sl] = part
        o_ref[:, sl] = res_ref[:, sl] + gate_ref[0, :, sl] * part


def matmul_residual(x, w3, res, mod3, row_of_tile, gate_chunk, tm, tn, tk, first=0, count=None):
    m, n = res.shape
    kf = w3.shape[1]
    n_e = w3.shape[0] - first if count is None else count
    kpe = kf // tk
    nk = n_e * kpe
    return pl.pallas_call(
        functools.partial(_mm_res_kernel, nk=nk),
        grid=(m // tm, n // tn, nk),
        in_specs=[
            pl.BlockSpec((tm, tk), lambda i, j, k: (i, k)),
            pl.BlockSpec((1, tk, tn), lambda i, j, k: (first + k // kpe, k % kpe, j)),
            pl.BlockSpec((tm, tn), lambda i, j, k: (i, j)),
            pl.BlockSpec((1, 1, tn), lambda i, j, k: (row_of_tile(i), 0, gate_chunk * (n // tn) + j)),
        ],
        out_specs=pl.BlockSpec((tm, tn), lambda i, j, k: (i, j)),
        out_shape=jax.ShapeDtypeStruct((m, n), F32),
        scratch_shapes=[pltpu.VMEM((tm, tn), F32)],
        compiler_params=_cp("parallel", "parallel", "arbitrary"),
        name="matmul_residual",
    )(x, w3, res, mod3)


def _swiglu_kernel(z_ref, w1_ref, w3_ref, o_ref):
    sub = _pick(o_ref.shape[1], (SUB_N, LANES))
    for s in range(o_ref.shape[1] // sub):
        sl = slice(s * sub, (s + 1) * sub)
        a = jnp.dot(z_ref[...], w1_ref[0, :, sl], preferred_element_type=F32)
        b = jnp.dot(z_ref[...], w3_ref[0, :, sl], preferred_element_type=F32)
        o_ref[:, sl] = (a * _sigmoid(a) * b).astype(o_ref.dtype)


def _swiglu_comb_kernel(z_ref, w1_ref, w3_ref, comb_ref, o_ref, *, blocks_per_expert):
    z = z_ref[...]
    a = jnp.dot(z, w1_ref[0], preferred_element_type=F32)
    b = jnp.dot(z, w3_ref[0], preferred_element_type=F32)
    e = pl.program_id(1) // blocks_per_expert
    comb = comb_ref[...]
    lane = lax.broadcasted_iota(jnp.int32, comb.shape, 1)
    scale = jnp.sum(jnp.where(lane == e, comb, 0.0), axis=-1, keepdims=True)
    o_ref[...] = (a * _sigmoid(a) * b * scale).astype(o_ref.dtype)


def swiglu_up(z, w1, w3, tm, tn, comb=None):
    m, k = z.shape
    n_e, _, f = w1.shape
    bpe = f // tn
    in_specs = [
        pl.BlockSpec((tm, k), lambda i, j: (i, 0)),
        pl.BlockSpec((1, k, tn), lambda i, j: (j // bpe, 0, j % bpe)),
        pl.BlockSpec((1, k, tn), lambda i, j: (j // bpe, 0, j % bpe)),
    ]
    args = [z, w1, w3]
    if comb is None:
        kern = _swiglu_kernel
    else:
        kern = functools.partial(_swiglu_comb_kernel, blocks_per_expert=bpe)
        in_specs.append(pl.BlockSpec((tm, LANES), lambda i, j: (i, 0)))
        args.append(comb)
    return pl.pallas_call(
        kern,
        grid=(m // tm, n_e * bpe),
        in_specs=in_specs,
        out_specs=pl.BlockSpec((tm, tn), lambda i, j: (i, j)),
        out_shape=jax.ShapeDtypeStruct((m, n_e * f), BF16),
        compiler_params=_cp("parallel", "parallel"),
        name="swiglu_up",
    )(*args)


MOE_TILE = 1024
TOP_K = 2
TOKEN_CHUNK = 128


def _gather_kernel(idx_ref, src_ref, o_ref, sem, *, rows):
    base = pl.program_id(0) * rows

    def issue(r, carry):
        pltpu.make_async_copy(src_ref.at[idx_ref[base + r]], o_ref.at[r], sem).start()
        return carry

    lax.fori_loop(0, rows, issue, 0)
    pltpu.make_async_copy(src_ref.at[pl.ds(0, rows)], o_ref, sem).wait()


def gather_rows(src3, idx, rows_per_step):
    n_src, chunks, _ = src3.shape
    n_out = idx.shape[0]
    return pl.pallas_call(
        functools.partial(_gather_kernel, rows=rows_per_step),
        grid_spec=pltpu.PrefetchScalarGridSpec(
            num_scalar_prefetch=1,
            grid=(n_out // rows_per_step,),
            in_specs=[pl.BlockSpec(memory_space=pl.ANY)],
            out_specs=pl.BlockSpec((rows_per_step, chunks, TOKEN_CHUNK), lambda i, idx_ref: (i, 0, 0)),
            scratch_shapes=[pltpu.SemaphoreType.DMA(())],
        ),
        out_shape=jax.ShapeDtypeStruct((n_out, chunks, TOKEN_CHUNK), src3.dtype),
        compiler_params=_cp("arbitrary"),
        name="gather_rows",
    )(idx, src3)


def _moe_up_kernel(te_ref, nu_ref, z3_ref, w1_ref, w3_ref, ws_ref, o_ref, z_ref, *, f_valid):
    used = pl.program_id(0) < nu_ref[0]
    tn = o_ref.shape[1]
    nt = (((1,), (1,)), ((), ()))

    @pl.when(jnp.logical_and(used, pl.program_id(1) == 0))
    def _():
        z_ref[...] = z3_ref[...].reshape(z_ref.shape)

    @pl.when(used)
    def _():
        z = z_ref[...]
        a = lax.dot_general(z, w1_ref[0], nt, preferred_element_type=F32)
        b = lax.dot_general(z, w3_ref[0], nt, preferred_element_type=F32)
        scale = jnp.concatenate([ws_ref[...]] * (tn // LANES), axis=1)
        col = pl.program_id(1) * tn + lax.broadcasted_iota(jnp.int32, a.shape, 1)
        val = jnp.where(col < f_valid, a * _sigmoid(a) * b * scale, 0.0)
        o_ref[...] = val.astype(o_ref.dtype)

    @pl.when(jnp.logical_not(used))
    def _():
        o_ref[...] = jnp.zeros(o_ref.shape, o_ref.dtype)


def _moe_down_kernel(te_ref, nu_ref, x_ref, w_ref, o_ref):
    used = pl.program_id(0) < nu_ref[0]

    @pl.when(used)
    def _():
        rows, chunks, _ = o_ref.shape
        sub = _pick(chunks * LANES, (SUB_N, LANES))
        for s in range(chunks * LANES // sub):
            part = jnp.dot(x_ref[...], w_ref[0, :, s * sub:(s + 1) * sub], preferred_element_type=F32)
            c0 = s * sub // LANES
            o_ref[:, c0:c0 + sub // LANES, :] = part.astype(o_ref.dtype).reshape(rows, sub // LANES, LANES)

    @pl.when(jnp.logical_not(used))
    def _():
        o_ref[...] = jnp.zeros(o_ref.shape, o_ref.dtype)


def _combine_kernel(h_ref, ya_ref, yb_ref, gate_ref, o_ref):
    y = ya_ref[0].astype(F32) + yb_ref[0].astype(F32)
    o_ref[...] = h_ref[...] + gate_ref[0] * y.reshape(o_ref.shape)


def moe_sparse(z3, comb, n_experts, w1t, w3t, w2, h2, mod3, row_of_tile, gate_chunk, tm):
    m, d = h2.shape
    f_valid = w1t.shape[1]
    f = w2.shape[1]
    n_tiles = (TOP_K * m) // MOE_TILE + n_experts
    n_rows = n_tiles * MOE_TILE
    gather_step = _pick(m, (512, 256, 128))

    cw = comb[:, :n_experts]
    sel = cw > 0.0
    seli = sel.astype(jnp.int32)
    rank = jnp.cumsum(seli, axis=0) - seli
    cnt = jnp.sum(seli, axis=0)
    padded = ((cnt + MOE_TILE - 1) // MOE_TILE) * MOE_TILE
    gend = jnp.cumsum(padded)
    slot = (gend - padded)[None, :] + rank
    top_w, top_e = lax.top_k(cw, TOP_K)
    slot_ab = jnp.take_along_axis(jnp.where(sel, slot, n_rows - 1), top_e, axis=1).astype(jnp.int32)
    pair_id = jnp.arange(m * TOP_K, dtype=jnp.int32)
    row_pair = jnp.full((n_rows,), -1, jnp.int32).at[slot_ab.reshape(-1)].set(pair_id)
    row_valid = row_pair >= 0
    src_row = jnp.where(row_valid, row_pair // TOP_K, 0)
    w_slot = jnp.where(row_valid, top_w.reshape(-1)[jnp.maximum(row_pair, 0)], 0.0)
    w_slot = jnp.broadcast_to(w_slot[:, None], (n_rows, LANES))
    tile_start = jnp.arange(n_tiles, dtype=jnp.int32) * MOE_TILE
    tile_expert = jnp.minimum(jnp.sum((tile_start[:, None] >= gend[None, :]).astype(jnp.int32), axis=1),
                              n_experts - 1).astype(jnp.int32)
    n_used = (gend[-1:] // MOE_TILE).astype(jnp.int32)

    chunks = d // TOKEN_CHUNK
    zs = gather_rows(z3, src_row, gather_step)
    tn_up = _pick(f, (256, 128))
    hid = pl.pallas_call(
        functools.partial(_moe_up_kernel, f_valid=f_valid),
        grid_spec=pltpu.PrefetchScalarGridSpec(
            num_scalar_prefetch=2,
            grid=(n_tiles, f // tn_up),
            in_specs=[
                pl.BlockSpec((MOE_TILE, chunks, TOKEN_CHUNK), lambda i, j, te, nu: (i, 0, 0)),
                pl.BlockSpec((1, tn_up, d), lambda i, j, te, nu: (te[i], j, 0)),
                pl.BlockSpec((1, tn_up, d), lambda i, j, te, nu: (te[i], j, 0)),
                pl.BlockSpec((MOE_TILE, LANES), lambda i, j, te, nu: (i, 0)),
            ],
            out_specs=pl.BlockSpec((MOE_TILE, tn_up), lambda i, j, te, nu: (i, j)),
            scratch_shapes=[pltpu.VMEM((MOE_TILE, d), BF16)],
        ),
        out_shape=jax.ShapeDtypeStruct((n_rows, f), BF16),
        compiler_params=_cp("parallel", "arbitrary"),
        name="moe_up",
    )(tile_expert, n_used, zs, w1t, w3t, w_slot)
    tn_dn = _pick(d, (1024, 512, 256))
    ys = pl.pallas_call(
        _moe_down_kernel,
        grid_spec=pltpu.PrefetchScalarGridSpec(
            num_scalar_prefetch=2,
            grid=(n_tiles, d // tn_dn),
            in_specs=[
                pl.BlockSpec((MOE_TILE, f), lambda i, j, te, nu: (i, 0)),
                pl.BlockSpec((1, f, tn_dn), lambda i, j, te, nu: (te[i], 0, j)),
            ],
            out_specs=pl.BlockSpec((MOE_TILE, tn_dn // TOKEN_CHUNK, TOKEN_CHUNK), lambda i, j, te, nu: (i, j, 0)),
        ),
        out_shape=jax.ShapeDtypeStruct((n_rows, chunks, TOKEN_CHUNK), BF16),
        compiler_params=_cp("parallel", "parallel"),
        name="moe_down",
    )(tile_expert, n_used, hid, w2)
    yab = gather_rows(ys, slot_ab.T.reshape(-1), gather_step).reshape(TOP_K, m, chunks, TOKEN_CHUNK)
    return pl.pallas_call(
        _combine_kernel,
        grid=(m // tm,),
        in_specs=[
            pl.BlockSpec((tm, d), lambda i: (i, 0)),
            pl.BlockSpec((1, tm, chunks, TOKEN_CHUNK), lambda i: (0, i, 0, 0)),
            pl.BlockSpec((1, tm, chunks, TOKEN_CHUNK), lambda i: (1, i, 0, 0)),
            pl.BlockSpec((1, 1, d), lambda i: (row_of_tile(i), 0, gate_chunk)),
        ],
        out_specs=pl.BlockSpec((tm, d), lambda i: (i, 0)),
        out_shape=jax.ShapeDtypeStruct((m, d), F32),
        compiler_params=_cp("parallel"),
        name="moe_combine",
    )(h2, yab, yab, mod3)


def _merge_kernel(xm_ref, wg_ref, *rest, n_branch):
    y_refs = rest[:n_branch]
    wb_ref, o_ref, acc_ref, y_ref = rest[n_branch:]
    b = pl.program_id(2)

    @pl.when(b == 0)
    def _():
        acc_ref[...] = jnp.zeros(acc_ref.shape, F32)

    for bi in range(n_branch):

        @pl.when(b == bi)
        def _(bi=bi):
            y_ref[...] = y_refs[bi][...]

    sub = _pick(o_ref.shape[1], (SUB_N, LANES))
    for s in range(o_ref.shape[1] // sub):
        sl = slice(s * sub, (s + 1) * sub)
        gate = _sigmoid(jnp.dot(xm_ref[...], wg_ref[0, :, sl], preferred_element_type=F32))
        new = acc_ref[:, sl] + gate * jnp.dot(y_ref[...], wb_ref[0, 0, :, sl], preferred_element_type=F32)
        acc_ref[:, sl] = new
        o_ref[:, sl] = new.astype(o_ref.dtype)


def merge_branches(xm, w_gate, ys, w_branch, tm, tn, layer):
    m, d = xm.shape
    _, n_branch, bw, _ = w_branch.shape
    nj = d // tn
    return pl.pallas_call(
        functools.partial(_merge_kernel, n_branch=n_branch),
        grid=(m // tm, nj, n_branch),
        in_specs=[
            pl.BlockSpec((tm, d), lambda i, j, b: (i, 0)),
            pl.BlockSpec((1, d, tn), lambda i, j, b: (layer, 0, b * nj + j)),
        ] + [pl.BlockSpec((tm, bw), lambda i, j, b: (i, 0))] * n_branch + [
            pl.BlockSpec((1, 1, bw, tn), lambda i, j, b: (layer, b, 0, j)),
        ],
        out_specs=pl.BlockSpec((tm, tn), lambda i, j, b: (i, j)),
        out_shape=jax.ShapeDtypeStruct((m, d), BF16),
        scratch_shapes=[pltpu.VMEM((tm, tn), F32), pltpu.VMEM((tm, bw), BF16)],
        compiler_params=_cp("parallel", "parallel", "arbitrary"),
        name="merge_branches",
    )(xm, w_gate, *ys, w_branch)


def _qk_prep_kernel(pq_ref, pk_ref, gq_ref, gk_ref, cos_ref, sin_ref, grp_ref, qo_ref, ko_ref, *, rope, q_scale):
    tm = pq_ref.shape[0]
    lane = lax.broadcasted_iota(jnp.int32, (tm, LANES), 1)
    first_half = (lane & 16) == 0
    for src, g_ref, dst, scale in ((pq_ref, gq_ref, qo_ref, q_scale), (pk_ref, gk_ref, ko_ref, 1.0)):
        for j in range(src.shape[1] // LANES):
            sl = slice(j * LANES, (j + 1) * LANES)
            x = src[:, sl].astype(F32)
            ms = jnp.dot(x * x, grp_ref[...], preferred_element_type=F32, precision=HI)
            y = x * lax.rsqrt(ms + EPS) * g_ref[...]
            if rope:
                partner = jnp.where(first_half, pltpu.roll(y, LANES - 16, 1), pltpu.roll(y, 16, 1))
                y = y * cos_ref[...] + partner * sin_ref[...]
            dst[:, sl] = (y * scale).astype(dst.dtype)


def qk_prep(p, n_seq, bw, gq, gk, cos_t, sin_t, rope, q_scale, tm):
    m = p.shape[0]
    dh = gq.shape[0]
    reps = LANES // dh
    idx = jnp.arange(LANES)
    grp = jnp.where((idx[:, None] // dh) == (idx[None, :] // dh), 1.0 / dh, 0.0).astype(F32)
    tiles_per_seq = n_seq // tm
    return pl.pallas_call(
        functools.partial(_qk_prep_kernel, rope=rope, q_scale=q_scale),
        grid=(m // tm,),
        in_specs=[
            pl.BlockSpec((tm, bw), lambda i: (i, 0)),
            pl.BlockSpec((tm, bw), lambda i: (i, 1)),
            pl.BlockSpec((1, LANES), lambda i: (0, 0)),
            pl.BlockSpec((1, LANES), lambda i: (0, 0)),
            pl.BlockSpec((tm, LANES), lambda i: (i % tiles_per_seq, 0)),
            pl.BlockSpec((tm, LANES), lambda i: (i % tiles_per_seq, 0)),
            pl.BlockSpec((LANES, LANES), lambda i: (0, 0)),
        ],
        out_specs=[pl.BlockSpec((tm, bw), lambda i: (i, 0)), pl.BlockSpec((tm, bw), lambda i: (i, 0))],
        out_shape=[jax.ShapeDtypeStruct((m, bw), BF16), jax.ShapeDtypeStruct((m, bw), BF16)],
        compiler_params=_cp("parallel"),
        name="qk_prep",
    )(p, p, jnp.tile(gq, reps).reshape(1, LANES), jnp.tile(gk, reps).reshape(1, LANES), cos_t, sin_t, grp)


def rope_tables(n_seq, dh):
    nf = dh // 4
    pos = jnp.arange(n_seq)
    row = (pos // GRID_W).astype(F32)
    col = (pos % GRID_W).astype(F32)
    freqs = ROPE_THETA ** (-jnp.arange(nf, dtype=F32) / nf)
    lane = jnp.arange(LANES)
    d = lane % dh
    use_col = (d // (2 * nf)) == 1
    second = ((d // nf) % 2) == 1
    f = freqs[d % nf]
    ang = jnp.where(use_col[None, :], col[:, None], row[:, None]) * f[None, :]
    return jnp.cos(ang), jnp.where(second[None, :], 1.0, -1.0) * jnp.sin(ang)


def _att_kernel(lam_ref, q_ref, kt_ref, v_ref, g_ref, o_ref, m_ref, acc_ref, s_ref, mc_ref, *, n_chunks, ck,
                out_scale):
    tq = q_ref.shape[1]
    half = LANES // 2
    n_tiles = ck // LANES
    q = q_ref[0]
    lane = lax.broadcasted_iota(jnp.int32, (tq, LANES), 1)
    zero = jnp.zeros_like(q)
    qs = (jnp.where(lane < half, q, zero), jnp.where(lane >= half, q, zero))
    m_ref[...] = jnp.full(m_ref.shape, -jnp.inf, F32)
    acc_ref[...] = jnp.zeros(acc_ref.shape, F32)
    ones = jnp.ones((ck, LANES), BF16)

    def scores(c, slot):
        kt = kt_ref[0, 0, c]
        for ci in range(2):
            s = jnp.dot(qs[ci], kt, preferred_element_type=F32)
            s_ref[slot, ci] = s
            mc = s[:, :LANES]
            for j in range(1, n_tiles):
                mc = jnp.maximum(mc, s[:, j * LANES:(j + 1) * LANES])
            mc_ref[slot, ci] = mc

    def softmax_pv(c, slot):
        v = jnp.concatenate([v_ref[0, pl.ds(pl.multiple_of(c * ck, ck), ck), :], ones], axis=1)
        for ci in range(2):
            m_old = m_ref[ci]
            m_new = jnp.maximum(m_old, jnp.max(mc_ref[slot, ci], axis=-1, keepdims=True))
            alpha = jnp.exp2(m_old - m_new)
            p = jnp.concatenate(
                [jnp.exp2(s_ref[slot, ci, :, j * LANES:(j + 1) * LANES] - m_new).astype(BF16)
                 for j in range(n_tiles)], axis=1)
            pv = jnp.dot(p, v, preferred_element_type=F32)
            acc_ref[ci] = jnp.concatenate([alpha, alpha], axis=1) * acc_ref[ci] + pv
            m_ref[ci] = m_new

    scores(0, 0)
    n_pairs = (n_chunks - 1) // 2

    def body(i, carry):
        c = 2 * i
        scores(c + 1, 1)
        softmax_pv(c, 0)
        scores(c + 2, 0)
        softmax_pv(c + 1, 1)
        return carry

    if n_pairs > 0:
        lax.fori_loop(0, n_pairs, body, 0)
    if (n_chunks - 1) % 2 == 1:
        scores(n_chunks - 1, 1)
        softmax_pv(n_chunks - 2, 0)
        softmax_pv(n_chunks - 1, 1)
    else:
        softmax_pv(n_chunks - 1, 0)

    a1 = acc_ref[0]
    a2 = acc_ref[1]
    o = a1[:, :LANES] / a1[:, LANES:] - lam_ref[0, 0] * (a2[:, :LANES] / a2[:, LANES:])
    ms = jnp.mean(o * o, axis=-1, keepdims=True)
    o_ref[0] = (o * lax.rsqrt(ms + EPS) * g_ref[...] * out_scale).astype(o_ref.dtype)


def diff_attention(q, k, v, lam, g_sub, out_scale, tq, ck):
    b, nq, bw = q.shape
    nk = k.shape[1]
    heads = bw // LANES
    n_chunks = nk // ck
    kt = k.reshape(b, n_chunks, ck, heads, LANES).transpose(0, 3, 1, 4, 2)
    return pl.pallas_call(
        functools.partial(_att_kernel, n_chunks=n_chunks, ck=ck, out_scale=out_scale),
        grid=(b, heads, nq // tq),
        in_specs=[
            pl.BlockSpec(memory_space=pltpu.SMEM),
            pl.BlockSpec((1, tq, LANES), lambda bi, h, i: (bi, i, h)),
            pl.BlockSpec((1, 1, n_chunks, LANES, ck), lambda bi, h, i: (bi, h, 0, 0, 0)),
            pl.BlockSpec((1, nk, LANES), lambda bi, h, i: (bi, 0, h)),
            pl.BlockSpec((1, LANES), lambda bi, h, i: (0, 0)),
        ],
        out_specs=pl.BlockSpec((1, tq, LANES), lambda bi, h, i: (bi, i, h)),
        out_shape=jax.ShapeDtypeStruct((b, nq, bw), BF16),
        scratch_shapes=[
            pltpu.VMEM((2, tq, LANES), F32),
            pltpu.VMEM((2, tq, 2 * LANES), F32),
            pltpu.VMEM((2, 2, tq, ck), F32),
            pltpu.VMEM((2, 2, tq, LANES), F32),
        ],
        compiler_params=_cp("parallel", "parallel", "parallel"),
        name="diff_attention",
    )(lam.reshape(1, 1).astype(F32), q, kt, v, g_sub.reshape(1, LANES))


def _local_kernel(pm_ref, pp_ref, pn_ref, cb_ref, cc_ref, ch_ref, ccp_ref, chp_ref, ccn_ref, chn_ref,
                  pw_ref, ps_ref, cw_ref, yp_ref, yc_ref, scr_ref, *, n_seq, tiles_per_seq):
    ts, bw = pm_ref.shape
    it = pl.program_id(0) % tiles_per_seq
    has_prev = jnp.where(it > 0, 1.0, 0.0).astype(F32)
    has_next = jnp.where(it < tiles_per_seq - 1, 1.0, 0.0).astype(F32)
    gw = bw // len(POOL_WINDOWS)
    pos = it * ts + lax.broadcasted_iota(jnp.int32, (ts, 1), 0)

    scr_ref[0:HALO, :] = pp_ref[...].astype(F32) * has_prev
    scr_ref[HALO:HALO + ts, :] = pm_ref[...].astype(F32)
    scr_ref[HALO + ts:2 * HALO + ts, :] = pn_ref[...].astype(F32) * has_next
    for g, w in enumerate(POOL_WINDOWS):
        sl = slice(g * gw, (g + 1) * gw)
        lo, hi = w // 2, w - 1 - w // 2
        tot = None
        for off in range(-lo, hi + 1):
            part = scr_ref[HALO + off:HALO + off + ts, sl]
            tot = part if tot is None else tot + part
        cnt = (jnp.minimum(pos + hi, n_seq - 1) - jnp.maximum(pos - lo, 0) + 1).astype(F32)
        pooled = tot / cnt - scr_ref[HALO:HALO + ts, sl]
        y = jnp.dot(pooled.astype(BF16), pw_ref[g], preferred_element_type=F32)
        yp_ref[:, sl] = (y * ps_ref[:, sl]).astype(yp_ref.dtype)

    scr_ref[0:HALO, :] = ccp_ref[...].astype(F32) * chp_ref[...].astype(F32) * has_prev
    scr_ref[HALO:HALO + ts, :] = cc_ref[...].astype(F32) * ch_ref[...].astype(F32)
    scr_ref[HALO + ts:2 * HALO + ts, :] = ccn_ref[...].astype(F32) * chn_ref[...].astype(F32) * has_next
    conv = None
    for j in range(CONV_K):
        off = j - CONV_K // 2
        term = cw_ref[j:j + 1, :] * scr_ref[HALO + off:HALO + off + ts, :]
        conv = term if conv is None else conv + term
    yc_ref[...] = (cb_ref[...].astype(F32) * conv).astype(yc_ref.dtype)


def local_mixers(p, n_seq, bw, pool_w, pool_scale, conv_w, ts):
    m = p.shape[0]
    tps = n_seq // ts
    r = ts // HALO
    last_halo = m // HALO - 1

    def main(cb):
        return pl.BlockSpec((ts, bw), lambda i: (i, cb))

    def prev(cb):
        return pl.BlockSpec((HALO, bw), lambda i: (jnp.maximum(i * r - 1, 0), cb))

    def nxt(cb):
        return pl.BlockSpec((HALO, bw), lambda i: (jnp.minimum((i + 1) * r, last_halo), cb))

    n_g = len(POOL_WINDOWS)
    return pl.pallas_call(
        functools.partial(_local_kernel, n_seq=n_seq, tiles_per_seq=tps),
        grid=(m // ts,),
        in_specs=[main(3), prev(3), nxt(3), main(5), main(6), main(7), prev(6), prev(7), nxt(6), nxt(7),
                  pl.BlockSpec((n_g, bw // n_g, bw // n_g), lambda i: (0, 0, 0)),
                  pl.BlockSpec((1, bw), lambda i: (0, 0)),
                  pl.BlockSpec((CONV_K, bw), lambda i: (0, 0))],
        out_specs=[pl.BlockSpec((ts, bw), lambda i: (i, 0)), pl.BlockSpec((ts, bw), lambda i: (i, 0))],
        out_shape=[jax.ShapeDtypeStruct((m, bw), BF16), jax.ShapeDtypeStruct((m, bw), BF16)],
        scratch_shapes=[pltpu.VMEM((ts + 2 * HALO, bw), F32)],
        compiler_params=_cp("parallel"),
        name="local_mixers",
    )(p, p, p, p, p, p, p, p, p, p, pool_w.astype(BF16), pool_scale.reshape(1, bw), conv_w)


def _dft_cos_sin(n, scale):
    k = jnp.arange(n, dtype=jnp.int32)
    ang = ((k[:, None] * k[None, :]) % n).astype(F32) * (2.0 * math.pi / n)
    return jnp.cos(ang) * scale, jnp.sin(ang) * scale


def channel_dft_matrix(bw):
    gw = bw // FOURIER_GROUPS
    c, s = _dft_cos_sin(gw, gw ** -0.5)
    eye = jnp.eye(FOURIER_GROUPS, dtype=F32)
    return jnp.concatenate([jnp.kron(eye, c), jnp.kron(eye, s)], axis=1).astype(BF16)


def _fft_stage2_kernel(p_ref, q_ref, cw_ref, sw_ref, f2_ref, o_ref, b_ref):
    n2 = p_ref.shape[2]
    bw = o_ref.shape[2]
    cw = cw_ref[0]
    sw = sw_ref[0]
    for j in range(bw // LANES):
        lo = slice(j * LANES, (j + 1) * LANES)
        hi = slice(bw + j * LANES, bw + (j + 1) * LANES)
        ar = p_ref[0, 0, :, lo].astype(F32) - q_ref[0, 0, :, hi].astype(F32)
        ai = -(p_ref[0, 0, :, hi].astype(F32) + q_ref[0, 0, :, lo].astype(F32))
        b_ref[0:n2, lo] = (ar * cw + ai * sw).astype(BF16)
        b_ref[n2:2 * n2, lo] = (ai * cw - ar * sw).astype(BF16)
    o_ref[0] = jnp.dot(f2_ref[...], b_ref[...], preferred_element_type=F32).astype(o_ref.dtype)


def fourier_seq_two_stage(z, b, n_seq, bw, n1, n2):
    c1, s1 = _dft_cos_sin(n1, n1 ** -0.5)
    f1 = jnp.concatenate([c1, s1], axis=0).astype(BF16)
    c2, s2 = _dft_cos_sin(n2, n2 ** -0.5)
    f2 = jnp.concatenate([c2, s2], axis=1).astype(BF16)
    k1 = jnp.arange(n1, dtype=jnp.int32)
    t2 = jnp.arange(n2, dtype=jnp.int32)
    ang = (k1[:, None] * t2[None, :]).astype(F32) * (2.0 * math.pi / n_seq)
    cw = jnp.broadcast_to(jnp.cos(ang)[:, :, None], (n1, n2, LANES))
    sw = jnp.broadcast_to(jnp.sin(ang)[:, :, None], (n1, n2, LANES))
    zb = z.reshape(b, n1, n2 * 2 * bw)
    tn1 = _pick(n2 * 2 * bw, (8192, 4096, 2048))
    a = pl.pallas_call(
        _mm_batched_kernel,
        grid=(b, (n2 * 2 * bw) // tn1),
        in_specs=[
            pl.BlockSpec((2 * n1, n1), lambda bi, j: (0, 0)),
            pl.BlockSpec((1, n1, tn1), lambda bi, j: (bi, 0, j)),
        ],
        out_specs=pl.BlockSpec((1, 2 * n1, tn1), lambda bi, j: (bi, 0, j)),
        out_shape=jax.ShapeDtypeStruct((b, 2 * n1, n2 * 2 * bw), BF16),
        compiler_params=_cp("parallel", "parallel"),
        name="fft_stage1",
    )(f1, zb)
    a = a.reshape(b, 2 * n1, n2, 2 * bw)
    out = pl.pallas_call(
        _fft_stage2_kernel,
        grid=(b, n1),
        in_specs=[
            pl.BlockSpec((1, 1, n2, 2 * bw), lambda bi, k: (bi, k, 0, 0)),
            pl.BlockSpec((1, 1, n2, 2 * bw), lambda bi, k: (bi, n1 + k, 0, 0)),
            pl.BlockSpec((1, n2, LANES), lambda bi, k: (k, 0, 0)),
            pl.BlockSpec((1, n2, LANES), lambda bi, k: (k, 0, 0)),
            pl.BlockSpec((n2, 2 * n2), lambda bi, k: (0, 0)),
        ],
        out_specs=pl.BlockSpec((1, n2, bw), lambda bi, k: (bi, 0, k)),
        out_shape=jax.ShapeDtypeStruct((b, n2, n1 * bw), BF16),
        scratch_shapes=[pltpu.VMEM((2 * n2, bw), BF16)],
        compiler_params=_cp("parallel", "parallel"),
        name="fft_stage2",
    )(a, a, cw, sw, f2)
    return out.reshape(b * n_seq, bw)


def fourier_seq_dense(z, b, n_seq, bw):
    c, s = _dft_cos_sin(n_seq, n_seq ** -0.5)
    f = jnp.concatenate([c, -s], axis=1).astype(BF16)
    zb = z.reshape(b, n_seq, 2 * bw)
    outs = [matmul(f, jnp.concatenate([zb[i, :, :bw], zb[i, :, bw:]], axis=0)) for i in range(b)]
    return jnp.concatenate(outs, axis=0)


def _fft_factors(n_seq):
    n2 = LANES
    n1 = n_seq // n2
    return n1, n2


def _pad_last(w, mult):
    pad = (-w.shape[-1]) % mult
    return jnp.pad(w, [(0, 0)] * (w.ndim - 1) + [(0, pad)]) if pad else w


def _pad_rows(w, mult):
    pad = (-w.shape[-2]) % mult
    return jnp.pad(w, [(0, 0)] * (w.ndim - 2) + [(0, pad), (0, 0)]) if pad else w


def kernel(x, c, ctx, c_ctx, w_mod, b_mod, norm_mix, norm_ffn, w_in, w_gate, q_norm, k_norm, lambda_q1, lambda_k1,
           lambda_q2, lambda_k2, subln, pool_w, pool_scale, conv_w, w_branch, w_out, ffn_w1, ffn_w3, ffn_w2,
           router, moe_w1, moe_w3, moe_w2):
    b, n_lat, d = x.shape
    n_ctx = ctx.shape[1]
    depth = w_mod.shape[0]
    bw = d // 4
    dh = q_norm.shape[1]
    m_lat = b * n_lat
    m_ctx = b * n_ctx

    cond8 = jnp.zeros((8, d), F32).at[:b].set(c).at[b].set(c_ctx)
    mod3 = modulation_all(cond8, w_mod, b_mod).reshape(depth * 8, 1, N_MOD * d)

    rope_lat = rope_tables(n_lat, dh)
    rope_ctx = rope_tables(n_ctx, dh)
    w_cdft = channel_dft_matrix(bw)
    n1, n2 = _fft_factors(n_lat)

    h = x.reshape(m_lat, d)
    hc = ctx.reshape(m_ctx, d)

    w_in_b = w_in.astype(BF16)
    w_gate_b = w_gate.astype(BF16)
    w_branch_b = w_branch.astype(BF16)
    w_out_b = w_out.astype(BF16)

    for l in range(depth):
        last = l == depth - 1
        lam_init = 0.8 - 0.6 * math.exp(-0.3 * l)
        lam = (jnp.exp(jnp.sum(lambda_q1[l] * lambda_k1[l])) - jnp.exp(jnp.sum(lambda_q2[l] * lambda_k2[l]))
               + lam_init)

        def lat_row(tm, l=l):
            return lambda i: l * 8 + i // (n_lat // tm)

        def ctx_row(tm, l=l):
            return lambda i: l * 8 + b

        def mixer_inputs(h2, n_seq, row_fn, rope, tables):
            tm_norm = _pick(n_seq, (256, 128))
            xm_ = mod_norm(h2, norm_mix[l], mod3, row_fn(tm_norm), 0, 1, tm_norm)
            p_ = matmul(xm_, w_in_b, tm=_pick(n_seq, (1024, 512, 256)), layer=l)
            q_, k_ = qk_prep(p_, n_seq, bw, q_norm[l], k_norm[l], tables[0], tables[1], rope=rope,
                             q_scale=LOG2_E * dh ** -0.5, tm=_pick(n_seq, (512, 256)))
            return xm_, p_, q_, k_, p_[:, 2 * bw:3 * bw]

        def mixer_output(h2, n_seq, row_fn, xm_, p_, att_, four_):
            pool_, conv_ = local_mixers(p_, n_seq, bw, pool_w[l], pool_scale[l], conv_w[l],
                                        ts=_pick(n_seq, (512, 256)))
            ys = (att_, pool_, four_, conv_)
            tm = _pick(n_seq, (1024, 512, 256))
            merged = merge_branches(xm_, w_gate_b, ys, w_branch_b, tm=tm, tn=512, layer=l)
            return matmul_residual(merged, w_out_b, h2, mod3, row_fn(tm), 2, tm=tm, tn=512, tk=d, first=l, count=1)

        xcm, pc, q_c, k_c, v_c = mixer_inputs(hc, n_ctx, ctx_row, False, rope_ctx)
        xm, p, q, k, v = mixer_inputs(h, n_lat, lat_row, True, rope_lat)

        k_all = jnp.concatenate([k_c.reshape(b, n_ctx, bw), k.reshape(b, n_lat, bw)], axis=1)
        v_all = jnp.concatenate([v_c.reshape(b, n_ctx, bw), v.reshape(b, n_lat, bw)], axis=1)
        nk = n_ctx + n_lat
        ck = _pick(nk, (1408, 768, 384, 256, 128))
        att = diff_attention(q.reshape(b, n_lat, bw), k_all, v_all, lam, subln[l], 1.0 - lam_init,
                             tq=_pick(n_lat, (512, 256, 128)), ck=ck).reshape(m_lat, bw)
        z_lat = matmul(p, w_cdft, x_col=4, tm=_pick(n_lat, (1024, 512, 256)))
        four = fourier_seq_two_stage(z_lat, b, n_lat, bw, n1, n2)
        h_new = mixer_output(h, n_lat, lat_row, xm, p, att, four)

        if not last:
            att_c = diff_attention(q_c.reshape(b, n_ctx, bw), k_c.reshape(b, n_ctx, bw), v_c.reshape(b, n_ctx, bw),
                                   lam, subln[l], 1.0 - lam_init, tq=_pick(n_ctx, (256, 128)),
                                   ck=_pick(n_ctx, (256, 128))).reshape(m_ctx, bw)
            z_ctx = matmul(pc, w_cdft, x_col=4, tm=_pick(n_ctx, (256, 128)))
            four_c = fourier_seq_dense(z_ctx, b, n_ctx, bw)
            hc = mixer_output(hc, n_ctx, ctx_row, xcm, pc, att_c, four_c)
        h = h_new

        if l % 2 == 0:
            w1 = _pad_last(ffn_w1[l // 2].astype(BF16), 1024)[None]
            w3 = _pad_last(ffn_w3[l // 2].astype(BF16), 1024)[None]
            w2 = _pad_rows(ffn_w2[l // 2].astype(BF16), 1024)[None]
            rt = None
        else:
            w1 = w3 = None
            w2 = _pad_rows(moe_w2[l // 2].astype(BF16), 256)
            rt = router[l // 2]
        f_pad = w2.shape[1]
        tn_up = _pick(f_pad, (512, 256))
        tk_dn = _pick(f_pad, (2816, 1024, 512, 256))

        def channel_mix(h2, n_seq, row_fn):
            tm_norm = _pick(n_seq, (256, 128))
            zn = mod_norm(h2, norm_ffn[l], mod3, row_fn(tm_norm), 3, 4, tm_norm, router=rt)
            z_, comb = zn if rt is not None else (zn, None)
            tm = _pick(n_seq, (1024, 512, 256))
            if rt is not None and (TOP_K * h2.shape[0]) % MOE_TILE == 0:
                w1t = jnp.swapaxes(moe_w1[l // 2], 1, 2).astype(BF16)
                w3t = jnp.swapaxes(moe_w3[l // 2], 1, 2).astype(BF16)
                return moe_sparse(z_, comb, rt.shape[1], w1t, w3t, w2, h2, mod3, row_fn(tm_norm), 5, tm_norm)
            if rt is not None:
                z_ = z_.reshape(h2.shape)
                w1_, w3_ = (_pad_last(w[l // 2].astype(BF16), 256) for w in (moe_w1, moe_w3))
            else:
                w1_, w3_ = w1, w3
            hid = swiglu_up(z_, w1_, w3_, tm=tm, tn=tn_up, comb=comb)
            return matmul_residual(hid, w2, h2, mod3, row_fn(tm), 5, tm=tm, tn=1024, tk=tk_dn)

        h = channel_mix(h, n_lat, lat_row)
        if not last:
            hc = channel_mix(hc, n_ctx, ctx_row)

    return h.reshape(b, n_lat, d)
```

```python
import functools
import math

import jax
import jax.numpy as jnp
from jax import lax
from jax.experimental import pallas as pl
from jax.experimental.pallas import tpu as pltpu

F32 = jnp.float32
BF16 = jnp.bfloat16

GRID_W = 64
ROPE_THETA = 10000.0
POOL_WINDOWS = (2, 4, 8, 16)
FOURIER_GROUPS = 4
CONV_K = 3
N_MOD = 6
EPS = 1e-6
LANES = 128
HALO = 16
VMEM_LIMIT = 56 * 1024 * 1024
HI = lax.Precision.HIGHEST
LOG2_E = 1.4426950408889634
SUB_N = 256


def _cp(*sem, vmem=VMEM_LIMIT):
    return pltpu.CompilerParams(dimension_semantics=sem, vmem_limit_bytes=vmem)


def _pick(n, prefs):
    for t in prefs:
        if n % t == 0:
            return t
    return n


def _sigmoid(x):
    return 1.0 / (1.0 + jnp.exp(-x))


def _mod_kernel(c_ref, w_ref, b_ref, o_ref):
    x = c_ref[...]
    s = x * _sigmoid(x)
    acc = jnp.dot(s.astype(BF16), w_ref[0].astype(BF16), preferred_element_type=F32)
    o_ref[0] = acc + b_ref[0]


def modulation_all(cond8, w_mod, b_mod):
    depth, d, cols = w_mod.shape
    tn = _pick(cols, (1024, 512, 256, 128))
    return pl.pallas_call(
        _mod_kernel,
        grid=(depth, cols // tn),
        in_specs=[
            pl.BlockSpec((8, d), lambda l, j: (0, 0)),
            pl.BlockSpec((1, d, tn), lambda l, j: (l, 0, j)),
            pl.BlockSpec((1, 1, tn), lambda l, j: (l, 0, j)),
        ],
        out_specs=pl.BlockSpec((1, 8, tn), lambda l, j: (l, 0, j)),
        out_shape=jax.ShapeDtypeStruct((depth, 8, cols), F32),
        compiler_params=_cp("parallel", "parallel"),
        name="modulation",
    )(cond8, w_mod, b_mod.reshape(depth, 1, cols))


def _norm_body(h_ref, g_ref, sh_ref, sc_ref):
    x = h_ref[...]
    ms = jnp.mean(x * x, axis=-1, keepdims=True)
    y = x * lax.rsqrt(ms + EPS) * g_ref[...]
    return y * (1.0 + sc_ref[0]) + sh_ref[0]


def _norm_kernel(h_ref, g_ref, sh_ref, sc_ref, o_ref):
    o_ref[...] = _norm_body(h_ref, g_ref, sh_ref, sc_ref).astype(o_ref.dtype)


def _norm_route_kernel(h_ref, g_ref, sh_ref, sc_ref, r_ref, o_ref, comb_ref, *, n_experts):
    z = _norm_body(h_ref, g_ref, sh_ref, sc_ref)
    o_ref[...] = z.astype(o_ref.dtype).reshape(o_ref.shape)
    logits = jnp.dot(z, r_ref[...], preferred_element_type=F32, precision=HI)
    lane = lax.broadcasted_iota(jnp.int32, logits.shape, 1).astype(F32)
    neg = jnp.float32(-jnp.inf)
    lg = jnp.where(lane < n_experts, logits, neg)
    m1 = jnp.max(lg, axis=-1, keepdims=True)
    i1 = jnp.min(jnp.where(lg == m1, lane, float(LANES)), axis=-1, keepdims=True)
    lg2 = jnp.where(lane == i1, neg, lg)
    m2 = jnp.max(lg2, axis=-1, keepdims=True)
    i2 = jnp.min(jnp.where(lg2 == m2, lane, float(LANES)), axis=-1, keepdims=True)
    e = jnp.exp(m2 - m1)
    w1 = 1.0 / (1.0 + e)
    w2 = e / (1.0 + e)
    comb_ref[...] = jnp.where(lane == i1, w1, 0.0) + jnp.where(lane == i2, w2, 0.0)


def mod_norm(h, g, mod3, row_of_tile, sh_chunk, sc_chunk, tm, router=None):
    m, d = h.shape
    in_specs = [
        pl.BlockSpec((tm, d), lambda i: (i, 0)),
        pl.BlockSpec((1, d), lambda i: (0, 0)),
        pl.BlockSpec((1, 1, d), lambda i: (row_of_tile(i), 0, sh_chunk)),
        pl.BlockSpec((1, 1, d), lambda i: (row_of_tile(i), 0, sc_chunk)),
    ]
    args = [h, g.reshape(1, d), mod3, mod3]
    if router is None:
        return pl.pallas_call(
            _norm_kernel,
            grid=(m // tm,),
            in_specs=in_specs,
            out_specs=pl.BlockSpec((tm, d), lambda i: (i, 0)),
            out_shape=jax.ShapeDtypeStruct((m, d), BF16),
            compiler_params=_cp("parallel"),
            name="mod_norm",
        )(*args)
    n_experts = router.shape[1]
    rpad = jnp.pad(router, ((0, 0), (0, LANES - n_experts)))
    in_specs.append(pl.BlockSpec((d, LANES), lambda i: (0, 0)))
    return pl.pallas_call(
        functools.partial(_norm_route_kernel, n_experts=n_experts),
        grid=(m // tm,),
        in_specs=in_specs,
        out_specs=[pl.BlockSpec((tm, d // LANES, LANES), lambda i: (i, 0, 0)),
                   pl.BlockSpec((tm, LANES), lambda i: (i, 0))],
        out_shape=[jax.ShapeDtypeStruct((m, d // LANES, LANES), BF16), jax.ShapeDtypeStruct((m, LANES), F32)],
        compiler_params=_cp("parallel"),
        name="mod_norm_route",
    )(*args, rpad)


def _mm_kernel(x_ref, w_ref, o_ref):
    o_ref[...] = jnp.dot(x_ref[...], w_ref[0], preferred_element_type=F32).astype(o_ref.dtype)


def _mm_batched_kernel(x_ref, w_ref, o_ref):
    o_ref[0] = jnp.dot(x_ref[...], w_ref[0], preferred_element_type=F32).astype(o_ref.dtype)


def matmul(x, w, out_dtype=BF16, x_col=0, tm=None, tn=None, layer=0):
    m = x.shape[0]
    if w.ndim == 2:
        w = w[None]
    _, k, n = w.shape
    tm = tm or _pick(m, (1024, 512, 256, 128))
    tn = tn or _pick(n, (1024, 512, 256, 128))
    return pl.pallas_call(
        _mm_kernel,
        grid=(m // tm, n // tn),
        in_specs=[
            pl.BlockSpec((tm, k), lambda i, j: (i, x_col)),
            pl.BlockSpec((1, k, tn), lambda i, j: (layer, 0, j)),
        ],
        out_specs=pl.BlockSpec((tm, tn), lambda i, j: (i, j)),
        out_shape=jax.ShapeDtypeStruct((m, n), out_dtype),
        compiler_params=_cp("parallel", "parallel"),
        name="matmul",
    )(x, w)


def _mm_res_kernel(x_ref, w_ref, res_ref, gate_ref, o_ref, acc_ref, *, nk):
    k = pl.program_id(2)
    sub = _pick(o_ref.shape[1], (SUB_N, LANES))
    if nk > 1:

        @pl.when(k == 0)
        def _():
            acc_ref[...] = jnp.zeros(acc_ref.shape, F32)

    for s in range(o_ref.shape[1] // sub):
        sl = slice(s * sub, (s + 1) * sub)
        part = jnp.dot(x_ref[...], w_ref[0, :, sl], preferred_element_type=F32)
        if nk > 1:
            part = acc_ref[:, sl] + part
            acc_ref[:, sl] = part
        o_ref[:, sl] = res_ref[:, sl] + gate_ref[0, :, sl] * part


def matmul_residual(x, w3, res, mod3, row_of_tile, gate_chunk, tm, tn, tk, first=0, count=None):
    m, n = res.shape
    kf = w3.shape[1]
    n_e = w3.shape[0] - first if count is None else count
    kpe = kf // tk
    nk = n_e * kpe
    return pl.pallas_call(
        functools.partial(_mm_res_kernel, nk=nk),
        grid=(m // tm, n // tn, nk),
        in_specs=[
            pl.BlockSpec((tm, tk), lambda i, j, k: (i, k)),
            pl.BlockSpec((1, tk, tn), lambda i, j, k: (first + k // kpe, k % kpe, j)),
            pl.BlockSpec((tm, tn), lambda i, j, k: (i, j)),
            pl.BlockSpec((1, 1, tn), lambda i, j, k: (row_of_tile(i), 0, gate_chunk * (n // tn) + j)),
        ],
        out_specs=pl.BlockSpec((tm, tn), lambda i, j, k: (i, j)),
        out_shape=jax.ShapeDtypeStruct((m, n), F32),
        scratch_shapes=[pltpu.VMEM((tm, tn), F32)],
        compiler_params=_cp("parallel", "parallel", "arbitrary"),
        name="matmul_residual",
    )(x, w3, res, mod3)


def _swiglu_kernel(z_ref, w1_ref, w3_ref, o_ref, *, f_valid, blocks_per_expert):
    tn = o_ref.shape[1]
    sub = _pick(tn, (SUB_N, LANES))
    col0 = (pl.program_id(1) % blocks_per_expert) * tn
    for s in range(tn // sub):
        sl = slice(s * sub, (s + 1) * sub)
        a = jnp.dot(z_ref[...], w1_ref[0, :, sl], preferred_element_type=F32)
        b = jnp.dot(z_ref[...], w3_ref[0, :, sl], preferred_element_type=F32)
        col = col0 + s * sub + lax.broadcasted_iota(jnp.int32, a.shape, 1)
        o_ref[:, sl] = jnp.where(col < f_valid, a * _sigmoid(a) * b, 0.0).astype(o_ref.dtype)


def _swiglu_comb_kernel(z_ref, w1_ref, w3_ref, comb_ref, o_ref, *, blocks_per_expert):
    z = z_ref[...]
    a = jnp.dot(z, w1_ref[0], preferred_element_type=F32)
    b = jnp.dot(z, w3_ref[0], preferred_element_type=F32)
    e = pl.program_id(1) // blocks_per_expert
    comb = comb_ref[...]
    lane = lax.broadcasted_iota(jnp.int32, comb.shape, 1)
    scale = jnp.sum(jnp.where(lane == e, comb, 0.0), axis=-1, keepdims=True)
    o_ref[...] = (a * _sigmoid(a) * b * scale).astype(o_ref.dtype)


def swiglu_up(z, w1, w3, tm, tn, comb=None, f_out=None):
    m, k = z.shape
    n_e, _, f_valid = w1.shape
    f = f_valid if f_out is None else f_out
    bpe = f // tn
    last_blk = (f_valid - 1) // tn

    def w_map(i, j):
        return (j // bpe, 0, jnp.minimum(j % bpe, last_blk))

    in_specs = [
        pl.BlockSpec((tm, k), lambda i, j: (i, 0)),
        pl.BlockSpec((1, k, tn), w_map),
        pl.BlockSpec((1, k, tn), w_map),
    ]
    args = [z, w1, w3]
    if comb is None:
        kern = functools.partial(_swiglu_kernel, f_valid=f_valid, blocks_per_expert=bpe)
    else:
        kern = functools.partial(_swiglu_comb_kernel, blocks_per_expert=bpe)
        in_specs.append(pl.BlockSpec((tm, LANES), lambda i, j: (i, 0)))
        args.append(comb)
    return pl.pallas_call(
        kern,
        grid=(m // tm, n_e * bpe),
        in_specs=in_specs,
        out_specs=pl.BlockSpec((tm, tn), lambda i, j: (i, j)),
        out_shape=jax.ShapeDtypeStruct((m, n_e * f), BF16),
        compiler_params=_cp("parallel", "parallel"),
        name="swiglu_up",
    )(*args)


MOE_TILE = 1024
TOP_K = 2
TOKEN_CHUNK = 128


def _gather_kernel(idx_ref, src_ref, o_ref, sem, *, rows):
    base = pl.program_id(0) * rows

    def issue(r, carry):
        pltpu.make_async_copy(src_ref.at[idx_ref[base + r]], o_ref.at[r], sem).start()
        return carry

    lax.fori_loop(0, rows, issue, 0)
    pltpu.make_async_copy(src_ref.at[pl.ds(0, rows)], o_ref, sem).wait()


def gather_rows(src3, idx, rows_per_step):
    n_src, chunks, _ = src3.shape
    n_out = idx.shape[0]
    return pl.pallas_call(
        functools.partial(_gather_kernel, rows=rows_per_step),
        grid_spec=pltpu.PrefetchScalarGridSpec(
            num_scalar_prefetch=1,
            grid=(n_out // rows_per_step,),
            in_specs=[pl.BlockSpec(memory_space=pl.ANY)],
            out_specs=pl.BlockSpec((rows_per_step, chunks, TOKEN_CHUNK), lambda i, idx_ref: (i, 0, 0)),
            scratch_shapes=[pltpu.SemaphoreType.DMA(())],
        ),
        out_shape=jax.ShapeDtypeStruct((n_out, chunks, TOKEN_CHUNK), src3.dtype),
        compiler_params=_cp("arbitrary"),
        name="gather_rows",
    )(idx, src3)


def _moe_up_kernel(te_ref, nu_ref, z3_ref, w1_ref, w3_ref, ws_ref, o_ref, z_ref, *, f_valid):
    used = pl.program_id(0) < nu_ref[0]
    tn = o_ref.shape[1]
    nt = (((1,), (1,)), ((), ()))

    @pl.when(jnp.logical_and(used, pl.program_id(1) == 0))
    def _():
        z_ref[...] = z3_ref[...].reshape(z_ref.shape)

    @pl.when(used)
    def _():
        z = z_ref[...]
        a = lax.dot_general(z, w1_ref[0], nt, preferred_element_type=F32)
        b = lax.dot_general(z, w3_ref[0], nt, preferred_element_type=F32)
        scale = jnp.concatenate([ws_ref[...]] * (tn // LANES), axis=1)
        col = pl.program_id(1) * tn + lax.broadcasted_iota(jnp.int32, a.shape, 1)
        val = jnp.where(col < f_valid, a * _sigmoid(a) * b * scale, 0.0)
        o_ref[...] = val.astype(o_ref.dtype)

    @pl.when(jnp.logical_not(used))
    def _():
        o_ref[...] = jnp.zeros(o_ref.shape, o_ref.dtype)


def _moe_down_kernel(te_ref, nu_ref, x_ref, w_ref, o_ref, *, f_valid):
    used = pl.program_id(0) < nu_ref[0]

    @pl.when(used)
    def _():
        rows, chunks, _ = o_ref.shape
        sub = _pick(chunks * LANES, (SUB_N, LANES))
        for s in range(chunks * LANES // sub):
            w = w_ref[0, :, s * sub:(s + 1) * sub]
            if f_valid < w.shape[0]:
                row = lax.broadcasted_iota(jnp.int32, w.shape, 0)
                w = jnp.where(row < f_valid, w, jnp.zeros_like(w))
            part = jnp.dot(x_ref[...], w, preferred_element_type=F32)
            c0 = s * sub // LANES
            o_ref[:, c0:c0 + sub // LANES, :] = part.astype(o_ref.dtype).reshape(rows, sub // LANES, LANES)

    @pl.when(jnp.logical_not(used))
    def _():
        o_ref[...] = jnp.zeros(o_ref.shape, o_ref.dtype)


def _combine_kernel(h_ref, ya_ref, yb_ref, gate_ref, o_ref):
    y = ya_ref[0].astype(F32) + yb_ref[0].astype(F32)
    o_ref[...] = h_ref[...] + gate_ref[0] * y.reshape(o_ref.shape)


def moe_sparse(z3, comb, n_experts, w1t, w3t, w2, f, h2, mod3, row_of_tile, gate_chunk, tm):
    m, d = h2.shape
    f_valid = w1t.shape[1]
    n_tiles = (TOP_K * m) // MOE_TILE + n_experts
    n_rows = n_tiles * MOE_TILE
    gather_step = _pick(m, (512, 256, 128))

    cw = comb[:, :n_experts]
    sel = cw > 0.0
    seli = sel.astype(jnp.int32)
    rank = jnp.cumsum(seli, axis=0) - seli
    cnt = jnp.sum(seli, axis=0)
    padded = ((cnt + MOE_TILE - 1) // MOE_TILE) * MOE_TILE
    gend = jnp.cumsum(padded)
    slot = (gend - padded)[None, :] + rank
    top_w, top_e = lax.top_k(cw, TOP_K)
    slot_ab = jnp.take_along_axis(jnp.where(sel, slot, n_rows - 1), top_e, axis=1).astype(jnp.int32)
    pair_id = jnp.arange(m * TOP_K, dtype=jnp.int32)
    row_pair = jnp.full((n_rows,), -1, jnp.int32).at[slot_ab.reshape(-1)].set(pair_id)
    row_valid = row_pair >= 0
    src_row = jnp.where(row_valid, row_pair // TOP_K, jnp.arange(n_rows, dtype=jnp.int32) % m)
    w_slot = jnp.where(row_valid, top_w.reshape(-1)[jnp.maximum(row_pair, 0)], 0.0)
    w_slot = jnp.broadcast_to(w_slot[:, None], (n_rows, LANES))
    tile_start = jnp.arange(n_tiles, dtype=jnp.int32) * MOE_TILE
    tile_expert = jnp.minimum(jnp.sum((tile_start[:, None] >= gend[None, :]).astype(jnp.int32), axis=1),
                              n_experts - 1).astype(jnp.int32)
    n_used = (gend[-1:] // MOE_TILE).astype(jnp.int32)

    chunks = d // TOKEN_CHUNK
    zs = gather_rows(z3, src_row, gather_step)
    tn_up = _pick(f, (256, 128))
    hid = pl.pallas_call(
        functools.partial(_moe_up_kernel, f_valid=f_valid),
        grid_spec=pltpu.PrefetchScalarGridSpec(
            num_scalar_prefetch=2,
            grid=(n_tiles, f // tn_up),
            in_specs=[
                pl.BlockSpec((MOE_TILE, chunks, TOKEN_CHUNK), lambda i, j, te, nu: (i, 0, 0)),
                pl.BlockSpec((1, tn_up, d), lambda i, j, te, nu: (te[i], j, 0)),
                pl.BlockSpec((1, tn_up, d), lambda i, j, te, nu: (te[i], j, 0)),
                pl.BlockSpec((MOE_TILE, LANES), lambda i, j, te, nu: (i, 0)),
            ],
            out_specs=pl.BlockSpec((MOE_TILE, tn_up), lambda i, j, te, nu: (i, j)),
            scratch_shapes=[pltpu.VMEM((MOE_TILE, d), BF16)],
        ),
        out_shape=jax.ShapeDtypeStruct((n_rows, f), BF16),
        compiler_params=_cp("parallel", "arbitrary"),
        name="moe_up",
    )(tile_expert, n_used, zs, w1t, w3t, w_slot)
    tn_dn = _pick(d, (1024, 512, 256))
    ys = pl.pallas_call(
        functools.partial(_moe_down_kernel, f_valid=f_valid),
        grid_spec=pltpu.PrefetchScalarGridSpec(
            num_scalar_prefetch=2,
            grid=(n_tiles, d // tn_dn),
            in_specs=[
                pl.BlockSpec((MOE_TILE, f), lambda i, j, te, nu: (i, 0)),
                pl.BlockSpec((1, f, tn_dn), lambda i, j, te, nu: (te[i], 0, j)),
            ],
            out_specs=pl.BlockSpec((MOE_TILE, tn_dn // TOKEN_CHUNK, TOKEN_CHUNK), lambda i, j, te, nu: (i, j, 0)),
        ),
        out_shape=jax.ShapeDtypeStruct((n_rows, chunks, TOKEN_CHUNK), BF16),
        compiler_params=_cp("parallel", "parallel"),
        name="moe_down",
    )(tile_expert, n_used, hid, w2)
    yab = gather_rows(ys, slot_ab.T.reshape(-1), gather_step).reshape(TOP_K, m, chunks, TOKEN_CHUNK)
    return pl.pallas_call(
        _combine_kernel,
        grid=(m // tm,),
        in_specs=[
            pl.BlockSpec((tm, d), lambda i: (i, 0)),
            pl.BlockSpec((1, tm, chunks, TOKEN_CHUNK), lambda i: (0, i, 0, 0)),
            pl.BlockSpec((1, tm, chunks, TOKEN_CHUNK), lambda i: (1, i, 0, 0)),
            pl.BlockSpec((1, 1, d), lambda i: (row_of_tile(i), 0, gate_chunk)),
        ],
        out_specs=pl.BlockSpec((tm, d), lambda i: (i, 0)),
        out_shape=jax.ShapeDtypeStruct((m, d), F32),
        compiler_params=_cp("parallel"),
        name="moe_combine",
    )(h2, yab, yab, mod3)


def _merge_kernel(xm_ref, wg_ref, *rest, n_branch):
    y_refs = rest[:n_branch]
    wb_ref, o_ref, acc_ref, y_ref = rest[n_branch:]
    b = pl.program_id(2)

    @pl.when(b == 0)
    def _():
        acc_ref[...] = jnp.zeros(acc_ref.shape, F32)

    for bi in range(n_branch):

        @pl.when(b == bi)
        def _(bi=bi):
            y_ref[...] = y_refs[bi][...]

    sub = _pick(o_ref.shape[1], (SUB_N, LANES))
    for s in range(o_ref.shape[1] // sub):
        sl = slice(s * sub, (s + 1) * sub)
        gate = _sigmoid(jnp.dot(xm_ref[...], wg_ref[0, :, sl], preferred_element_type=F32))
        new = acc_ref[:, sl] + gate * jnp.dot(y_ref[...], wb_ref[0, 0, :, sl], preferred_element_type=F32)
        acc_ref[:, sl] = new
        o_ref[:, sl] = new.astype(o_ref.dtype)


def merge_branches(xm, w_gate, ys, w_branch, tm, tn, layer):
    m, d = xm.shape
    _, n_branch, bw, _ = w_branch.shape
    nj = d // tn
    return pl.pallas_call(
        functools.partial(_merge_kernel, n_branch=n_branch),
        grid=(m // tm, nj, n_branch),
        in_specs=[
            pl.BlockSpec((tm, d), lambda i, j, b: (i, 0)),
            pl.BlockSpec((1, d, tn), lambda i, j, b: (layer, 0, b * nj + j)),
        ] + [pl.BlockSpec((tm, bw), lambda i, j, b: (i, 0))] * n_branch + [
            pl.BlockSpec((1, 1, bw, tn), lambda i, j, b: (layer, b, 0, j)),
        ],
        out_specs=pl.BlockSpec((tm, tn), lambda i, j, b: (i, j)),
        out_shape=jax.ShapeDtypeStruct((m, d), BF16),
        scratch_shapes=[pltpu.VMEM((tm, tn), F32), pltpu.VMEM((tm, bw), BF16)],
        compiler_params=_cp("parallel", "parallel", "arbitrary"),
        name="merge_branches",
    )(xm, w_gate, *ys, w_branch)


def _qk_prep_kernel(pq_ref, pk_ref, gq_ref, gk_ref, cos_ref, sin_ref, grp_ref, qo_ref, ko_ref, *, rope, q_scale):
    tm = pq_ref.shape[0]
    lane = lax.broadcasted_iota(jnp.int32, (tm, LANES), 1)
    first_half = (lane & 16) == 0
    for src, g_ref, dst, scale in ((pq_ref, gq_ref, qo_ref, q_scale), (pk_ref, gk_ref, ko_ref, 1.0)):
        for j in range(src.shape[1] // LANES):
            sl = slice(j * LANES, (j + 1) * LANES)
            x = src[:, sl].astype(F32)
            ms = jnp.dot(x * x, grp_ref[...], preferred_element_type=F32, precision=HI)
            y = x * lax.rsqrt(ms + EPS) * g_ref[...]
            if rope:
                partner = jnp.where(first_half, pltpu.roll(y, LANES - 16, 1), pltpu.roll(y, 16, 1))
                y = y * cos_ref[...] + partner * sin_ref[...]
            dst[:, sl] = (y * scale).astype(dst.dtype)


def qk_prep(p, n_seq, bw, gq, gk, cos_t, sin_t, rope, q_scale, tm):
    m = p.shape[0]
    dh = gq.shape[0]
    reps = LANES // dh
    idx = jnp.arange(LANES)
    grp = jnp.where((idx[:, None] // dh) == (idx[None, :] // dh), 1.0 / dh, 0.0).astype(F32)
    tiles_per_seq = n_seq // tm
    return pl.pallas_call(
        functools.partial(_qk_prep_kernel, rope=rope, q_scale=q_scale),
        grid=(m // tm,),
        in_specs=[
            pl.BlockSpec((tm, bw), lambda i: (i, 0)),
            pl.BlockSpec((tm, bw), lambda i: (i, 1)),
            pl.BlockSpec((1, LANES), lambda i: (0, 0)),
            pl.BlockSpec((1, LANES), lambda i: (0, 0)),
            pl.BlockSpec((tm, LANES), lambda i: (i % tiles_per_seq, 0)),
            pl.BlockSpec((tm, LANES), lambda i: (i % tiles_per_seq, 0)),
            pl.BlockSpec((LANES, LANES), lambda i: (0, 0)),
        ],
        out_specs=[pl.BlockSpec((tm, bw), lambda i: (i, 0)), pl.BlockSpec((tm, bw), lambda i: (i, 0))],
        out_shape=[jax.ShapeDtypeStruct((m, bw), BF16), jax.ShapeDtypeStruct((m, bw), BF16)],
        compiler_params=_cp("parallel"),
        name="qk_prep",
    )(p, p, jnp.tile(gq, reps).reshape(1, LANES), jnp.tile(gk, reps).reshape(1, LANES), cos_t, sin_t, grp)


def rope_tables(n_seq, dh):
    nf = dh // 4
    pos = jnp.arange(n_seq)
    row = (pos // GRID_W).astype(F32)
    col = (pos % GRID_W).astype(F32)
    freqs = ROPE_THETA ** (-jnp.arange(nf, dtype=F32) / nf)
    lane = jnp.arange(LANES)
    d = lane % dh
    use_col = (d // (2 * nf)) == 1
    second = ((d // nf) % 2) == 1
    f = freqs[d % nf]
    ang = jnp.where(use_col[None, :], col[:, None], row[:, None]) * f[None, :]
    return jnp.cos(ang), jnp.where(second[None, :], 1.0, -1.0) * jnp.sin(ang)


def _att_kernel(lam_ref, q_ref, kt_ref, v_ref, g_ref, o_ref, m_ref, acc_ref, s_ref, mc_ref, *, n_chunks, ck,
                out_scale):
    tq = q_ref.shape[1]
    half = LANES // 2
    n_tiles = ck // LANES
    q = q_ref[0]
    lane = lax.broadcasted_iota(jnp.int32, (tq, LANES), 1)
    zero = jnp.zeros_like(q)
    qs = (jnp.where(lane < half, q, zero), jnp.where(lane >= half, q, zero))
    m_ref[...] = jnp.full(m_ref.shape, -jnp.inf, F32)
    acc_ref[...] = jnp.zeros(acc_ref.shape, F32)
    ones = jnp.ones((ck, LANES), BF16)

    def scores(c, slot):
        kt = kt_ref[0, 0, c]
        for ci in range(2):
            s = jnp.dot(qs[ci], kt, preferred_element_type=F32)
            s_ref[slot, ci] = s
            mc = s[:, :LANES]
            for j in range(1, n_tiles):
                mc = jnp.maximum(mc, s[:, j * LANES:(j + 1) * LANES])
            mc_ref[slot, ci] = mc

    def softmax_pv(c, slot):
        v = jnp.concatenate([v_ref[0, pl.ds(pl.multiple_of(c * ck, ck), ck), :], ones], axis=1)
        for ci in range(2):
            m_old = m_ref[ci]
            m_new = jnp.maximum(m_old, jnp.max(mc_ref[slot, ci], axis=-1, keepdims=True))
            alpha = jnp.exp2(m_old - m_new)
            p = jnp.concatenate(
                [jnp.exp2(s_ref[slot, ci, :, j * LANES:(j + 1) * LANES] - m_new).astype(BF16)
                 for j in range(n_tiles)], axis=1)
            pv = jnp.dot(p, v, preferred_element_type=F32)
            acc_ref[ci] = jnp.concatenate([alpha, alpha], axis=1) * acc_ref[ci] + pv
            m_ref[ci] = m_new

    scores(0, 0)
    n_pairs = (n_chunks - 1) // 2

    def body(i, carry):
        c = 2 * i
        scores(c + 1, 1)
        softmax_pv(c, 0)
        scores(c + 2, 0)
        softmax_pv(c + 1, 1)
        return carry

    if n_pairs > 0:
        lax.fori_loop(0, n_pairs, body, 0)
    if (n_chunks - 1) % 2 == 1:
        scores(n_chunks - 1, 1)
        softmax_pv(n_chunks - 2, 0)
        softmax_pv(n_chunks - 1, 1)
    else:
        softmax_pv(n_chunks - 1, 0)

    a1 = acc_ref[0]
    a2 = acc_ref[1]
    o = a1[:, :LANES] / a1[:, LANES:] - lam_ref[0, 0] * (a2[:, :LANES] / a2[:, LANES:])
    ms = jnp.mean(o * o, axis=-1, keepdims=True)
    o_ref[0] = (o * lax.rsqrt(ms + EPS) * g_ref[...] * out_scale).astype(o_ref.dtype)


def diff_attention(q, k, v, lam, g_sub, out_scale, tq, ck):
    b, nq, bw = q.shape
    nk = k.shape[1]
    heads = bw // LANES
    n_chunks = nk // ck
    kt = k.reshape(b, n_chunks, ck, heads, LANES).transpose(0, 3, 1, 4, 2)
    return pl.pallas_call(
        functools.partial(_att_kernel, n_chunks=n_chunks, ck=ck, out_scale=out_scale),
        grid=(b, heads, nq // tq),
        in_specs=[
            pl.BlockSpec(memory_space=pltpu.SMEM),
            pl.BlockSpec((1, tq, LANES), lambda bi, h, i: (bi, i, h)),
            pl.BlockSpec((1, 1, n_chunks, LANES, ck), lambda bi, h, i: (bi, h, 0, 0, 0)),
            pl.BlockSpec((1, nk, LANES), lambda bi, h, i: (bi, 0, h)),
            pl.BlockSpec((1, LANES), lambda bi, h, i: (0, 0)),
        ],
        out_specs=pl.BlockSpec((1, tq, LANES), lambda bi, h, i: (bi, i, h)),
        out_shape=jax.ShapeDtypeStruct((b, nq, bw), BF16),
        scratch_shapes=[
            pltpu.VMEM((2, tq, LANES), F32),
            pltpu.VMEM((2, tq, 2 * LANES), F32),
            pltpu.VMEM((2, 2, tq, ck), F32),
            pltpu.VMEM((2, 2, tq, LANES), F32),
        ],
        compiler_params=_cp("parallel", "parallel", "parallel"),
        name="diff_attention",
    )(lam.reshape(1, 1).astype(F32), q, kt, v, g_sub.reshape(1, LANES))


def _local_kernel(pm_ref, pp_ref, pn_ref, cb_ref, cc_ref, ch_ref, ccp_ref, chp_ref, ccn_ref, chn_ref,
                  pw_ref, ps_ref, cw_ref, yp_ref, yc_ref, scr_ref, *, n_seq, tiles_per_seq):
    ts, bw = pm_ref.shape
    it = pl.program_id(0) % tiles_per_seq
    has_prev = jnp.where(it > 0, 1.0, 0.0).astype(F32)
    has_next = jnp.where(it < tiles_per_seq - 1, 1.0, 0.0).astype(F32)
    gw = bw // len(POOL_WINDOWS)
    pos = it * ts + lax.broadcasted_iota(jnp.int32, (ts, 1), 0)

    scr_ref[0:HALO, :] = pp_ref[...].astype(F32) * has_prev
    scr_ref[HALO:HALO + ts, :] = pm_ref[...].astype(F32)
    scr_ref[HALO + ts:2 * HALO + ts, :] = pn_ref[...].astype(F32) * has_next
    for g, w in enumerate(POOL_WINDOWS):
        sl = slice(g * gw, (g + 1) * gw)
        lo, hi = w // 2, w - 1 - w // 2
        tot = None
        for off in range(-lo, hi + 1):
            part = scr_ref[HALO + off:HALO + off + ts, sl]
            tot = part if tot is None else tot + part
        cnt = (jnp.minimum(pos + hi, n_seq - 1) - jnp.maximum(pos - lo, 0) + 1).astype(F32)
        pooled = tot / cnt - scr_ref[HALO:HALO + ts, sl]
        y = jnp.dot(pooled.astype(BF16), pw_ref[g], preferred_element_type=F32)
        yp_ref[:, sl] = (y * ps_ref[:, sl]).astype(yp_ref.dtype)

    scr_ref[0:HALO, :] = ccp_ref[...].astype(F32) * chp_ref[...].astype(F32) * has_prev
    scr_ref[HALO:HALO + ts, :] = cc_ref[...].astype(F32) * ch_ref[...].astype(F32)
    scr_ref[HALO + ts:2 * HALO + ts, :] = ccn_ref[...].astype(F32) * chn_ref[...].astype(F32) * has_next
    conv = None
    for j in range(CONV_K):
        off = j - CONV_K // 2
        term = cw_ref[j:j + 1, :] * scr_ref[HALO + off:HALO + off + ts, :]
        conv = term if conv is None else conv + term
    yc_ref[...] = (cb_ref[...].astype(F32) * conv).astype(yc_ref.dtype)


def local_mixers(p, n_seq, bw, pool_w, pool_scale, conv_w, ts):
    m = p.shape[0]
    tps = n_seq // ts
    r = ts // HALO
    last_halo = m // HALO - 1

    def main(cb):
        return pl.BlockSpec((ts, bw), lambda i: (i, cb))

    def prev(cb):
        return pl.BlockSpec((HALO, bw), lambda i: (jnp.maximum(i * r - 1, 0), cb))

    def nxt(cb):
        return pl.BlockSpec((HALO, bw), lambda i: (jnp.minimum((i + 1) * r, last_halo), cb))

    n_g = len(POOL_WINDOWS)
    return pl.pallas_call(
        functools.partial(_local_kernel, n_seq=n_seq, tiles_per_seq=tps),
        grid=(m // ts,),
        in_specs=[main(3), prev(3), nxt(3), main(5), main(6), main(7), prev(6), prev(7), nxt(6), nxt(7),
                  pl.BlockSpec((n_g, bw // n_g, bw // n_g), lambda i: (0, 0, 0)),
                  pl.BlockSpec((1, bw), lambda i: (0, 0)),
                  pl.BlockSpec((CONV_K, bw), lambda i: (0, 0))],
        out_specs=[pl.BlockSpec((ts, bw), lambda i: (i, 0)), pl.BlockSpec((ts, bw), lambda i: (i, 0))],
        out_shape=[jax.ShapeDtypeStruct((m, bw), BF16), jax.ShapeDtypeStruct((m, bw), BF16)],
        scratch_shapes=[pltpu.VMEM((ts + 2 * HALO, bw), F32)],
        compiler_params=_cp("parallel"),
        name="local_mixers",
    )(p, p, p, p, p, p, p, p, p, p, pool_w.astype(BF16), pool_scale.reshape(1, bw), conv_w)


def _dft_cos_sin(n, scale):
    k = jnp.arange(n, dtype=jnp.int32)
    ang = ((k[:, None] * k[None, :]) % n).astype(F32) * (2.0 * math.pi / n)
    return jnp.cos(ang) * scale, jnp.sin(ang) * scale


def channel_dft_matrix(bw):
    gw = bw // FOURIER_GROUPS
    c, s = _dft_cos_sin(gw, gw ** -0.5)
    eye = jnp.eye(FOURIER_GROUPS, dtype=F32)
    return jnp.concatenate([jnp.kron(eye, c), jnp.kron(eye, s)], axis=1).astype(BF16)


def _fft_stage2_kernel(p_ref, q_ref, cw_ref, sw_ref, f2_ref, o_ref, b_ref):
    n2 = p_ref.shape[2]
    bw = o_ref.shape[2]
    cw = cw_ref[0]
    sw = sw_ref[0]
    for j in range(bw // LANES):
        lo = slice(j * LANES, (j + 1) * LANES)
        hi = slice(bw + j * LANES, bw + (j + 1) * LANES)
        ar = p_ref[0, 0, :, lo].astype(F32) - q_ref[0, 0, :, hi].astype(F32)
        ai = -(p_ref[0, 0, :, hi].astype(F32) + q_ref[0, 0, :, lo].astype(F32))
        b_ref[0:n2, lo] = (ar * cw + ai * sw).astype(BF16)
        b_ref[n2:2 * n2, lo] = (ai * cw - ar * sw).astype(BF16)
    o_ref[0] = jnp.dot(f2_ref[...], b_ref[...], preferred_element_type=F32).astype(o_ref.dtype)


def fourier_seq_two_stage(z, b, n_seq, bw, n1, n2):
    c1, s1 = _dft_cos_sin(n1, n1 ** -0.5)
    f1 = jnp.concatenate([c1, s1], axis=0).astype(BF16)
    c2, s2 = _dft_cos_sin(n2, n2 ** -0.5)
    f2 = jnp.concatenate([c2, s2], axis=1).astype(BF16)
    k1 = jnp.arange(n1, dtype=jnp.int32)
    t2 = jnp.arange(n2, dtype=jnp.int32)
    ang = (k1[:, None] * t2[None, :]).astype(F32) * (2.0 * math.pi / n_seq)
    cw = jnp.broadcast_to(jnp.cos(ang)[:, :, None], (n1, n2, LANES))
    sw = jnp.broadcast_to(jnp.sin(ang)[:, :, None], (n1, n2, LANES))
    zb = z.reshape(b, n1, n2 * 2 * bw)
    tn1 = _pick(n2 * 2 * bw, (8192, 4096, 2048))
    a = pl.pallas_call(
        _mm_batched_kernel,
        grid=(b, (n2 * 2 * bw) // tn1),
        in_specs=[
            pl.BlockSpec((2 * n1, n1), lambda bi, j: (0, 0)),
            pl.BlockSpec((1, n1, tn1), lambda bi, j: (bi, 0, j)),
        ],
        out_specs=pl.BlockSpec((1, 2 * n1, tn1), lambda bi, j: (bi, 0, j)),
        out_shape=jax.ShapeDtypeStruct((b, 2 * n1, n2 * 2 * bw), BF16),
        compiler_params=_cp("parallel", "parallel"),
        name="fft_stage1",
    )(f1, zb)
    a = a.reshape(b, 2 * n1, n2, 2 * bw)
    out = pl.pallas_call(
        _fft_stage2_kernel,
        grid=(b, n1),
        in_specs=[
            pl.BlockSpec((1, 1, n2, 2 * bw), lambda bi, k: (bi, k, 0, 0)),
            pl.BlockSpec((1, 1, n2, 2 * bw), lambda bi, k: (bi, n1 + k, 0, 0)),
            pl.BlockSpec((1, n2, LANES), lambda bi, k: (k, 0, 0)),
            pl.BlockSpec((1, n2, LANES), lambda bi, k: (k, 0, 0)),
            pl.BlockSpec((n2, 2 * n2), lambda bi, k: (0, 0)),
        ],
        out_specs=pl.BlockSpec((1, n2, bw), lambda bi, k: (bi, 0, k)),
        out_shape=jax.ShapeDtypeStruct((b, n2, n1 * bw), BF16),
        scratch_shapes=[pltpu.VMEM((2 * n2, bw), BF16)],
        compiler_params=_cp("parallel", "parallel"),
        name="fft_stage2",
    )(a, a, cw, sw, f2)
    return out.reshape(b * n_seq, bw)


def fourier_seq_dense(z, b, n_seq, bw):
    c, s = _dft_cos_sin(n_seq, n_seq ** -0.5)
    f = jnp.concatenate([c, -s], axis=1).astype(BF16)
    zb = z.reshape(b, n_seq, 2 * bw)
    outs = [matmul(f, jnp.concatenate([zb[i, :, :bw], zb[i, :, bw:]], axis=0)) for i in range(b)]
    return jnp.concatenate(outs, axis=0)


def _fft_factors(n_seq):
    n2 = LANES
    n1 = n_seq // n2
    return n1, n2


def _pad_last(w, mult):
    pad = (-w.shape[-1]) % mult
    return jnp.pad(w, [(0, 0)] * (w.ndim - 1) + [(0, pad)]) if pad else w


def _pad_rows(w, mult):
    pad = (-w.shape[-2]) % mult
    return jnp.pad(w, [(0, 0)] * (w.ndim - 2) + [(0, pad), (0, 0)]) if pad else w


def kernel(x, c, ctx, c_ctx, w_mod, b_mod, norm_mix, norm_ffn, w_in, w_gate, q_norm, k_norm, lambda_q1, lambda_k1,
           lambda_q2, lambda_k2, subln, pool_w, pool_scale, conv_w, w_branch, w_out, ffn_w1, ffn_w3, ffn_w2,
           router, moe_w1, moe_w3, moe_w2):
    b, n_lat, d = x.shape
    n_ctx = ctx.shape[1]
    depth = w_mod.shape[0]
    bw = d // 4
    dh = q_norm.shape[1]
    m_lat = b * n_lat
    m_ctx = b * n_ctx

    cond8 = jnp.zeros((8, d), F32).at[:b].set(c).at[b].set(c_ctx)
    mod3 = modulation_all(cond8, w_mod, b_mod).reshape(depth * 8, 1, N_MOD * d)

    rope_lat = rope_tables(n_lat, dh)
    rope_ctx = rope_tables(n_ctx, dh)
    w_cdft = channel_dft_matrix(bw)
    n1, n2 = _fft_factors(n_lat)

    h = x.reshape(m_lat, d)
    hc = ctx.reshape(m_ctx, d)

    w_in_b = w_in.astype(BF16)
    w_gate_b = w_gate.astype(BF16)
    w_branch_b = w_branch.astype(BF16)
    w_out_b = w_out.astype(BF16)

    for l in range(depth):
        last = l == depth - 1
        lam_init = 0.8 - 0.6 * math.exp(-0.3 * l)
        lam = (jnp.exp(jnp.sum(lambda_q1[l] * lambda_k1[l])) - jnp.exp(jnp.sum(lambda_q2[l] * lambda_k2[l]))
               + lam_init)

        def lat_row(tm, l=l):
            return lambda i: l * 8 + i // (n_lat // tm)

        def ctx_row(tm, l=l):
            return lambda i: l * 8 + b

        def mm_rows(h2, n_seq, row_fn):
            return _pick(h2.shape[0] if row_fn is ctx_row else n_seq, (1024, 512, 256))

        def mixer_inputs(h2, n_seq, row_fn, rope, tables):
            tm_norm = _pick(n_seq, (256, 128))
            xm_ = mod_norm(h2, norm_mix[l], mod3, row_fn(tm_norm), 0, 1, tm_norm)
            p_ = matmul(xm_, w_in_b, tm=mm_rows(h2, n_seq, row_fn), layer=l)
            q_, k_ = qk_prep(p_, n_seq, bw, q_norm[l], k_norm[l], tables[0], tables[1], rope=rope,
                             q_scale=LOG2_E * dh ** -0.5, tm=_pick(n_seq, (512, 256)))
            return xm_, p_, q_, k_, p_[:, 2 * bw:3 * bw]

        def mixer_output(h2, n_seq, row_fn, xm_, p_, att_, four_):
            pool_, conv_ = local_mixers(p_, n_seq, bw, pool_w[l], pool_scale[l], conv_w[l],
                                        ts=_pick(n_seq, (512, 256)))
            ys = (att_, pool_, four_, conv_)
            tm = mm_rows(h2, n_seq, row_fn)
            merged = merge_branches(xm_, w_gate_b, ys, w_branch_b, tm=tm, tn=512, layer=l)
            return matmul_residual(merged, w_out_b, h2, mod3, row_fn(tm), 2, tm=tm, tn=512, tk=d, first=l, count=1)

        xcm, pc, q_c, k_c, v_c = mixer_inputs(hc, n_ctx, ctx_row, False, rope_ctx)
        xm, p, q, k, v = mixer_inputs(h, n_lat, lat_row, True, rope_lat)

        k_all = jnp.concatenate([k_c.reshape(b, n_ctx, bw), k.reshape(b, n_lat, bw)], axis=1)
        v_all = jnp.concatenate([v_c.reshape(b, n_ctx, bw), v.reshape(b, n_lat, bw)], axis=1)
        nk = n_ctx + n_lat
        ck = _pick(nk, (1408, 768, 384, 256, 128))
        att = diff_attention(q.reshape(b, n_lat, bw), k_all, v_all, lam, subln[l], 1.0 - lam_init,
                             tq=_pick(n_lat, (512, 256, 128)), ck=ck).reshape(m_lat, bw)
        z_lat = matmul(p, w_cdft, x_col=4, tm=_pick(n_lat, (1024, 512, 256)))
        four = fourier_seq_two_stage(z_lat, b, n_lat, bw, n1, n2)
        h_new = mixer_output(h, n_lat, lat_row, xm, p, att, four)

        if not last:
            att_c = diff_attention(q_c.reshape(b, n_ctx, bw), k_c.reshape(b, n_ctx, bw), v_c.reshape(b, n_ctx, bw),
                                   lam, subln[l], 1.0 - lam_init, tq=_pick(n_ctx, (256, 128)),
                                   ck=_pick(n_ctx, (256, 128))).reshape(m_ctx, bw)
            z_ctx = matmul(pc, w_cdft, x_col=4, tm=_pick(n_ctx, (256, 128)))
            four_c = fourier_seq_dense(z_ctx, b, n_ctx, bw)
            hc = mixer_output(hc, n_ctx, ctx_row, xcm, pc, att_c, four_c)
        h = h_new

        if l % 2 == 0:
            w1 = ffn_w1[l // 2].astype(BF16)[None]
            w3 = ffn_w3[l // 2].astype(BF16)[None]
            w2 = _pad_rows(ffn_w2[l // 2].astype(BF16), 1024)[None]
            rt = None
            f_pad = w2.shape[1]
        else:
            w1 = w3 = w2 = None
            rt = router[l // 2]
            f_pad = -(-moe_w2.shape[2] // 256) * 256
        tn_up = _pick(f_pad, (512, 256))
        tk_dn = _pick(f_pad, (2816, 1024, 512, 256))

        def channel_mix(h2, n_seq, row_fn):
            tm_norm = _pick(n_seq, (256, 128))
            zn = mod_norm(h2, norm_ffn[l], mod3, row_fn(tm_norm), 3, 4, tm_norm, router=rt)
            z_, comb = zn if rt is not None else (zn, None)
            tm = mm_rows(h2, n_seq, row_fn)
            if rt is not None and (TOP_K * h2.shape[0]) % MOE_TILE == 0:
                w1t = jnp.swapaxes(moe_w1[l // 2], 1, 2).astype(BF16)
                w3t = jnp.swapaxes(moe_w3[l // 2], 1, 2).astype(BF16)
                return moe_sparse(z_, comb, rt.shape[1], w1t, w3t, moe_w2[l // 2].astype(BF16), f_pad, h2, mod3,
                                  row_fn(tm_norm), 5, tm_norm)
            if rt is not None:
                z_ = z_.reshape(h2.shape)
                w1_, w3_ = (_pad_last(w[l // 2].astype(BF16), 256) for w in (moe_w1, moe_w3))
                w2_ = _pad_rows(moe_w2[l // 2].astype(BF16), 256)
            else:
                w1_, w3_, w2_ = w1, w3, w2
            hid = swiglu_up(z_, w1_, w3_, tm=tm, tn=tn_up, comb=comb, f_out=f_pad)
            return matmul_residual(hid, w2_, h2, mod3, row_fn(tm), 5, tm=tm, tn=1024, tk=tk_dn)

        h = channel_mix(h, n_lat, lat_row)
        if not last:
            hc = channel_mix(hc, n_ctx, ctx_row)

    return h.reshape(b, n_lat, d)
```

```python
import functools
import math

import jax
import jax.numpy as jnp
from jax import lax
from jax.experimental import pallas as pl
from jax.experimental.pallas import tpu as pltpu

F32 = jnp.float32
BF16 = jnp.bfloat16

GRID_W = 64
ROPE_THETA = 10000.0
POOL_WINDOWS = (2, 4, 8, 16)
FOURIER_GROUPS = 4
CONV_K = 3
N_MOD = 6
EPS = 1e-6
LANES = 128
HALO = 16
VMEM_LIMIT = 56 * 1024 * 1024
HI = lax.Precision.HIGHEST
LOG2_E = 1.4426950408889634
SUB_N = 256


def _cp(*sem, vmem=VMEM_LIMIT):
    return pltpu.CompilerParams(dimension_semantics=sem, vmem_limit_bytes=vmem)


def _pick(n, prefs):
    for t in prefs:
        if n % t == 0:
            return t
    return n


def _sigmoid(x):
    return 1.0 / (1.0 + jnp.exp(-x))


def _mod_kernel(c_ref, w_ref, b_ref, o_ref):
    x = c_ref[...]
    s = x * _sigmoid(x)
    acc = jnp.dot(s.astype(BF16), w_ref[0].astype(BF16), preferred_element_type=F32)
    o_ref[0] = acc + b_ref[0]


def modulation_all(cond8, w_mod, b_mod):
    depth, d, cols = w_mod.shape
    tn = _pick(cols, (1024, 512, 256, 128))
    return pl.pallas_call(
        _mod_kernel,
        grid=(depth, cols // tn),
        in_specs=[
            pl.BlockSpec((8, d), lambda l, j: (0, 0)),
            pl.BlockSpec((1, d, tn), lambda l, j: (l, 0, j)),
            pl.BlockSpec((1, 1, tn), lambda l, j: (l, 0, j)),
        ],
        out_specs=pl.BlockSpec((1, 8, tn), lambda l, j: (l, 0, j)),
        out_shape=jax.ShapeDtypeStruct((depth, 8, cols), F32),
        compiler_params=_cp("parallel", "parallel"),
        name="modulation",
    )(cond8, w_mod, b_mod.reshape(depth, 1, cols))


def _norm_body(h_ref, g_ref, sh_ref, sc_ref):
    x = h_ref[...]
    ms = jnp.mean(x * x, axis=-1, keepdims=True)
    y = x * lax.rsqrt(ms + EPS) * g_ref[...]
    return y * (1.0 + sc_ref[0]) + sh_ref[0]


def _norm_kernel(h_ref, g_ref, sh_ref, sc_ref, o_ref):
    o_ref[...] = _norm_body(h_ref, g_ref, sh_ref, sc_ref).astype(o_ref.dtype)


def _norm_route_kernel(h_ref, g_ref, sh_ref, sc_ref, r_ref, o_ref, comb_ref, *, n_experts):
    z = _norm_body(h_ref, g_ref, sh_ref, sc_ref)
    o_ref[...] = z.astype(o_ref.dtype).reshape(o_ref.shape)
    logits = jnp.dot(z, r_ref[...], preferred_element_type=F32, precision=HI)
    lane = lax.broadcasted_iota(jnp.int32, logits.shape, 1).astype(F32)
    neg = jnp.float32(-jnp.inf)
    lg = jnp.where(lane < n_experts, logits, neg)
    m1 = jnp.max(lg, axis=-1, keepdims=True)
    i1 = jnp.min(jnp.where(lg == m1, lane, float(LANES)), axis=-1, keepdims=True)
    lg2 = jnp.where(lane == i1, neg, lg)
    m2 = jnp.max(lg2, axis=-1, keepdims=True)
    i2 = jnp.min(jnp.where(lg2 == m2, lane, float(LANES)), axis=-1, keepdims=True)
    e = jnp.exp(m2 - m1)
    w1 = 1.0 / (1.0 + e)
    w2 = e / (1.0 + e)
    comb_ref[...] = jnp.where(lane == i1, w1, 0.0) + jnp.where(lane == i2, w2, 0.0)


def mod_norm(h, g, mod3, row_of_tile, sh_chunk, sc_chunk, tm, router=None):
    m, d = h.shape
    in_specs = [
        pl.BlockSpec((tm, d), lambda i: (i, 0)),
        pl.BlockSpec((1, d), lambda i: (0, 0)),
        pl.BlockSpec((1, 1, d), lambda i: (row_of_tile(i), 0, sh_chunk)),
        pl.BlockSpec((1, 1, d), lambda i: (row_of_tile(i), 0, sc_chunk)),
    ]
    args = [h, g.reshape(1, d), mod3, mod3]
    if router is None:
        return pl.pallas_call(
            _norm_kernel,
            grid=(m // tm,),
            in_specs=in_specs,
            out_specs=pl.BlockSpec((tm, d), lambda i: (i, 0)),
            out_shape=jax.ShapeDtypeStruct((m, d), BF16),
            compiler_params=_cp("parallel"),
            name="mod_norm",
        )(*args)
    n_experts = router.shape[1]
    rpad = jnp.pad(router, ((0, 0), (0, LANES - n_experts)))
    in_specs.append(pl.BlockSpec((d, LANES), lambda i: (0, 0)))
    return pl.pallas_call(
        functools.partial(_norm_route_kernel, n_experts=n_experts),
        grid=(m // tm,),
        in_specs=in_specs,
        out_specs=[pl.BlockSpec((tm, d // LANES, LANES), lambda i: (i, 0, 0)),
                   pl.BlockSpec((tm, LANES), lambda i: (i, 0))],
        out_shape=[jax.ShapeDtypeStruct((m, d // LANES, LANES), BF16), jax.ShapeDtypeStruct((m, LANES), F32)],
        compiler_params=_cp("parallel"),
        name="mod_norm_route",
    )(*args, rpad)


def _mm_kernel(x_ref, w_ref, o_ref):
    o_ref[...] = jnp.dot(x_ref[...], w_ref[0], preferred_element_type=F32).astype(o_ref.dtype)


def _mm_batched_kernel(x_ref, w_ref, o_ref):
    o_ref[0] = jnp.dot(x_ref[...], w_ref[0], preferred_element_type=F32).astype(o_ref.dtype)


def matmul(x, w, out_dtype=BF16, x_col=0, tm=None, tn=None, layer=0):
    m = x.shape[0]
    if w.ndim == 2:
        w = w[None]
    _, k, n = w.shape
    tm = tm or _pick(m, (1024, 512, 256, 128))
    tn = tn or _pick(n, (1024, 512, 256, 128))
    return pl.pallas_call(
        _mm_kernel,
        grid=(m // tm, n // tn),
        in_specs=[
            pl.BlockSpec((tm, k), lambda i, j: (i, x_col)),
            pl.BlockSpec((1, k, tn), lambda i, j: (layer, 0, j)),
        ],
        out_specs=pl.BlockSpec((tm, tn), lambda i, j: (i, j)),
        out_shape=jax.ShapeDtypeStruct((m, n), out_dtype),
        compiler_params=_cp("parallel", "parallel"),
        name="matmul",
    )(x, w)


def _mm_res_kernel(x_ref, w_ref, res_ref, gate_ref, o_ref, acc_ref, *, nk):
    k = pl.program_id(2)
    sub = _pick(o_ref.shape[1], (SUB_N, LANES))
    if nk > 1:

        @pl.when(k == 0)
        def _():
            acc_ref[...] = jnp.zeros(acc_ref.shape, F32)

    for s in range(o_ref.shape[1] // sub):
        sl = slice(s * sub, (s + 1) * sub)
        part = jnp.dot(x_ref[...], w_ref[0, :, sl], preferred_element_type=F32)
        if nk > 1:
            part = acc_ref[:, sl] + part
            acc_ref[:, sl] = part
        o_ref[:, sl] = res_ref[:, sl] + gate_ref[0, :, sl] * part


def matmul_residual(x, w3, res, mod3, row_of_tile, gate_chunk, tm, tn, tk, first=0, count=None):
    m, n = res.shape
    kf = w3.shape[1]
    n_e = w3.shape[0] - first if count is None else count
    kpe = kf // tk
    nk = n_e * kpe
    return pl.pallas_call(
        functools.partial(_mm_res_kernel, nk=nk),
        grid=(m // tm, n // tn, nk),
        in_specs=[
            pl.BlockSpec((tm, tk), lambda i, j, k: (i, k)),
            pl.BlockSpec((1, tk, tn), lambda i, j, k: (first + k // kpe, k % kpe, j)),
            pl.BlockSpec((tm, tn), lambda i, j, k: (i, j)),
            pl.BlockSpec((1, 1, tn), lambda i, j, k: (row_of_tile(i), 0, gate_chunk * (n // tn) + j)),
        ],
        out_specs=pl.BlockSpec((tm, tn), lambda i, j, k: (i, j)),
        out_shape=jax.ShapeDtypeStruct((m, n), F32),
        scratch_shapes=[pltpu.VMEM((tm, tn), F32)],
        compiler_params=_cp("parallel", "parallel", "arbitrary"),
        name="matmul_residual",
    )(x, w3, res, mod3)


def _swiglu_kernel(z_ref, w1_ref, w3_ref, o_ref, *, f_valid, blocks_per_expert):
    tn = o_ref.shape[1]
    sub = _pick(tn, (SUB_N, LANES))
    col0 = (pl.program_id(1) % blocks_per_expert) * tn
    for s in range(tn // sub):
        sl = slice(s * sub, (s + 1) * sub)
        a = jnp.dot(z_ref[...], w1_ref[0, :, sl], preferred_element_type=F32)
        b = jnp.dot(z_ref[...], w3_ref[0, :, sl], preferred_element_type=F32)
        col = col0 + s * sub + lax.broadcasted_iota(jnp.int32, a.shape, 1)
        o_ref[:, sl] = jnp.where(col < f_valid, a * _sigmoid(a) * b, 0.0).astype(o_ref.dtype)


def _swiglu_comb_kernel(z_ref, w1_ref, w3_ref, comb_ref, o_ref, *, blocks_per_expert):
    z = z_ref[...]
    a = jnp.dot(z, w1_ref[0], preferred_element_type=F32)
    b = jnp.dot(z, w3_ref[0], preferred_element_type=F32)
    e = pl.program_id(1) // blocks_per_expert
    comb = comb_ref[...]
    lane = lax.broadcasted_iota(jnp.int32, comb.shape, 1)
    scale = jnp.sum(jnp.where(lane == e, comb, 0.0), axis=-1, keepdims=True)
    o_ref[...] = (a * _sigmoid(a) * b * scale).astype(o_ref.dtype)


def swiglu_up(z, w1, w3, tm, tn, comb=None, f_out=None):
    m, k = z.shape
    n_e, _, f_valid = w1.shape
    f = f_valid if f_out is None else f_out
    bpe = f // tn
    last_blk = (f_valid - 1) // tn

    def w_map(i, j):
        return (j // bpe, 0, jnp.minimum(j % bpe, last_blk))

    in_specs = [
        pl.BlockSpec((tm, k), lambda i, j: (i, 0)),
        pl.BlockSpec((1, k, tn), w_map),
        pl.BlockSpec((1, k, tn), w_map),
    ]
    args = [z, w1, w3]
    if comb is None:
        kern = functools.partial(_swiglu_kernel, f_valid=f_valid, blocks_per_expert=bpe)
    else:
        kern = functools.partial(_swiglu_comb_kernel, blocks_per_expert=bpe)
        in_specs.append(pl.BlockSpec((tm, LANES), lambda i, j: (i, 0)))
        args.append(comb)
    return pl.pallas_call(
        kern,
        grid=(m // tm, n_e * bpe),
        in_specs=in_specs,
        out_specs=pl.BlockSpec((tm, tn), lambda i, j: (i, j)),
        out_shape=jax.ShapeDtypeStruct((m, n_e * f), BF16),
        compiler_params=_cp("parallel", "parallel"),
        name="swiglu_up",
    )(*args)


MOE_TILE = 1024
TOP_K = 2
TOKEN_CHUNK = 128


def _gather_kernel(idx_ref, src_ref, o_ref, sem, *, rows):
    base = pl.program_id(0) * rows

    def issue(r, carry):
        pltpu.make_async_copy(src_ref.at[idx_ref[base + r]], o_ref.at[r], sem).start()
        return carry

    lax.fori_loop(0, rows, issue, 0)
    pltpu.make_async_copy(src_ref.at[pl.ds(0, rows)], o_ref, sem).wait()


def gather_rows(src3, idx, rows_per_step):
    n_src, chunks, _ = src3.shape
    n_out = idx.shape[0]
    return pl.pallas_call(
        functools.partial(_gather_kernel, rows=rows_per_step),
        grid_spec=pltpu.PrefetchScalarGridSpec(
            num_scalar_prefetch=1,
            grid=(n_out // rows_per_step,),
            in_specs=[pl.BlockSpec(memory_space=pl.ANY)],
            out_specs=pl.BlockSpec((rows_per_step, chunks, TOKEN_CHUNK), lambda i, idx_ref: (i, 0, 0)),
            scratch_shapes=[pltpu.SemaphoreType.DMA(())],
        ),
        out_shape=jax.ShapeDtypeStruct((n_out, chunks, TOKEN_CHUNK), src3.dtype),
        compiler_params=_cp("arbitrary"),
        name="gather_rows",
    )(idx, src3)


def _moe_up_kernel(te_ref, nu_ref, z3_ref, w1_ref, w3_ref, ws_ref, o_ref, z_ref, *, f_valid):
    used = pl.program_id(0) < nu_ref[0]
    tn = o_ref.shape[1]
    nt = (((1,), (1,)), ((), ()))

    @pl.when(jnp.logical_and(used, pl.program_id(1) == 0))
    def _():
        z_ref[...] = z3_ref[...].reshape(z_ref.shape)

    @pl.when(used)
    def _():
        z = z_ref[...]
        a = lax.dot_general(z, w1_ref[0].astype(BF16), nt, preferred_element_type=F32)
        b = lax.dot_general(z, w3_ref[0].astype(BF16), nt, preferred_element_type=F32)
        scale = jnp.concatenate([ws_ref[...]] * (tn // LANES), axis=1)
        col = pl.program_id(1) * tn + lax.broadcasted_iota(jnp.int32, a.shape, 1)
        val = jnp.where(col < f_valid, a * _sigmoid(a) * b * scale, 0.0)
        o_ref[...] = val.astype(o_ref.dtype)

    @pl.when(jnp.logical_not(used))
    def _():
        o_ref[...] = jnp.zeros(o_ref.shape, o_ref.dtype)


def _moe_down_kernel(te_ref, nu_ref, x_ref, w_ref, o_ref, *, f_valid):
    used = pl.program_id(0) < nu_ref[0]

    @pl.when(used)
    def _():
        rows, chunks, _ = o_ref.shape
        sub = _pick(chunks * LANES, (SUB_N, LANES))
        for s in range(chunks * LANES // sub):
            w = w_ref[0, :, s * sub:(s + 1) * sub].astype(BF16)
            if f_valid < w.shape[0]:
                row = lax.broadcasted_iota(jnp.int32, w.shape, 0)
                w = jnp.where(row < f_valid, w, jnp.zeros_like(w))
            part = jnp.dot(x_ref[...], w, preferred_element_type=F32)
            c0 = s * sub // LANES
            o_ref[:, c0:c0 + sub // LANES, :] = part.astype(o_ref.dtype).reshape(rows, sub // LANES, LANES)

    @pl.when(jnp.logical_not(used))
    def _():
        o_ref[...] = jnp.zeros(o_ref.shape, o_ref.dtype)


def _combine_kernel(h_ref, ya_ref, yb_ref, gate_ref, o_ref):
    y = ya_ref[0].astype(F32) + yb_ref[0].astype(F32)
    o_ref[...] = h_ref[...] + gate_ref[0] * y.reshape(o_ref.shape)


def moe_sparse(z3, comb, n_experts, w1t, w3t, w2, f, h2, mod3, row_of_tile, gate_chunk, tm):
    m, d = h2.shape
    f_valid = w1t.shape[1]
    n_tiles = (TOP_K * m) // MOE_TILE + n_experts
    n_rows = n_tiles * MOE_TILE
    gather_step = _pick(m, (512, 256, 128))

    cw = comb[:, :n_experts]
    sel = cw > 0.0
    seli = sel.astype(jnp.int32)
    rank = jnp.cumsum(seli, axis=0) - seli
    cnt = jnp.sum(seli, axis=0)
    padded = ((cnt + MOE_TILE - 1) // MOE_TILE) * MOE_TILE
    gend = jnp.cumsum(padded)
    slot = (gend - padded)[None, :] + rank
    top_w, top_e = lax.top_k(cw, TOP_K)
    slot_ab = jnp.take_along_axis(jnp.where(sel, slot, n_rows - 1), top_e, axis=1).astype(jnp.int32)
    pair_id = jnp.arange(m * TOP_K, dtype=jnp.int32)
    row_pair = jnp.full((n_rows,), -1, jnp.int32).at[slot_ab.reshape(-1)].set(pair_id)
    row_valid = row_pair >= 0
    src_row = jnp.where(row_valid, row_pair // TOP_K, jnp.arange(n_rows, dtype=jnp.int32) % m)
    w_slot = jnp.where(row_valid, top_w.reshape(-1)[jnp.maximum(row_pair, 0)], 0.0)
    w_slot = jnp.broadcast_to(w_slot[:, None], (n_rows, LANES))
    tile_start = jnp.arange(n_tiles, dtype=jnp.int32) * MOE_TILE
    tile_expert = jnp.minimum(jnp.sum((tile_start[:, None] >= gend[None, :]).astype(jnp.int32), axis=1),
                              n_experts - 1).astype(jnp.int32)
    n_used = (gend[-1:] // MOE_TILE).astype(jnp.int32)

    chunks = d // TOKEN_CHUNK
    zs = gather_rows(z3, src_row, gather_step)
    tn_up = _pick(f, (256, 128))
    hid = pl.pallas_call(
        functools.partial(_moe_up_kernel, f_valid=f_valid),
        grid_spec=pltpu.PrefetchScalarGridSpec(
            num_scalar_prefetch=2,
            grid=(n_tiles, f // tn_up),
            in_specs=[
                pl.BlockSpec((MOE_TILE, chunks, TOKEN_CHUNK), lambda i, j, te, nu: (i, 0, 0)),
                pl.BlockSpec((1, tn_up, d), lambda i, j, te, nu: (te[i], j, 0)),
                pl.BlockSpec((1, tn_up, d), lambda i, j, te, nu: (te[i], j, 0)),
                pl.BlockSpec((MOE_TILE, LANES), lambda i, j, te, nu: (i, 0)),
            ],
            out_specs=pl.BlockSpec((MOE_TILE, tn_up), lambda i, j, te, nu: (i, j)),
            scratch_shapes=[pltpu.VMEM((MOE_TILE, d), BF16)],
        ),
        out_shape=jax.ShapeDtypeStruct((n_rows, f), BF16),
        compiler_params=_cp("parallel", "arbitrary"),
        name="moe_up",
    )(tile_expert, n_used, zs, w1t, w3t, w_slot)
    tn_dn = _pick(d, (1024, 512, 256))
    ys = pl.pallas_call(
        functools.partial(_moe_down_kernel, f_valid=f_valid),
        grid_spec=pltpu.PrefetchScalarGridSpec(
            num_scalar_prefetch=2,
            grid=(n_tiles, d // tn_dn),
            in_specs=[
                pl.BlockSpec((MOE_TILE, f), lambda i, j, te, nu: (i, 0)),
                pl.BlockSpec((1, f, tn_dn), lambda i, j, te, nu: (te[i], 0, j)),
            ],
            out_specs=pl.BlockSpec((MOE_TILE, tn_dn // TOKEN_CHUNK, TOKEN_CHUNK), lambda i, j, te, nu: (i, j, 0)),
        ),
        out_shape=jax.ShapeDtypeStruct((n_rows, chunks, TOKEN_CHUNK), BF16),
        compiler_params=_cp("parallel", "parallel"),
        name="moe_down",
    )(tile_expert, n_used, hid, w2)
    yab = gather_rows(ys, slot_ab.T.reshape(-1), gather_step).reshape(TOP_K, m, chunks, TOKEN_CHUNK)
    return pl.pallas_call(
        _combine_kernel,
        grid=(m // tm,),
        in_specs=[
            pl.BlockSpec((tm, d), lambda i: (i, 0)),
            pl.BlockSpec((1, tm, chunks, TOKEN_CHUNK), lambda i: (0, i, 0, 0)),
            pl.BlockSpec((1, tm, chunks, TOKEN_CHUNK), lambda i: (1, i, 0, 0)),
            pl.BlockSpec((1, 1, d), lambda i: (row_of_tile(i), 0, gate_chunk)),
        ],
        out_specs=pl.BlockSpec((tm, d), lambda i: (i, 0)),
        out_shape=jax.ShapeDtypeStruct((m, d), F32),
        compiler_params=_cp("parallel"),
        name="moe_combine",
    )(h2, yab, yab, mod3)


def _merge_kernel(xm_ref, wg_ref, *rest, n_branch):
    y_refs = rest[:n_branch]
    wb_ref, o_ref, acc_ref, y_ref = rest[n_branch:]
    b = pl.program_id(2)

    @pl.when(b == 0)
    def _():
        acc_ref[...] = jnp.zeros(acc_ref.shape, F32)

    for bi in range(n_branch):

        @pl.when(b == bi)
        def _(bi=bi):
            y_ref[...] = y_refs[bi][...]

    sub = _pick(o_ref.shape[1], (SUB_N, LANES))
    for s in range(o_ref.shape[1] // sub):
        sl = slice(s * sub, (s + 1) * sub)
        gate = _sigmoid(jnp.dot(xm_ref[...], wg_ref[0, :, sl], preferred_element_type=F32))
        new = acc_ref[:, sl] + gate * jnp.dot(y_ref[...], wb_ref[0, 0, :, sl], preferred_element_type=F32)
        acc_ref[:, sl] = new
        o_ref[:, sl] = new.astype(o_ref.dtype)


def merge_branches(xm, w_gate, ys, w_branch, tm, tn, layer):
    m, d = xm.shape
    _, n_branch, bw, _ = w_branch.shape
    nj = d // tn
    return pl.pallas_call(
        functools.partial(_merge_kernel, n_branch=n_branch),
        grid=(m // tm, nj, n_branch),
        in_specs=[
            pl.BlockSpec((tm, d), lambda i, j, b: (i, 0)),
            pl.BlockSpec((1, d, tn), lambda i, j, b: (layer, 0, b * nj + j)),
        ] + [pl.BlockSpec((tm, bw), lambda i, j, b: (i, 0))] * n_branch + [
            pl.BlockSpec((1, 1, bw, tn), lambda i, j, b: (layer, b, 0, j)),
        ],
        out_specs=pl.BlockSpec((tm, tn), lambda i, j, b: (i, j)),
        out_shape=jax.ShapeDtypeStruct((m, d), BF16),
        scratch_shapes=[pltpu.VMEM((tm, tn), F32), pltpu.VMEM((tm, bw), BF16)],
        compiler_params=_cp("parallel", "parallel", "arbitrary"),
        name="merge_branches",
    )(xm, w_gate, *ys, w_branch)


def _qk_prep_kernel(pq_ref, pk_ref, gq_ref, gk_ref, cos_ref, sin_ref, grp_ref, qo_ref, ko_ref, *, rope, q_scale):
    tm = pq_ref.shape[0]
    lane = lax.broadcasted_iota(jnp.int32, (tm, LANES), 1)
    first_half = (lane & 16) == 0
    for src, g_ref, dst, scale in ((pq_ref, gq_ref, qo_ref, q_scale), (pk_ref, gk_ref, ko_ref, 1.0)):
        for j in range(src.shape[1] // LANES):
            sl = slice(j * LANES, (j + 1) * LANES)
            x = src[:, sl].astype(F32)
            ms = jnp.dot(x * x, grp_ref[...], preferred_element_type=F32, precision=HI)
            y = x * lax.rsqrt(ms + EPS) * g_ref[...]
            if rope:
                partner = jnp.where(first_half, pltpu.roll(y, LANES - 16, 1), pltpu.roll(y, 16, 1))
                y = y * cos_ref[...] + partner * sin_ref[...]
            dst[:, sl] = (y * scale).astype(dst.dtype)


def qk_prep(p, n_seq, bw, gq, gk, cos_t, sin_t, rope, q_scale, tm):
    m = p.shape[0]
    dh = gq.shape[0]
    reps = LANES // dh
    idx = jnp.arange(LANES)
    grp = jnp.where((idx[:, None] // dh) == (idx[None, :] // dh), 1.0 / dh, 0.0).astype(F32)
    tiles_per_seq = n_seq // tm
    return pl.pallas_call(
        functools.partial(_qk_prep_kernel, rope=rope, q_scale=q_scale),
        grid=(m // tm,),
        in_specs=[
            pl.BlockSpec((tm, bw), lambda i: (i, 0)),
            pl.BlockSpec((tm, bw), lambda i: (i, 1)),
            pl.BlockSpec((1, LANES), lambda i: (0, 0)),
            pl.BlockSpec((1, LANES), lambda i: (0, 0)),
            pl.BlockSpec((tm, LANES), lambda i: (i % tiles_per_seq, 0)),
            pl.BlockSpec((tm, LANES), lambda i: (i % tiles_per_seq, 0)),
            pl.BlockSpec((LANES, LANES), lambda i: (0, 0)),
        ],
        out_specs=[pl.BlockSpec((tm, bw), lambda i: (i, 0)), pl.BlockSpec((tm, bw), lambda i: (i, 0))],
        out_shape=[jax.ShapeDtypeStruct((m, bw), BF16), jax.ShapeDtypeStruct((m, bw), BF16)],
        compiler_params=_cp("parallel"),
        name="qk_prep",
    )(p, p, jnp.tile(gq, reps).reshape(1, LANES), jnp.tile(gk, reps).reshape(1, LANES), cos_t, sin_t, grp)


def rope_tables(n_seq, dh):
    nf = dh // 4
    pos = jnp.arange(n_seq)
    row = (pos // GRID_W).astype(F32)
    col = (pos % GRID_W).astype(F32)
    freqs = ROPE_THETA ** (-jnp.arange(nf, dtype=F32) / nf)
    lane = jnp.arange(LANES)
    d = lane % dh
    use_col = (d // (2 * nf)) == 1
    second = ((d // nf) % 2) == 1
    f = freqs[d % nf]
    ang = jnp.where(use_col[None, :], col[:, None], row[:, None]) * f[None, :]
    return jnp.cos(ang), jnp.where(second[None, :], 1.0, -1.0) * jnp.sin(ang)


def _att_kernel(lam_ref, q_ref, kt_ref, v_ref, g_ref, o_ref, m_ref, acc_ref, s_ref, mc_ref, *, n_chunks, ck,
                out_scale):
    tq = q_ref.shape[1]
    half = LANES // 2
    n_tiles = ck // LANES
    q = q_ref[0]
    lane = lax.broadcasted_iota(jnp.int32, (tq, LANES), 1)
    zero = jnp.zeros_like(q)
    qs = (jnp.where(lane < half, q, zero), jnp.where(lane >= half, q, zero))
    m_ref[...] = jnp.full(m_ref.shape, -jnp.inf, F32)
    acc_ref[...] = jnp.zeros(acc_ref.shape, F32)
    ones = jnp.ones((ck, LANES), BF16)

    def scores(c, slot):
        kt = kt_ref[0, 0, c]
        for ci in range(2):
            s = jnp.dot(qs[ci], kt, preferred_element_type=F32)
            s_ref[slot, ci] = s
            mc = s[:, :LANES]
            for j in range(1, n_tiles):
                mc = jnp.maximum(mc, s[:, j * LANES:(j + 1) * LANES])
            mc_ref[slot, ci] = mc

    def softmax_pv(c, slot):
        v = jnp.concatenate([v_ref[0, pl.ds(pl.multiple_of(c * ck, ck), ck), :], ones], axis=1)
        for ci in range(2):
            m_old = m_ref[ci]
            m_new = jnp.maximum(m_old, jnp.max(mc_ref[slot, ci], axis=-1, keepdims=True))
            alpha = jnp.exp2(m_old - m_new)
            p = jnp.concatenate(
                [jnp.exp2(s_ref[slot, ci, :, j * LANES:(j + 1) * LANES] - m_new).astype(BF16)
                 for j in range(n_tiles)], axis=1)
            pv = jnp.dot(p, v, preferred_element_type=F32)
            acc_ref[ci] = jnp.concatenate([alpha, alpha], axis=1) * acc_ref[ci] + pv
            m_ref[ci] = m_new

    scores(0, 0)
    n_pairs = (n_chunks - 1) // 2

    def body(i, carry):
        c = 2 * i
        scores(c + 1, 1)
        softmax_pv(c, 0)
        scores(c + 2, 0)
        softmax_pv(c + 1, 1)
        return carry

    if n_pairs > 0:
        lax.fori_loop(0, n_pairs, body, 0)
    if (n_chunks - 1) % 2 == 1:
        scores(n_chunks - 1, 1)
        softmax_pv(n_chunks - 2, 0)
        softmax_pv(n_chunks - 1, 1)
    else:
        softmax_pv(n_chunks - 1, 0)

    a1 = acc_ref[0]
    a2 = acc_ref[1]
    o = a1[:, :LANES] / a1[:, LANES:] - lam_ref[0, 0] * (a2[:, :LANES] / a2[:, LANES:])
    ms = jnp.mean(o * o, axis=-1, keepdims=True)
    o_ref[0] = (o * lax.rsqrt(ms + EPS) * g_ref[...] * out_scale).astype(o_ref.dtype)


def diff_attention(q, k, v, lam, g_sub, out_scale, tq, ck):
    b, nq, bw = q.shape
    nk = k.shape[1]
    heads = bw // LANES
    n_chunks = nk // ck
    kt = k.reshape(b, n_chunks, ck, heads, LANES).transpose(0, 3, 1, 4, 2)
    return pl.pallas_call(
        functools.partial(_att_kernel, n_chunks=n_chunks, ck=ck, out_scale=out_scale),
        grid=(b, heads, nq // tq),
        in_specs=[
            pl.BlockSpec(memory_space=pltpu.SMEM),
            pl.BlockSpec((1, tq, LANES), lambda bi, h, i: (bi, i, h)),
            pl.BlockSpec((1, 1, n_chunks, LANES, ck), lambda bi, h, i: (bi, h, 0, 0, 0)),
            pl.BlockSpec((1, nk, LANES), lambda bi, h, i: (bi, 0, h)),
            pl.BlockSpec((1, LANES), lambda bi, h, i: (0, 0)),
        ],
        out_specs=pl.BlockSpec((1, tq, LANES), lambda bi, h, i: (bi, i, h)),
        out_shape=jax.ShapeDtypeStruct((b, nq, bw), BF16),
        scratch_shapes=[
            pltpu.VMEM((2, tq, LANES), F32),
            pltpu.VMEM((2, tq, 2 * LANES), F32),
            pltpu.VMEM((2, 2, tq, ck), F32),
            pltpu.VMEM((2, 2, tq, LANES), F32),
        ],
        compiler_params=_cp("parallel", "parallel", "parallel"),
        name="diff_attention",
    )(lam.reshape(1, 1).astype(F32), q, kt, v, g_sub.reshape(1, LANES))


def _local_kernel(pm_ref, pp_ref, pn_ref, cb_ref, cc_ref, ch_ref, ccp_ref, chp_ref, ccn_ref, chn_ref,
                  pw_ref, ps_ref, cw_ref, yp_ref, yc_ref, scr_ref, *, n_seq, tiles_per_seq):
    ts, bw = pm_ref.shape
    it = pl.program_id(0) % tiles_per_seq
    has_prev = jnp.where(it > 0, 1.0, 0.0).astype(F32)
    has_next = jnp.where(it < tiles_per_seq - 1, 1.0, 0.0).astype(F32)
    gw = bw // len(POOL_WINDOWS)
    pos = it * ts + lax.broadcasted_iota(jnp.int32, (ts, 1), 0)

    scr_ref[0:HALO, :] = pp_ref[...].astype(F32) * has_prev
    scr_ref[HALO:HALO + ts, :] = pm_ref[...].astype(F32)
    scr_ref[HALO + ts:2 * HALO + ts, :] = pn_ref[...].astype(F32) * has_next
    for g, w in enumerate(POOL_WINDOWS):
        sl = slice(g * gw, (g + 1) * gw)
        lo, hi = w // 2, w - 1 - w // 2
        tot = None
        for off in range(-lo, hi + 1):
            part = scr_ref[HALO + off:HALO + off + ts, sl]
            tot = part if tot is None else tot + part
        cnt = (jnp.minimum(pos + hi, n_seq - 1) - jnp.maximum(pos - lo, 0) + 1).astype(F32)
        pooled = tot / cnt - scr_ref[HALO:HALO + ts, sl]
        y = jnp.dot(pooled.astype(BF16), pw_ref[g], preferred_element_type=F32)
        yp_ref[:, sl] = (y * ps_ref[:, sl]).astype(yp_ref.dtype)

    scr_ref[0:HALO, :] = ccp_ref[...].astype(F32) * chp_ref[...].astype(F32) * has_prev
    scr_ref[HALO:HALO + ts, :] = cc_ref[...].astype(F32) * ch_ref[...].astype(F32)
    scr_ref[HALO + ts:2 * HALO + ts, :] = ccn_ref[...].astype(F32) * chn_ref[...].astype(F32) * has_next
    conv = None
    for j in range(CONV_K):
        off = j - CONV_K // 2
        term = cw_ref[j:j + 1, :] * scr_ref[HALO + off:HALO + off + ts, :]
        conv = term if conv is None else conv + term
    yc_ref[...] = (cb_ref[...].astype(F32) * conv).astype(yc_ref.dtype)


def local_mixers(p, n_seq, bw, pool_w, pool_scale, conv_w, ts):
    m = p.shape[0]
    tps = n_seq // ts
    r = ts // HALO
    last_halo = m // HALO - 1

    def main(cb):
        return pl.BlockSpec((ts, bw), lambda i: (i, cb))

    def prev(cb):
        return pl.BlockSpec((HALO, bw), lambda i: (jnp.maximum(i * r - 1, 0), cb))

    def nxt(cb):
        return pl.BlockSpec((HALO, bw), lambda i: (jnp.minimum((i + 1) * r, last_halo), cb))

    n_g = len(POOL_WINDOWS)
    return pl.pallas_call(
        functools.partial(_local_kernel, n_seq=n_seq, tiles_per_seq=tps),
        grid=(m // ts,),
        in_specs=[main(3), prev(3), nxt(3), main(5), main(6), main(7), prev(6), prev(7), nxt(6), nxt(7),
                  pl.BlockSpec((n_g, bw // n_g, bw // n_g), lambda i: (0, 0, 0)),
                  pl.BlockSpec((1, bw), lambda i: (0, 0)),
                  pl.BlockSpec((CONV_K, bw), lambda i: (0, 0))],
        out_specs=[pl.BlockSpec((ts, bw), lambda i: (i, 0)), pl.BlockSpec((ts, bw), lambda i: (i, 0))],
        out_shape=[jax.ShapeDtypeStruct((m, bw), BF16), jax.ShapeDtypeStruct((m, bw), BF16)],
        scratch_shapes=[pltpu.VMEM((ts + 2 * HALO, bw), F32)],
        compiler_params=_cp("parallel"),
        name="local_mixers",
    )(p, p, p, p, p, p, p, p, p, p, pool_w.astype(BF16), pool_scale.reshape(1, bw), conv_w)


def _dft_cos_sin(n, scale):
    k = jnp.arange(n, dtype=jnp.int32)
    ang = ((k[:, None] * k[None, :]) % n).astype(F32) * (2.0 * math.pi / n)
    return jnp.cos(ang) * scale, jnp.sin(ang) * scale


def channel_dft_matrix(bw):
    gw = bw // FOURIER_GROUPS
    c, s = _dft_cos_sin(gw, gw ** -0.5)
    eye = jnp.eye(FOURIER_GROUPS, dtype=F32)
    return jnp.concatenate([jnp.kron(eye, c), jnp.kron(eye, s)], axis=1).astype(BF16)


def _fft_stage2_kernel(p_ref, q_ref, cw_ref, sw_ref, f2_ref, o_ref, b_ref):
    n2 = p_ref.shape[2]
    bw = o_ref.shape[2]
    cw = cw_ref[0]
    sw = sw_ref[0]
    for j in range(bw // LANES):
        lo = slice(j * LANES, (j + 1) * LANES)
        hi = slice(bw + j * LANES, bw + (j + 1) * LANES)
        ar = p_ref[0, 0, :, lo].astype(F32) - q_ref[0, 0, :, hi].astype(F32)
        ai = -(p_ref[0, 0, :, hi].astype(F32) + q_ref[0, 0, :, lo].astype(F32))
        b_ref[0:n2, lo] = (ar * cw + ai * sw).astype(BF16)
        b_ref[n2:2 * n2, lo] = (ai * cw - ar * sw).astype(BF16)
    o_ref[0] = jnp.dot(f2_ref[...], b_ref[...], preferred_element_type=F32).astype(o_ref.dtype)


def fourier_seq_two_stage(z, b, n_seq, bw, n1, n2):
    c1, s1 = _dft_cos_sin(n1, n1 ** -0.5)
    f1 = jnp.concatenate([c1, s1], axis=0).astype(BF16)
    c2, s2 = _dft_cos_sin(n2, n2 ** -0.5)
    f2 = jnp.concatenate([c2, s2], axis=1).astype(BF16)
    k1 = jnp.arange(n1, dtype=jnp.int32)
    t2 = jnp.arange(n2, dtype=jnp.int32)
    ang = (k1[:, None] * t2[None, :]).astype(F32) * (2.0 * math.pi / n_seq)
    cw = jnp.broadcast_to(jnp.cos(ang)[:, :, None], (n1, n2, LANES))
    sw = jnp.broadcast_to(jnp.sin(ang)[:, :, None], (n1, n2, LANES))
    zb = z.reshape(b, n1, n2 * 2 * bw)
    tn1 = _pick(n2 * 2 * bw, (8192, 4096, 2048))
    a = pl.pallas_call(
        _mm_batched_kernel,
        grid=(b, (n2 * 2 * bw) // tn1),
        in_specs=[
            pl.BlockSpec((2 * n1, n1), lambda bi, j: (0, 0)),
            pl.BlockSpec((1, n1, tn1), lambda bi, j: (bi, 0, j)),
        ],
        out_specs=pl.BlockSpec((1, 2 * n1, tn1), lambda bi, j: (bi, 0, j)),
        out_shape=jax.ShapeDtypeStruct((b, 2 * n1, n2 * 2 * bw), BF16),
        compiler_params=_cp("parallel", "parallel"),
        name="fft_stage1",
    )(f1, zb)
    a = a.reshape(b, 2 * n1, n2, 2 * bw)
    out = pl.pallas_call(
        _fft_stage2_kernel,
        grid=(b, n1),
        in_specs=[
            pl.BlockSpec((1, 1, n2, 2 * bw), lambda bi, k: (bi, k, 0, 0)),
            pl.BlockSpec((1, 1, n2, 2 * bw), lambda bi, k: (bi, n1 + k, 0, 0)),
            pl.BlockSpec((1, n2, LANES), lambda bi, k: (k, 0, 0)),
            pl.BlockSpec((1, n2, LANES), lambda bi, k: (k, 0, 0)),
            pl.BlockSpec((n2, 2 * n2), lambda bi, k: (0, 0)),
        ],
        out_specs=pl.BlockSpec((1, n2, bw), lambda bi, k: (bi, 0, k)),
        out_shape=jax.ShapeDtypeStruct((b, n2, n1 * bw), BF16),
        scratch_shapes=[pltpu.VMEM((2 * n2, bw), BF16)],
        compiler_params=_cp("parallel", "parallel"),
        name="fft_stage2",
    )(a, a, cw, sw, f2)
    return out.reshape(b * n_seq, bw)


def fourier_seq_dense(z, b, n_seq, bw):
    c, s = _dft_cos_sin(n_seq, n_seq ** -0.5)
    f = jnp.concatenate([c, -s], axis=1).astype(BF16)
    zb = z.reshape(b, n_seq, 2 * bw)
    outs = [matmul(f, jnp.concatenate([zb[i, :, :bw], zb[i, :, bw:]], axis=0)) for i in range(b)]
    return jnp.concatenate(outs, axis=0)


def _fft_factors(n_seq):
    n2 = LANES
    n1 = n_seq // n2
    return n1, n2


def _pad_last(w, mult):
    pad = (-w.shape[-1]) % mult
    return jnp.pad(w, [(0, 0)] * (w.ndim - 1) + [(0, pad)]) if pad else w


def _pad_rows(w, mult):
    pad = (-w.shape[-2]) % mult
    return jnp.pad(w, [(0, 0)] * (w.ndim - 2) + [(0, pad), (0, 0)]) if pad else w


def kernel(x, c, ctx, c_ctx, w_mod, b_mod, norm_mix, norm_ffn, w_in, w_gate, q_norm, k_norm, lambda_q1, lambda_k1,
           lambda_q2, lambda_k2, subln, pool_w, pool_scale, conv_w, w_branch, w_out, ffn_w1, ffn_w3, ffn_w2,
           router, moe_w1, moe_w3, moe_w2):
    b, n_lat, d = x.shape
    n_ctx = ctx.shape[1]
    depth = w_mod.shape[0]
    bw = d // 4
    dh = q_norm.shape[1]
    m_lat = b * n_lat
    m_ctx = b * n_ctx

    cond8 = jnp.zeros((8, d), F32).at[:b].set(c).at[b].set(c_ctx)
    mod3 = modulation_all(cond8, w_mod, b_mod).reshape(depth * 8, 1, N_MOD * d)

    rope_lat = rope_tables(n_lat, dh)
    rope_ctx = rope_tables(n_ctx, dh)
    w_cdft = channel_dft_matrix(bw)
    n1, n2 = _fft_factors(n_lat)

    h = x.reshape(m_lat, d)
    hc = ctx.reshape(m_ctx, d)

    w_in_b = w_in.astype(BF16)
    w_gate_b = w_gate.astype(BF16)
    w_branch_b = w_branch.astype(BF16)
    w_out_b = w_out.astype(BF16)

    for l in range(depth):
        last = l == depth - 1
        lam_init = 0.8 - 0.6 * math.exp(-0.3 * l)
        lam = (jnp.exp(jnp.sum(lambda_q1[l] * lambda_k1[l])) - jnp.exp(jnp.sum(lambda_q2[l] * lambda_k2[l]))
               + lam_init)

        def lat_row(tm, l=l):
            return lambda i: l * 8 + i // (n_lat // tm)

        def ctx_row(tm, l=l):
            return lambda i: l * 8 + b

        def mm_rows(h2, n_seq, row_fn):
            return _pick(h2.shape[0] if row_fn is ctx_row else n_seq, (1024, 512, 256))

        def mixer_inputs(h2, n_seq, row_fn, rope, tables):
            tm_norm = _pick(n_seq, (256, 128))
            xm_ = mod_norm(h2, norm_mix[l], mod3, row_fn(tm_norm), 0, 1, tm_norm)
            p_ = matmul(xm_, w_in_b, tm=mm_rows(h2, n_seq, row_fn), layer=l)
            q_, k_ = qk_prep(p_, n_seq, bw, q_norm[l], k_norm[l], tables[0], tables[1], rope=rope,
                             q_scale=LOG2_E * dh ** -0.5, tm=_pick(n_seq, (512, 256)))
            return xm_, p_, q_, k_, p_[:, 2 * bw:3 * bw]

        def mixer_output(h2, n_seq, row_fn, xm_, p_, att_, four_):
            pool_, conv_ = local_mixers(p_, n_seq, bw, pool_w[l], pool_scale[l], conv_w[l],
                                        ts=_pick(n_seq, (512, 256)))
            ys = (att_, pool_, four_, conv_)
            tm = mm_rows(h2, n_seq, row_fn)
            merged = merge_branches(xm_, w_gate_b, ys, w_branch_b, tm=tm, tn=512, layer=l)
            return matmul_residual(merged, w_out_b, h2, mod3, row_fn(tm), 2, tm=tm, tn=512, tk=d, first=l, count=1)

        xcm, pc, q_c, k_c, v_c = mixer_inputs(hc, n_ctx, ctx_row, False, rope_ctx)
        xm, p, q, k, v = mixer_inputs(h, n_lat, lat_row, True, rope_lat)

        k_all = jnp.concatenate([k_c.reshape(b, n_ctx, bw), k.reshape(b, n_lat, bw)], axis=1)
        v_all = jnp.concatenate([v_c.reshape(b, n_ctx, bw), v.reshape(b, n_lat, bw)], axis=1)
        nk = n_ctx + n_lat
        ck = _pick(nk, (1408, 768, 384, 256, 128))
        att = diff_attention(q.reshape(b, n_lat, bw), k_all, v_all, lam, subln[l], 1.0 - lam_init,
                             tq=_pick(n_lat, (512, 256, 128)), ck=ck).reshape(m_lat, bw)
        z_lat = matmul(p, w_cdft, x_col=4, tm=_pick(n_lat, (1024, 512, 256)))
        four = fourier_seq_two_stage(z_lat, b, n_lat, bw, n1, n2)
        h_new = mixer_output(h, n_lat, lat_row, xm, p, att, four)

        if not last:
            att_c = diff_attention(q_c.reshape(b, n_ctx, bw), k_c.reshape(b, n_ctx, bw), v_c.reshape(b, n_ctx, bw),
                                   lam, subln[l], 1.0 - lam_init, tq=_pick(n_ctx, (256, 128)),
                                   ck=_pick(n_ctx, (256, 128))).reshape(m_ctx, bw)
            z_ctx = matmul(pc, w_cdft, x_col=4, tm=_pick(n_ctx, (256, 128)))
            four_c = fourier_seq_dense(z_ctx, b, n_ctx, bw)
            hc = mixer_output(hc, n_ctx, ctx_row, xcm, pc, att_c, four_c)
        h = h_new

        if l % 2 == 0:
            w1 = ffn_w1[l // 2].astype(BF16)[None]
            w3 = ffn_w3[l // 2].astype(BF16)[None]
            w2 = _pad_rows(ffn_w2[l // 2].astype(BF16), 1024)[None]
            rt = None
            f_pad = w2.shape[1]
        else:
            w1 = w3 = w2 = None
            rt = router[l // 2]
            f_pad = -(-moe_w2.shape[2] // 256) * 256
        tn_up = _pick(f_pad, (512, 256))
        tk_dn = _pick(f_pad, (2816, 1024, 512, 256))

        def channel_mix(h2, n_seq, row_fn):
            tm_norm = _pick(n_seq, (256, 128))
            zn = mod_norm(h2, norm_ffn[l], mod3, row_fn(tm_norm), 3, 4, tm_norm, router=rt)
            z_, comb = zn if rt is not None else (zn, None)
            tm = mm_rows(h2, n_seq, row_fn)
            if rt is not None and (TOP_K * h2.shape[0]) % MOE_TILE == 0:
                w1t = jnp.swapaxes(moe_w1[l // 2], 1, 2)
                w3t = jnp.swapaxes(moe_w3[l // 2], 1, 2)
                return moe_sparse(z_, comb, rt.shape[1], w1t, w3t, moe_w2[l // 2], f_pad, h2, mod3,
                                  row_fn(tm_norm), 5, tm_norm)
            if rt is not None:
                z_ = z_.reshape(h2.shape)
                w1_, w3_ = (_pad_last(w[l // 2].astype(BF16), 256) for w in (moe_w1, moe_w3))
                w2_ = _pad_rows(moe_w2[l // 2].astype(BF16), 256)
            else:
                w1_, w3_, w2_ = w1, w3, w2
            hid = swiglu_up(z_, w1_, w3_, tm=tm, tn=tn_up, comb=comb, f_out=f_pad)
            return matmul_residual(hid, w2_, h2, mod3, row_fn(tm), 5, tm=tm, tn=1024, tk=tk_dn)

        h = channel_mix(h, n_lat, lat_row)
        if not last:
            hc = channel_mix(hc, n_ctx, ctx_row)

    return h.reshape(b, n_lat, d)
```

```python
import functools
import math

import jax
import jax.numpy as jnp
from jax import lax
from jax.experimental import pallas as pl
from jax.experimental.pallas import tpu as pltpu

F32 = jnp.float32
BF16 = jnp.bfloat16

GRID_W = 64
ROPE_THETA = 10000.0
POOL_WINDOWS = (2, 4, 8, 16)
FOURIER_GROUPS = 4
CONV_K = 3
N_MOD = 6
EPS = 1e-6
LANES = 128
HALO = 16
VMEM_LIMIT = 56 * 1024 * 1024
HI = lax.Precision.HIGHEST
LOG2_E = 1.4426950408889634
SUB_N = 256


def _cp(*sem, vmem=VMEM_LIMIT):
    return pltpu.CompilerParams(dimension_semantics=sem, vmem_limit_bytes=vmem)


def _pick(n, prefs):
    for t in prefs:
        if n % t == 0:
            return t
    return n


def _sigmoid(x):
    return 1.0 / (1.0 + jnp.exp(-x))


def _mod_kernel(c_ref, w_ref, b_ref, o_ref):
    x = c_ref[...]
    s = x * _sigmoid(x)
    acc = jnp.dot(s.astype(BF16), w_ref[0].astype(BF16), preferred_element_type=F32)
    o_ref[0] = acc + b_ref[0]


def modulation_all(cond8, w_mod, b_mod):
    depth, d, cols = w_mod.shape
    tn = _pick(cols, (1024, 512, 256, 128))
    return pl.pallas_call(
        _mod_kernel,
        grid=(depth, cols // tn),
        in_specs=[
            pl.BlockSpec((8, d), lambda l, j: (0, 0)),
            pl.BlockSpec((1, d, tn), lambda l, j: (l, 0, j)),
            pl.BlockSpec((1, 1, tn), lambda l, j: (l, 0, j)),
        ],
        out_specs=pl.BlockSpec((1, 8, tn), lambda l, j: (l, 0, j)),
        out_shape=jax.ShapeDtypeStruct((depth, 8, cols), F32),
        compiler_params=_cp("parallel", "parallel"),
        name="modulation",
    )(cond8, w_mod, b_mod.reshape(depth, 1, cols))


def _norm_body(h_ref, g_ref, sh_ref, sc_ref):
    x = h_ref[...]
    ms = jnp.mean(x * x, axis=-1, keepdims=True)
    y = x * lax.rsqrt(ms + EPS) * g_ref[...]
    return y * (1.0 + sc_ref[0]) + sh_ref[0]


def _norm_kernel(h_ref, g_ref, sh_ref, sc_ref, o_ref):
    o_ref[...] = _norm_body(h_ref, g_ref, sh_ref, sc_ref).astype(o_ref.dtype)


def _norm_route_kernel(h_ref, g_ref, sh_ref, sc_ref, r_ref, o_ref, comb_ref, *, n_experts):
    z = _norm_body(h_ref, g_ref, sh_ref, sc_ref)
    o_ref[...] = z.astype(o_ref.dtype).reshape(o_ref.shape)
    logits = jnp.dot(z, r_ref[...], preferred_element_type=F32, precision=HI)
    lane = lax.broadcasted_iota(jnp.int32, logits.shape, 1).astype(F32)
    neg = jnp.float32(-jnp.inf)
    lg = jnp.where(lane < n_experts, logits, neg)
    m1 = jnp.max(lg, axis=-1, keepdims=True)
    i1 = jnp.min(jnp.where(lg == m1, lane, float(LANES)), axis=-1, keepdims=True)
    lg2 = jnp.where(lane == i1, neg, lg)
    m2 = jnp.max(lg2, axis=-1, keepdims=True)
    i2 = jnp.min(jnp.where(lg2 == m2, lane, float(LANES)), axis=-1, keepdims=True)
    e = jnp.exp(m2 - m1)
    w1 = 1.0 / (1.0 + e)
    w2 = e / (1.0 + e)
    comb_ref[...] = jnp.where(lane == i1, w1, 0.0) + jnp.where(lane == i2, w2, 0.0)


def mod_norm(h, g, mod3, row_of_tile, sh_chunk, sc_chunk, tm, router=None):
    m, d = h.shape
    in_specs = [
        pl.BlockSpec((tm, d), lambda i: (i, 0)),
        pl.BlockSpec((1, d), lambda i: (0, 0)),
        pl.BlockSpec((1, 1, d), lambda i: (row_of_tile(i), 0, sh_chunk)),
        pl.BlockSpec((1, 1, d), lambda i: (row_of_tile(i), 0, sc_chunk)),
    ]
    args = [h, g.reshape(1, d), mod3, mod3]
    if router is None:
        return pl.pallas_call(
            _norm_kernel,
            grid=(m // tm,),
            in_specs=in_specs,
            out_specs=pl.BlockSpec((tm, d), lambda i: (i, 0)),
            out_shape=jax.ShapeDtypeStruct((m, d), BF16),
            compiler_params=_cp("parallel"),
            name="mod_norm",
        )(*args)
    n_experts = router.shape[1]
    rpad = jnp.pad(router, ((0, 0), (0, LANES - n_experts)))
    in_specs.append(pl.BlockSpec((d, LANES), lambda i: (0, 0)))
    return pl.pallas_call(
        functools.partial(_norm_route_kernel, n_experts=n_experts),
        grid=(m // tm,),
        in_specs=in_specs,
        out_specs=[pl.BlockSpec((tm, d // LANES, LANES), lambda i: (i, 0, 0)),
                   pl.BlockSpec((tm, LANES), lambda i: (i, 0))],
        out_shape=[jax.ShapeDtypeStruct((m, d // LANES, LANES), BF16), jax.ShapeDtypeStruct((m, LANES), F32)],
        compiler_params=_cp("parallel"),
        name="mod_norm_route",
    )(*args, rpad)


def _mm_kernel(x_ref, w_ref, o_ref):
    o_ref[...] = jnp.dot(x_ref[...], w_ref[0], preferred_element_type=F32).astype(o_ref.dtype)


def _mm_batched_kernel(x_ref, w_ref, o_ref):
    o_ref[0] = jnp.dot(x_ref[...], w_ref[0], preferred_element_type=F32).astype(o_ref.dtype)


def matmul(x, w, out_dtype=BF16, x_col=0, tm=None, tn=None, layer=0):
    m = x.shape[0]
    if w.ndim == 2:
        w = w[None]
    _, k, n = w.shape
    tm = tm or _pick(m, (1024, 512, 256, 128))
    tn = tn or _pick(n, (1024, 512, 256, 128))
    return pl.pallas_call(
        _mm_kernel,
        grid=(m // tm, n // tn),
        in_specs=[
            pl.BlockSpec((tm, k), lambda i, j: (i, x_col)),
            pl.BlockSpec((1, k, tn), lambda i, j: (layer, 0, j)),
        ],
        out_specs=pl.BlockSpec((tm, tn), lambda i, j: (i, j)),
        out_shape=jax.ShapeDtypeStruct((m, n), out_dtype),
        compiler_params=_cp("parallel", "parallel"),
        name="matmul",
    )(x, w)


def _mm_res_kernel(x_ref, w_ref, res_ref, gate_ref, o_ref, *maybe_acc, nk):
    acc_ref = maybe_acc[0] if nk > 1 else None
    k = pl.program_id(2)
    sub = _pick(o_ref.shape[1], (SUB_N, LANES))
    if nk > 1:

        @pl.when(k == 0)
        def _():
            acc_ref[...] = jnp.zeros(acc_ref.shape, F32)

    for s in range(o_ref.shape[1] // sub):
        sl = slice(s * sub, (s + 1) * sub)
        part = jnp.dot(x_ref[...], w_ref[0, :, sl], preferred_element_type=F32)
        if nk > 1:
            part = acc_ref[:, sl] + part
            acc_ref[:, sl] = part
        o_ref[:, sl] = res_ref[:, sl] + gate_ref[0, :, sl] * part


def matmul_residual(x, w3, res, mod3, row_of_tile, gate_chunk, tm, tn, tk, first=0, count=None):
    m, n = res.shape
    kf = w3.shape[1]
    n_e = w3.shape[0] - first if count is None else count
    kpe = kf // tk
    nk = n_e * kpe
    return pl.pallas_call(
        functools.partial(_mm_res_kernel, nk=nk),
        grid=(m // tm, n // tn, nk),
        in_specs=[
            pl.BlockSpec((tm, tk), lambda i, j, k: (i, k)),
            pl.BlockSpec((1, tk, tn), lambda i, j, k: (first + k // kpe, k % kpe, j)),
            pl.BlockSpec((tm, tn), lambda i, j, k: (i, j)),
            pl.BlockSpec((1, 1, tn), lambda i, j, k: (row_of_tile(i), 0, gate_chunk * (n // tn) + j)),
        ],
        out_specs=pl.BlockSpec((tm, tn), lambda i, j, k: (i, j)),
        out_shape=jax.ShapeDtypeStruct((m, n), F32),
        scratch_shapes=[pltpu.VMEM((tm, tn), F32)] if nk > 1 else [],
        compiler_params=_cp("parallel", "parallel", "arbitrary"),
        name="matmul_residual",
    )(x, w3, res, mod3)


def _swiglu_kernel(z_ref, w1_ref, w3_ref, o_ref, *, f_valid, blocks_per_expert):
    tn = o_ref.shape[1]
    sub = _pick(tn, (SUB_N, LANES))
    col0 = (pl.program_id(1) % blocks_per_expert) * tn
    for s in range(tn // sub):
        sl = slice(s * sub, (s + 1) * sub)
        a = jnp.dot(z_ref[...], w1_ref[0, :, sl], preferred_element_type=F32)
        b = jnp.dot(z_ref[...], w3_ref[0, :, sl], preferred_element_type=F32)
        col = col0 + s * sub + lax.broadcasted_iota(jnp.int32, a.shape, 1)
        o_ref[:, sl] = jnp.where(col < f_valid, a * _sigmoid(a) * b, 0.0).astype(o_ref.dtype)


def _swiglu_comb_kernel(z_ref, w1_ref, w3_ref, comb_ref, o_ref, *, blocks_per_expert):
    z = z_ref[...]
    a = jnp.dot(z, w1_ref[0], preferred_element_type=F32)
    b = jnp.dot(z, w3_ref[0], preferred_element_type=F32)
    e = pl.program_id(1) // blocks_per_expert
    comb = comb_ref[...]
    lane = lax.broadcasted_iota(jnp.int32, comb.shape, 1)
    scale = jnp.sum(jnp.where(lane == e, comb, 0.0), axis=-1, keepdims=True)
    o_ref[...] = (a * _sigmoid(a) * b * scale).astype(o_ref.dtype)


def swiglu_up(z, w1, w3, tm, tn, comb=None, f_out=None):
    m, k = z.shape
    n_e, _, f_valid = w1.shape
    f = f_valid if f_out is None else f_out
    bpe = f // tn
    last_blk = (f_valid - 1) // tn

    def w_map(i, j):
        return (j // bpe, 0, jnp.minimum(j % bpe, last_blk))

    in_specs = [
        pl.BlockSpec((tm, k), lambda i, j: (i, 0)),
        pl.BlockSpec((1, k, tn), w_map),
        pl.BlockSpec((1, k, tn), w_map),
    ]
    args = [z, w1, w3]
    if comb is None:
        kern = functools.partial(_swiglu_kernel, f_valid=f_valid, blocks_per_expert=bpe)
    else:
        kern = functools.partial(_swiglu_comb_kernel, blocks_per_expert=bpe)
        in_specs.append(pl.BlockSpec((tm, LANES), lambda i, j: (i, 0)))
        args.append(comb)
    return pl.pallas_call(
        kern,
        grid=(m // tm, n_e * bpe),
        in_specs=in_specs,
        out_specs=pl.BlockSpec((tm, tn), lambda i, j: (i, j)),
        out_shape=jax.ShapeDtypeStruct((m, n_e * f), BF16),
        compiler_params=_cp("parallel", "parallel"),
        name="swiglu_up",
    )(*args)


MOE_TILE = 1024
TOP_K = 2
TOKEN_CHUNK = 128


def _gather_kernel(idx_ref, src_ref, o_ref, sem, *, rows):
    base = pl.program_id(0) * rows

    def issue(r, carry):
        pltpu.make_async_copy(src_ref.at[idx_ref[base + r]], o_ref.at[r], sem).start()
        return carry

    lax.fori_loop(0, rows, issue, 0)
    pltpu.make_async_copy(src_ref.at[pl.ds(0, rows)], o_ref, sem).wait()


def gather_rows(src3, idx, rows_per_step):
    n_src, chunks, _ = src3.shape
    n_out = idx.shape[0]
    return pl.pallas_call(
        functools.partial(_gather_kernel, rows=rows_per_step),
        grid_spec=pltpu.PrefetchScalarGridSpec(
            num_scalar_prefetch=1,
            grid=(n_out // rows_per_step,),
            in_specs=[pl.BlockSpec(memory_space=pl.ANY)],
            out_specs=pl.BlockSpec((rows_per_step, chunks, TOKEN_CHUNK), lambda i, idx_ref: (i, 0, 0)),
            scratch_shapes=[pltpu.SemaphoreType.DMA(())],
        ),
        out_shape=jax.ShapeDtypeStruct((n_out, chunks, TOKEN_CHUNK), src3.dtype),
        compiler_params=_cp("arbitrary"),
        name="gather_rows",
    )(idx, src3)


def _moe_up_kernel(te_ref, nu_ref, z3_ref, w1_ref, w3_ref, ws_ref, o_ref, z_ref, *, f_valid):
    used = pl.program_id(0) < nu_ref[0]
    tn = o_ref.shape[1]
    nt = (((1,), (1,)), ((), ()))

    @pl.when(jnp.logical_and(used, pl.program_id(1) == 0))
    def _():
        z_ref[...] = z3_ref[...].reshape(z_ref.shape)

    @pl.when(used)
    def _():
        z = z_ref[...]
        a = lax.dot_general(z, w1_ref[0].astype(BF16), nt, preferred_element_type=F32)
        b = lax.dot_general(z, w3_ref[0].astype(BF16), nt, preferred_element_type=F32)
        scale = jnp.concatenate([ws_ref[...]] * (tn // LANES), axis=1)
        col = pl.program_id(1) * tn + lax.broadcasted_iota(jnp.int32, a.shape, 1)
        val = jnp.where(col < f_valid, a * _sigmoid(a) * b * scale, 0.0)
        o_ref[...] = val.astype(o_ref.dtype)

    @pl.when(jnp.logical_not(used))
    def _():
        o_ref[...] = jnp.zeros(o_ref.shape, o_ref.dtype)


def _moe_down_kernel(te_ref, nu_ref, x_ref, w_ref, o_ref, *, f_valid):
    used = pl.program_id(0) < nu_ref[0]

    @pl.when(used)
    def _():
        rows, chunks, _ = o_ref.shape
        sub = _pick(chunks * LANES, (SUB_N, LANES))
        for s in range(chunks * LANES // sub):
            w = w_ref[0, :, s * sub:(s + 1) * sub].astype(BF16)
            if f_valid < w.shape[0]:
                row = lax.broadcasted_iota(jnp.int32, w.shape, 0)
                w = jnp.where(row < f_valid, w, jnp.zeros_like(w))
            part = jnp.dot(x_ref[...], w, preferred_element_type=F32)
            c0 = s * sub // LANES
            o_ref[:, c0:c0 + sub // LANES, :] = part.astype(o_ref.dtype).reshape(rows, sub // LANES, LANES)

    @pl.when(jnp.logical_not(used))
    def _():
        o_ref[...] = jnp.zeros(o_ref.shape, o_ref.dtype)


def _combine_kernel(h_ref, ya_ref, yb_ref, gate_ref, o_ref):
    y = ya_ref[0].astype(F32) + yb_ref[0].astype(F32)
    o_ref[...] = h_ref[...] + gate_ref[0] * y.reshape(o_ref.shape)


def moe_sparse(z3, comb, n_experts, w1t, w3t, w2, f, h2, mod3, row_of_tile, gate_chunk, tm):
    m, d = h2.shape
    f_valid = w1t.shape[1]
    n_tiles = (TOP_K * m) // MOE_TILE + n_experts
    n_rows = n_tiles * MOE_TILE
    gather_step = _pick(m, (512, 256, 128))

    cw = comb[:, :n_experts]
    sel = cw > 0.0
    seli = sel.astype(jnp.int32)
    rank = jnp.cumsum(seli, axis=0) - seli
    cnt = jnp.sum(seli, axis=0)
    padded = ((cnt + MOE_TILE - 1) // MOE_TILE) * MOE_TILE
    gend = jnp.cumsum(padded)
    slot = (gend - padded)[None, :] + rank
    top_w, top_e = lax.top_k(cw, TOP_K)
    slot_ab = jnp.take_along_axis(jnp.where(sel, slot, n_rows - 1), top_e, axis=1).astype(jnp.int32)
    pair_id = jnp.arange(m * TOP_K, dtype=jnp.int32)
    row_pair = jnp.full((n_rows,), -1, jnp.int32).at[slot_ab.reshape(-1)].set(pair_id)
    row_valid = row_pair >= 0
    src_row = jnp.where(row_valid, row_pair // TOP_K, jnp.arange(n_rows, dtype=jnp.int32) % m)
    w_slot = jnp.where(row_valid, top_w.reshape(-1)[jnp.maximum(row_pair, 0)], 0.0)
    w_slot = jnp.broadcast_to(w_slot[:, None], (n_rows, LANES))
    tile_start = jnp.arange(n_tiles, dtype=jnp.int32) * MOE_TILE
    tile_expert = jnp.minimum(jnp.sum((tile_start[:, None] >= gend[None, :]).astype(jnp.int32), axis=1),
                              n_experts - 1).astype(jnp.int32)
    n_used = (gend[-1:] // MOE_TILE).astype(jnp.int32)

    chunks = d // TOKEN_CHUNK
    zs = gather_rows(z3, src_row, gather_step)
    tn_up = _pick(f, (256, 128))
    hid = pl.pallas_call(
        functools.partial(_moe_up_kernel, f_valid=f_valid),
        grid_spec=pltpu.PrefetchScalarGridSpec(
            num_scalar_prefetch=2,
            grid=(n_tiles, f // tn_up),
            in_specs=[
                pl.BlockSpec((MOE_TILE, chunks, TOKEN_CHUNK), lambda i, j, te, nu: (i, 0, 0)),
                pl.BlockSpec((1, tn_up, d), lambda i, j, te, nu: (te[i], j, 0)),
                pl.BlockSpec((1, tn_up, d), lambda i, j, te, nu: (te[i], j, 0)),
                pl.BlockSpec((MOE_TILE, LANES), lambda i, j, te, nu: (i, 0)),
            ],
            out_specs=pl.BlockSpec((MOE_TILE, tn_up), lambda i, j, te, nu: (i, j)),
            scratch_shapes=[pltpu.VMEM((MOE_TILE, d), BF16)],
        ),
        out_shape=jax.ShapeDtypeStruct((n_rows, f), BF16),
        compiler_params=_cp("parallel", "arbitrary"),
        name="moe_up",
    )(tile_expert, n_used, zs, w1t, w3t, w_slot)
    tn_dn = _pick(d, (1024, 512, 256))
    ys = pl.pallas_call(
        functools.partial(_moe_down_kernel, f_valid=f_valid),
        grid_spec=pltpu.PrefetchScalarGridSpec(
            num_scalar_prefetch=2,
            grid=(n_tiles, d // tn_dn),
            in_specs=[
                pl.BlockSpec((MOE_TILE, f), lambda i, j, te, nu: (i, 0)),
                pl.BlockSpec((1, f, tn_dn), lambda i, j, te, nu: (te[i], 0, j)),
            ],
            out_specs=pl.BlockSpec((MOE_TILE, tn_dn // TOKEN_CHUNK, TOKEN_CHUNK), lambda i, j, te, nu: (i, j, 0)),
        ),
        out_shape=jax.ShapeDtypeStruct((n_rows, chunks, TOKEN_CHUNK), BF16),
        compiler_params=_cp("parallel", "parallel"),
        name="moe_down",
    )(tile_expert, n_used, hid, w2)
    yab = gather_rows(ys, slot_ab.T.reshape(-1), gather_step).reshape(TOP_K, m, chunks, TOKEN_CHUNK)
    return pl.pallas_call(
        _combine_kernel,
        grid=(m // tm,),
        in_specs=[
            pl.BlockSpec((tm, d), lambda i: (i, 0)),
            pl.BlockSpec((1, tm, chunks, TOKEN_CHUNK), lambda i: (0, i, 0, 0)),
            pl.BlockSpec((1, tm, chunks, TOKEN_CHUNK), lambda i: (1, i, 0, 0)),
            pl.BlockSpec((1, 1, d), lambda i: (row_of_tile(i), 0, gate_chunk)),
        ],
        out_specs=pl.BlockSpec((tm, d), lambda i: (i, 0)),
        out_shape=jax.ShapeDtypeStruct((m, d), F32),
        compiler_params=_cp("parallel"),
        name="moe_combine",
    )(h2, yab, yab, mod3)


def _merge_kernel(xm_ref, wg_ref, *rest, n_branch):
    y_refs = rest[:n_branch]
    wb_ref, o_ref, acc_ref, y_ref = rest[n_branch:]
    b = pl.program_id(2)

    @pl.when(b == 0)
    def _():
        acc_ref[...] = jnp.zeros(acc_ref.shape, F32)

    for bi in range(n_branch):

        @pl.when(b == bi)
        def _(bi=bi):
            y_ref[...] = y_refs[bi][...]

    sub = _pick(o_ref.shape[1], (SUB_N, LANES))
    for s in range(o_ref.shape[1] // sub):
        sl = slice(s * sub, (s + 1) * sub)
        gate = _sigmoid(jnp.dot(xm_ref[...], wg_ref[0, :, sl], preferred_element_type=F32))
        new = acc_ref[:, sl] + gate * jnp.dot(y_ref[...], wb_ref[0, 0, :, sl], preferred_element_type=F32)
        acc_ref[:, sl] = new
        o_ref[:, sl] = new.astype(o_ref.dtype)


def merge_branches(xm, w_gate, ys, w_branch, tm, tn, layer):
    m, d = xm.shape
    _, n_branch, bw, _ = w_branch.shape
    nj = d // tn
    return pl.pallas_call(
        functools.partial(_merge_kernel, n_branch=n_branch),
        grid=(m // tm, nj, n_branch),
        in_specs=[
            pl.BlockSpec((tm, d), lambda i, j, b: (i, 0)),
            pl.BlockSpec((1, d, tn), lambda i, j, b: (layer, 0, b * nj + j)),
        ] + [pl.BlockSpec((tm, bw), lambda i, j, b: (i, 0))] * n_branch + [
            pl.BlockSpec((1, 1, bw, tn), lambda i, j, b: (layer, b, 0, j)),
        ],
        out_specs=pl.BlockSpec((tm, tn), lambda i, j, b: (i, j)),
        out_shape=jax.ShapeDtypeStruct((m, d), BF16),
        scratch_shapes=[pltpu.VMEM((tm, tn), F32), pltpu.VMEM((tm, bw), BF16)],
        compiler_params=_cp("parallel", "parallel", "arbitrary"),
        name="merge_branches",
    )(xm, w_gate, *ys, w_branch)


def _qk_prep_kernel(pq_ref, pk_ref, gq_ref, gk_ref, cos_ref, sin_ref, grp_ref, qo_ref, ko_ref, *, rope, q_scale,
                    inv_dh):
    tm = pq_ref.shape[0]
    lane = lax.broadcasted_iota(jnp.int32, (tm, LANES), 1)
    first_half = (lane & 16) == 0
    for src, g_ref, dst, scale in ((pq_ref, gq_ref, qo_ref, q_scale), (pk_ref, gk_ref, ko_ref, 1.0)):
        for j in range(src.shape[1] // LANES):
            sl = slice(j * LANES, (j + 1) * LANES)
            x = src[:, sl].astype(F32)
            sq = x * x
            sq_hi = sq.astype(BF16)
            sq_lo = (sq - sq_hi.astype(F32)).astype(BF16)
            ms = (jnp.dot(sq_hi, grp_ref[...], preferred_element_type=F32)
                  + jnp.dot(sq_lo, grp_ref[...], preferred_element_type=F32)) * inv_dh
            y = x * lax.rsqrt(ms + EPS) * g_ref[...]
            if rope:
                partner = jnp.where(first_half, pltpu.roll(y, LANES - 16, 1), pltpu.roll(y, 16, 1))
                y = y * cos_ref[...] + partner * sin_ref[...]
            dst[:, sl] = (y * scale).astype(dst.dtype)


def qk_prep(p, n_seq, bw, gq, gk, cos_t, sin_t, rope, q_scale, tm):
    m = p.shape[0]
    dh = gq.shape[0]
    reps = LANES // dh
    idx = jnp.arange(LANES)
    grp = jnp.where((idx[:, None] // dh) == (idx[None, :] // dh), 1.0, 0.0).astype(BF16)
    tiles_per_seq = n_seq // tm
    return pl.pallas_call(
        functools.partial(_qk_prep_kernel, rope=rope, q_scale=q_scale, inv_dh=1.0 / dh),
        grid=(m // tm,),
        in_specs=[
            pl.BlockSpec((tm, bw), lambda i: (i, 0)),
            pl.BlockSpec((tm, bw), lambda i: (i, 1)),
            pl.BlockSpec((1, LANES), lambda i: (0, 0)),
            pl.BlockSpec((1, LANES), lambda i: (0, 0)),
            pl.BlockSpec((tm, LANES), lambda i: (i % tiles_per_seq, 0)),
            pl.BlockSpec((tm, LANES), lambda i: (i % tiles_per_seq, 0)),
            pl.BlockSpec((LANES, LANES), lambda i: (0, 0)),
        ],
        out_specs=[pl.BlockSpec((tm, bw), lambda i: (i, 0)), pl.BlockSpec((tm, bw), lambda i: (i, 0))],
        out_shape=[jax.ShapeDtypeStruct((m, bw), BF16), jax.ShapeDtypeStruct((m, bw), BF16)],
        compiler_params=_cp("parallel"),
        name="qk_prep",
    )(p, p, jnp.tile(gq, reps).reshape(1, LANES), jnp.tile(gk, reps).reshape(1, LANES), cos_t, sin_t, grp)


def rope_tables(n_seq, dh):
    nf = dh // 4
    pos = jnp.arange(n_seq)
    row = (pos // GRID_W).astype(F32)
    col = (pos % GRID_W).astype(F32)
    freqs = ROPE_THETA ** (-jnp.arange(nf, dtype=F32) / nf)
    lane = jnp.arange(LANES)
    d = lane % dh
    use_col = (d // (2 * nf)) == 1
    second = ((d // nf) % 2) == 1
    f = freqs[d % nf]
    ang = jnp.where(use_col[None, :], col[:, None], row[:, None]) * f[None, :]
    return jnp.cos(ang), jnp.where(second[None, :], 1.0, -1.0) * jnp.sin(ang)


def _att_kernel(lam_ref, q_ref, kt_ref, v_ref, g_ref, o_ref, m_ref, acc_ref, s_ref, mc_ref, *, n_chunks, ck,
                out_scale):
    tq = q_ref.shape[1]
    half = LANES // 2
    n_tiles = ck // LANES
    q = q_ref[0]
    lane = lax.broadcasted_iota(jnp.int32, (tq, LANES), 1)
    zero = jnp.zeros_like(q)
    qs = (jnp.where(lane < half, q, zero), jnp.where(lane >= half, q, zero))
    m_ref[...] = jnp.full(m_ref.shape, -jnp.inf, F32)
    acc_ref[...] = jnp.zeros(acc_ref.shape, F32)
    ones = jnp.ones((ck, LANES), BF16)

    def scores(c, slot):
        kt = kt_ref[0, 0, c]
        for ci in range(2):
            s = jnp.dot(qs[ci], kt, preferred_element_type=F32)
            s_ref[slot, ci] = s
            mc = s[:, :LANES]
            for j in range(1, n_tiles):
                mc = jnp.maximum(mc, s[:, j * LANES:(j + 1) * LANES])
            mc_ref[slot, ci] = mc

    def softmax_pv(c, slot):
        v = jnp.concatenate([v_ref[0, pl.ds(pl.multiple_of(c * ck, ck), ck), :], ones], axis=1)
        for ci in range(2):
            m_old = m_ref[ci]
            m_new = jnp.maximum(m_old, jnp.max(mc_ref[slot, ci], axis=-1, keepdims=True))
            alpha = jnp.exp2(m_old - m_new)
            p = jnp.concatenate(
                [jnp.exp2(s_ref[slot, ci, :, j * LANES:(j + 1) * LANES] - m_new).astype(BF16)
                 for j in range(n_tiles)], axis=1)
            pv = jnp.dot(p, v, preferred_element_type=F32)
            acc_ref[ci] = jnp.concatenate([alpha, alpha], axis=1) * acc_ref[ci] + pv
            m_ref[ci] = m_new

    scores(0, 0)
    n_pairs = (n_chunks - 1) // 2

    def body(i, carry):
        c = 2 * i
        scores(c + 1, 1)
        softmax_pv(c, 0)
        scores(c + 2, 0)
        softmax_pv(c + 1, 1)
        return carry

    if n_pairs > 0:
        lax.fori_loop(0, n_pairs, body, 0)
    if (n_chunks - 1) % 2 == 1:
        scores(n_chunks - 1, 1)
        softmax_pv(n_chunks - 2, 0)
        softmax_pv(n_chunks - 1, 1)
    else:
        softmax_pv(n_chunks - 1, 0)

    a1 = acc_ref[0]
    a2 = acc_ref[1]
    o = a1[:, :LANES] / a1[:, LANES:] - lam_ref[0, 0] * (a2[:, :LANES] / a2[:, LANES:])
    ms = jnp.mean(o * o, axis=-1, keepdims=True)
    o_ref[0] = (o * lax.rsqrt(ms + EPS) * g_ref[...] * out_scale).astype(o_ref.dtype)


def diff_attention(q, k, v, lam, g_sub, out_scale, tq, ck):
    b, nq, bw = q.shape
    nk = k.shape[1]
    heads = bw // LANES
    n_chunks = nk // ck
    kt = k.reshape(b, n_chunks, ck, heads, LANES).transpose(0, 3, 1, 4, 2)
    return pl.pallas_call(
        functools.partial(_att_kernel, n_chunks=n_chunks, ck=ck, out_scale=out_scale),
        grid=(b, heads, nq // tq),
        in_specs=[
            pl.BlockSpec(memory_space=pltpu.SMEM),
            pl.BlockSpec((1, tq, LANES), lambda bi, h, i: (bi, i, h)),
            pl.BlockSpec((1, 1, n_chunks, LANES, ck), lambda bi, h, i: (bi, h, 0, 0, 0)),
            pl.BlockSpec((1, nk, LANES), lambda bi, h, i: (bi, 0, h)),
            pl.BlockSpec((1, LANES), lambda bi, h, i: (0, 0)),
        ],
        out_specs=pl.BlockSpec((1, tq, LANES), lambda bi, h, i: (bi, i, h)),
        out_shape=jax.ShapeDtypeStruct((b, nq, bw), BF16),
        scratch_shapes=[
            pltpu.VMEM((2, tq, LANES), F32),
            pltpu.VMEM((2, tq, 2 * LANES), F32),
            pltpu.VMEM((2, 2, tq, ck), F32),
            pltpu.VMEM((2, 2, tq, LANES), F32),
        ],
        compiler_params=_cp("parallel", "parallel", "parallel"),
        name="diff_attention",
    )(lam.reshape(1, 1).astype(F32), q, kt, v, g_sub.reshape(1, LANES))


def _local_kernel(pm_ref, pp_ref, pn_ref, cb_ref, cc_ref, ch_ref, ccp_ref, chp_ref, ccn_ref, chn_ref,
                  pw_ref, ps_ref, cw_ref, yp_ref, yc_ref, scr_ref, *, n_seq, tiles_per_seq):
    ts, bw = pm_ref.shape
    it = pl.program_id(0) % tiles_per_seq
    has_prev = jnp.where(it > 0, 1.0, 0.0).astype(F32)
    has_next = jnp.where(it < tiles_per_seq - 1, 1.0, 0.0).astype(F32)
    gw = bw // len(POOL_WINDOWS)
    pos = it * ts + lax.broadcasted_iota(jnp.int32, (ts, 1), 0)

    scr_ref[0:HALO, :] = pp_ref[...].astype(F32) * has_prev
    scr_ref[HALO:HALO + ts, :] = pm_ref[...].astype(F32)
    scr_ref[HALO + ts:2 * HALO + ts, :] = pn_ref[...].astype(F32) * has_next
    for g, w in enumerate(POOL_WINDOWS):
        sl = slice(g * gw, (g + 1) * gw)
        lo, hi = w // 2, w - 1 - w // 2
        tot = None
        for off in range(-lo, hi + 1):
            part = scr_ref[HALO + off:HALO + off + ts, sl]
            tot = part if tot is None else tot + part
        cnt = (jnp.minimum(pos + hi, n_seq - 1) - jnp.maximum(pos - lo, 0) + 1).astype(F32)
        pooled = tot / cnt - scr_ref[HALO:HALO + ts, sl]
        y = jnp.dot(pooled.astype(BF16), pw_ref[g], preferred_element_type=F32)
        yp_ref[:, sl] = (y * ps_ref[:, sl]).astype(yp_ref.dtype)

    scr_ref[0:HALO, :] = ccp_ref[...].astype(F32) * chp_ref[...].astype(F32) * has_prev
    scr_ref[HALO:HALO + ts, :] = cc_ref[...].astype(F32) * ch_ref[...].astype(F32)
    scr_ref[HALO + ts:2 * HALO + ts, :] = ccn_ref[...].astype(F32) * chn_ref[...].astype(F32) * has_next
    conv = None
    for j in range(CONV_K):
        off = j - CONV_K // 2
        term = cw_ref[j:j + 1, :] * scr_ref[HALO + off:HALO + off + ts, :]
        conv = term if conv is None else conv + term
    yc_ref[...] = (cb_ref[...].astype(F32) * conv).astype(yc_ref.dtype)


def local_mixers(p, n_seq, bw, pool_w, pool_scale, conv_w, ts):
    m = p.shape[0]
    tps = n_seq // ts
    r = ts // HALO
    last_halo = m // HALO - 1

    def main(cb):
        return pl.BlockSpec((ts, bw), lambda i: (i, cb))

    def prev(cb):
        return pl.BlockSpec((HALO, bw), lambda i: (jnp.maximum(i * r - 1, 0), cb))

    def nxt(cb):
        return pl.BlockSpec((HALO, bw), lambda i: (jnp.minimum((i + 1) * r, last_halo), cb))

    n_g = len(POOL_WINDOWS)
    return pl.pallas_call(
        functools.partial(_local_kernel, n_seq=n_seq, tiles_per_seq=tps),
        grid=(m // ts,),
        in_specs=[main(3), prev(3), nxt(3), main(5), main(6), main(7), prev(6), prev(7), nxt(6), nxt(7),
                  pl.BlockSpec((n_g, bw // n_g, bw // n_g), lambda i: (0, 0, 0)),
                  pl.BlockSpec((1, bw), lambda i: (0, 0)),
                  pl.BlockSpec((CONV_K, bw), lambda i: (0, 0))],
        out_specs=[pl.BlockSpec((ts, bw), lambda i: (i, 0)), pl.BlockSpec((ts, bw), lambda i: (i, 0))],
        out_shape=[jax.ShapeDtypeStruct((m, bw), BF16), jax.ShapeDtypeStruct((m, bw), BF16)],
        scratch_shapes=[pltpu.VMEM((ts + 2 * HALO, bw), F32)],
        compiler_params=_cp("parallel"),
        name="local_mixers",
    )(p, p, p, p, p, p, p, p, p, p, pool_w.astype(BF16), pool_scale.reshape(1, bw), conv_w)


def _dft_cos_sin(n, scale):
    k = jnp.arange(n, dtype=jnp.int32)
    ang = ((k[:, None] * k[None, :]) % n).astype(F32) * (2.0 * math.pi / n)
    return jnp.cos(ang) * scale, jnp.sin(ang) * scale


def channel_dft_matrix(bw):
    gw = bw // FOURIER_GROUPS
    c, s = _dft_cos_sin(gw, gw ** -0.5)
    eye = jnp.eye(FOURIER_GROUPS, dtype=F32)
    return jnp.concatenate([jnp.kron(eye, c), jnp.kron(eye, s)], axis=1).astype(BF16)


def _fft_stage2_kernel(p_ref, q_ref, cw_ref, sw_ref, f2_ref, o_ref, b_ref):
    n2 = p_ref.shape[2]
    bw = o_ref.shape[2]
    cw = cw_ref[0]
    sw = sw_ref[0]
    for j in range(bw // LANES):
        lo = slice(j * LANES, (j + 1) * LANES)
        hi = slice(bw + j * LANES, bw + (j + 1) * LANES)
        ar = p_ref[0, 0, :, lo].astype(F32) - q_ref[0, 0, :, hi].astype(F32)
        ai = -(p_ref[0, 0, :, hi].astype(F32) + q_ref[0, 0, :, lo].astype(F32))
        b_ref[0:n2, lo] = (ar * cw + ai * sw).astype(BF16)
        b_ref[n2:2 * n2, lo] = (ai * cw - ar * sw).astype(BF16)
    o_ref[0] = jnp.dot(f2_ref[...], b_ref[...], preferred_element_type=F32).astype(o_ref.dtype)


def fourier_seq_two_stage(z, b, n_seq, bw, n1, n2):
    c1, s1 = _dft_cos_sin(n1, n1 ** -0.5)
    f1 = jnp.concatenate([c1, s1], axis=0).astype(BF16)
    c2, s2 = _dft_cos_sin(n2, n2 ** -0.5)
    f2 = jnp.concatenate([c2, s2], axis=1).astype(BF16)
    k1 = jnp.arange(n1, dtype=jnp.int32)
    t2 = jnp.arange(n2, dtype=jnp.int32)
    ang = (k1[:, None] * t2[None, :]).astype(F32) * (2.0 * math.pi / n_seq)
    cw = jnp.broadcast_to(jnp.cos(ang)[:, :, None], (n1, n2, LANES))
    sw = jnp.broadcast_to(jnp.sin(ang)[:, :, None], (n1, n2, LANES))
    zb = z.reshape(b, n1, n2 * 2 * bw)
    tn1 = _pick(n2 * 2 * bw, (8192, 4096, 2048))
    a = pl.pallas_call(
        _mm_batched_kernel,
        grid=(b, (n2 * 2 * bw) // tn1),
        in_specs=[
            pl.BlockSpec((2 * n1, n1), lambda bi, j: (0, 0)),
            pl.BlockSpec((1, n1, tn1), lambda bi, j: (bi, 0, j)),
        ],
        out_specs=pl.BlockSpec((1, 2 * n1, tn1), lambda bi, j: (bi, 0, j)),
        out_shape=jax.ShapeDtypeStruct((b, 2 * n1, n2 * 2 * bw), BF16),
        compiler_params=_cp("parallel", "parallel"),
        name="fft_stage1",
    )(f1, zb)
    a = a.reshape(b, 2 * n1, n2, 2 * bw)
    out = pl.pallas_call(
        _fft_stage2_kernel,
        grid=(b, n1),
        in_specs=[
            pl.BlockSpec((1, 1, n2, 2 * bw), lambda bi, k: (bi, k, 0, 0)),
            pl.BlockSpec((1, 1, n2, 2 * bw), lambda bi, k: (bi, n1 + k, 0, 0)),
            pl.BlockSpec((1, n2, LANES), lambda bi, k: (k, 0, 0)),
            pl.BlockSpec((1, n2, LANES), lambda bi, k: (k, 0, 0)),
            pl.BlockSpec((n2, 2 * n2), lambda bi, k: (0, 0)),
        ],
        out_specs=pl.BlockSpec((1, n2, bw), lambda bi, k: (bi, 0, k)),
        out_shape=jax.ShapeDtypeStruct((b, n2, n1 * bw), BF16),
        scratch_shapes=[pltpu.VMEM((2 * n2, bw), BF16)],
        compiler_params=_cp("parallel", "parallel"),
        name="fft_stage2",
    )(a, a, cw, sw, f2)
    return out.reshape(b * n_seq, bw)


def fourier_seq_dense(z, b, n_seq, bw):
    c, s = _dft_cos_sin(n_seq, n_seq ** -0.5)
    f = jnp.concatenate([c, -s], axis=1).astype(BF16)
    zb = z.reshape(b, n_seq, 2 * bw)
    outs = [matmul(f, jnp.concatenate([zb[i, :, :bw], zb[i, :, bw:]], axis=0)) for i in range(b)]
    return jnp.concatenate(outs, axis=0)


def _fft_factors(n_seq):
    n2 = LANES
    n1 = n_seq // n2
    return n1, n2


def _pad_last(w, mult):
    pad = (-w.shape[-1]) % mult
    return jnp.pad(w, [(0, 0)] * (w.ndim - 1) + [(0, pad)]) if pad else w


def _pad_rows(w, mult):
    pad = (-w.shape[-2]) % mult
    return jnp.pad(w, [(0, 0)] * (w.ndim - 2) + [(0, pad), (0, 0)]) if pad else w


def kernel(x, c, ctx, c_ctx, w_mod, b_mod, norm_mix, norm_ffn, w_in, w_gate, q_norm, k_norm, lambda_q1, lambda_k1,
           lambda_q2, lambda_k2, subln, pool_w, pool_scale, conv_w, w_branch, w_out, ffn_w1, ffn_w3, ffn_w2,
           router, moe_w1, moe_w3, moe_w2):
    b, n_lat, d = x.shape
    n_ctx = ctx.shape[1]
    depth = w_mod.shape[0]
    bw = d // 4
    dh = q_norm.shape[1]
    m_lat = b * n_lat
    m_ctx = b * n_ctx

    cond8 = jnp.zeros((8, d), F32).at[:b].set(c).at[b].set(c_ctx)
    mod3 = modulation_all(cond8, w_mod, b_mod).reshape(depth * 8, 1, N_MOD * d)

    rope_lat = rope_tables(n_lat, dh)
    rope_ctx = rope_tables(n_ctx, dh)
    w_cdft = channel_dft_matrix(bw)
    n1, n2 = _fft_factors(n_lat)

    h = x.reshape(m_lat, d)
    hc = ctx.reshape(m_ctx, d)

    w_in_b = w_in.astype(BF16)
    w_gate_b = w_gate.astype(BF16)
    w_branch_b = w_branch.astype(BF16)
    w_out_b = w_out.astype(BF16)

    for l in range(depth):
        last = l == depth - 1
        lam_init = 0.8 - 0.6 * math.exp(-0.3 * l)
        lam = (jnp.exp(jnp.sum(lambda_q1[l] * lambda_k1[l])) - jnp.exp(jnp.sum(lambda_q2[l] * lambda_k2[l]))
               + lam_init)

        def lat_row(tm, l=l):
            return lambda i: l * 8 + i // (n_lat // tm)

        def ctx_row(tm, l=l):
            return lambda i: l * 8 + b

        def mm_rows(h2, n_seq, row_fn):
            return _pick(h2.shape[0] if row_fn is ctx_row else n_seq, (1024, 512, 256))

        def mixer_inputs(h2, n_seq, row_fn, rope, tables):
            tm_norm = _pick(n_seq, (512, 256, 128))
            xm_ = mod_norm(h2, norm_mix[l], mod3, row_fn(tm_norm), 0, 1, tm_norm)
            p_ = matmul(xm_, w_in_b, tm=mm_rows(h2, n_seq, row_fn), layer=l)
            q_, k_ = qk_prep(p_, n_seq, bw, q_norm[l], k_norm[l], tables[0], tables[1], rope=rope,
                             q_scale=LOG2_E * dh ** -0.5, tm=_pick(n_seq, (512, 256)))
            return xm_, p_, q_, k_, p_[:, 2 * bw:3 * bw]

        def mixer_output(h2, n_seq, row_fn, xm_, p_, att_, four_):
            pool_, conv_ = local_mixers(p_, n_seq, bw, pool_w[l], pool_scale[l], conv_w[l],
                                        ts=_pick(n_seq, (512, 256)))
            ys = (att_, pool_, four_, conv_)
            tm = mm_rows(h2, n_seq, row_fn)
            merged = merge_branches(xm_, w_gate_b, ys, w_branch_b, tm=tm, tn=512, layer=l)
            return matmul_residual(merged, w_out_b, h2, mod3, row_fn(tm), 2, tm=tm, tn=1024, tk=d, first=l, count=1)

        xcm, pc, q_c, k_c, v_c = mixer_inputs(hc, n_ctx, ctx_row, False, rope_ctx)
        xm, p, q, k, v = mixer_inputs(h, n_lat, lat_row, True, rope_lat)

        k_all = jnp.concatenate([k_c.reshape(b, n_ctx, bw), k.reshape(b, n_lat, bw)], axis=1)
        v_all = jnp.concatenate([v_c.reshape(b, n_ctx, bw), v.reshape(b, n_lat, bw)], axis=1)
        nk = n_ctx + n_lat
        ck = _pick(nk, (1408, 768, 384, 256, 128))
        att = diff_attention(q.reshape(b, n_lat, bw), k_all, v_all, lam, subln[l], 1.0 - lam_init,
                             tq=_pick(n_lat, (1024, 512, 256, 128)), ck=ck).reshape(m_lat, bw)
        z_lat = matmul(p, w_cdft, x_col=4, tm=_pick(n_lat, (1024, 512, 256)))
        four = fourier_seq_two_stage(z_lat, b, n_lat, bw, n1, n2)
        h_new = mixer_output(h, n_lat, lat_row, xm, p, att, four)

        if not last:
            att_c = diff_attention(q_c.reshape(b, n_ctx, bw), k_c.reshape(b, n_ctx, bw), v_c.reshape(b, n_ctx, bw),
                                   lam, subln[l], 1.0 - lam_init, tq=_pick(n_ctx, (256, 128)),
                                   ck=_pick(n_ctx, (256, 128))).reshape(m_ctx, bw)
            z_ctx = matmul(pc, w_cdft, x_col=4, tm=_pick(n_ctx, (256, 128)))
            four_c = fourier_seq_dense(z_ctx, b, n_ctx, bw)
            hc = mixer_output(hc, n_ctx, ctx_row, xcm, pc, att_c, four_c)
        h = h_new

        if l % 2 == 0:
            w1 = ffn_w1[l // 2].astype(BF16)[None]
            w3 = ffn_w3[l // 2].astype(BF16)[None]
            w2 = _pad_rows(ffn_w2[l // 2].astype(BF16), 1024)[None]
            rt = None
            f_pad = w2.shape[1]
        else:
            w1 = w3 = w2 = None
            rt = router[l // 2]
            f_pad = -(-moe_w2.shape[2] // 256) * 256
        tn_up = _pick(f_pad, (512, 256))
        tk_dn = _pick(f_pad, (2816, 1024, 512, 256))

        def channel_mix(h2, n_seq, row_fn):
            tm_norm = _pick(n_seq, (512, 256, 128))
            zn = mod_norm(h2, norm_ffn[l], mod3, row_fn(tm_norm), 3, 4, tm_norm, router=rt)
            z_, comb = zn if rt is not None else (zn, None)
            tm = mm_rows(h2, n_seq, row_fn)
            if rt is not None and (TOP_K * h2.shape[0]) % MOE_TILE == 0:
                w1t = jnp.swapaxes(moe_w1[l // 2], 1, 2)
                w3t = jnp.swapaxes(moe_w3[l // 2], 1, 2)
                return moe_sparse(z_, comb, rt.shape[1], w1t, w3t, moe_w2[l // 2], f_pad, h2, mod3,
                                  row_fn(tm_norm), 5, tm_norm)
            if rt is not None:
                z_ = z_.reshape(h2.shape)
                w1_, w3_ = (_pad_last(w[l // 2].astype(BF16), 256) for w in (moe_w1, moe_w3))
                w2_ = _pad_rows(moe_w2[l // 2].astype(BF16), 256)
            else:
                w1_, w3_, w2_ = w1, w3, w2
            hid = swiglu_up(z_, w1_, w3_, tm=tm, tn=tn_up, comb=comb, f_out=f_pad)
            return matmul_residual(hid, w2_, h2, mod3, row_fn(tm), 5, tm=tm, tn=1024, tk=tk_dn)

        h = channel_mix(h, n_lat, lat_row)
        if not last:
            hc = channel_mix(hc, n_ctx, ctx_row)

    return h.reshape(b, n_lat, d)
```

```python
import functools
import math

import jax
import jax.numpy as jnp
from jax import lax
from jax.experimental import pallas as pl
from jax.experimental.pallas import tpu as pltpu

F32 = jnp.float32
BF16 = jnp.bfloat16

GRID_W = 64
ROPE_THETA = 10000.0
POOL_WINDOWS = (2, 4, 8, 16)
FOURIER_GROUPS = 4
CONV_K = 3
N_MOD = 6
EPS = 1e-6
LANES = 128
HALO = 16
VMEM_LIMIT = 56 * 1024 * 1024
HI = lax.Precision.HIGHEST
LOG2_E = 1.4426950408889634
SUB_N = 256


def _cp(*sem, vmem=VMEM_LIMIT):
    return pltpu.CompilerParams(dimension_semantics=sem, vmem_limit_bytes=vmem)


def _pick(n, prefs):
    for t in prefs:
        if n % t == 0:
            return t
    return n


def _sigmoid(x):
    return 1.0 / (1.0 + jnp.exp(-x))


def _mod_kernel(c_ref, w_ref, b_ref, o_ref):
    x = c_ref[...]
    s = x * _sigmoid(x)
    acc = jnp.dot(s.astype(BF16), w_ref[0].astype(BF16), preferred_element_type=F32)
    o_ref[0] = acc + b_ref[0]


def modulation_all(cond8, w_mod, b_mod):
    depth, d, cols = w_mod.shape
    tn = _pick(cols, (1024, 512, 256, 128))
    return pl.pallas_call(
        _mod_kernel,
        grid=(depth, cols // tn),
        in_specs=[
            pl.BlockSpec((8, d), lambda l, j: (0, 0)),
            pl.BlockSpec((1, d, tn), lambda l, j: (l, 0, j)),
            pl.BlockSpec((1, 1, tn), lambda l, j: (l, 0, j)),
        ],
        out_specs=pl.BlockSpec((1, 8, tn), lambda l, j: (l, 0, j)),
        out_shape=jax.ShapeDtypeStruct((depth, 8, cols), F32),
        compiler_params=_cp("parallel", "parallel"),
        name="modulation",
    )(cond8, w_mod, b_mod.reshape(depth, 1, cols))


def _norm_body(h_ref, g_ref, sh_ref, sc_ref):
    x = h_ref[...]
    ms = jnp.mean(x * x, axis=-1, keepdims=True)
    y = x * lax.rsqrt(ms + EPS) * g_ref[...]
    return y * (1.0 + sc_ref[0]) + sh_ref[0]


def _norm_kernel(h_ref, g_ref, sh_ref, sc_ref, o_ref):
    o_ref[...] = _norm_body(h_ref, g_ref, sh_ref, sc_ref).astype(o_ref.dtype)


def _norm_route_kernel(h_ref, g_ref, sh_ref, sc_ref, r_ref, o_ref, comb_ref, *, n_experts):
    z = _norm_body(h_ref, g_ref, sh_ref, sc_ref)
    o_ref[...] = z.astype(o_ref.dtype).reshape(o_ref.shape)
    logits = jnp.dot(z, r_ref[...], preferred_element_type=F32, precision=HI)
    lane = lax.broadcasted_iota(jnp.int32, logits.shape, 1).astype(F32)
    neg = jnp.float32(-jnp.inf)
    lg = jnp.where(lane < n_experts, logits, neg)
    m1 = jnp.max(lg, axis=-1, keepdims=True)
    i1 = jnp.min(jnp.where(lg == m1, lane, float(LANES)), axis=-1, keepdims=True)
    lg2 = jnp.where(lane == i1, neg, lg)
    m2 = jnp.max(lg2, axis=-1, keepdims=True)
    i2 = jnp.min(jnp.where(lg2 == m2, lane, float(LANES)), axis=-1, keepdims=True)
    e = jnp.exp(m2 - m1)
    w1 = 1.0 / (1.0 + e)
    w2 = e / (1.0 + e)
    comb_ref[...] = jnp.where(lane == i1, w1, 0.0) + jnp.where(lane == i2, w2, 0.0)


def mod_norm(h, g, mod3, row_of_tile, sh_chunk, sc_chunk, tm, router=None):
    m, d = h.shape
    in_specs = [
        pl.BlockSpec((tm, d), lambda i: (i, 0)),
        pl.BlockSpec((1, d), lambda i: (0, 0)),
        pl.BlockSpec((1, 1, d), lambda i: (row_of_tile(i), 0, sh_chunk)),
        pl.BlockSpec((1, 1, d), lambda i: (row_of_tile(i), 0, sc_chunk)),
    ]
    args = [h, g.reshape(1, d), mod3, mod3]
    if router is None:
        return pl.pallas_call(
            _norm_kernel,
            grid=(m // tm,),
            in_specs=in_specs,
            out_specs=pl.BlockSpec((tm, d), lambda i: (i, 0)),
            out_shape=jax.ShapeDtypeStruct((m, d), BF16),
            compiler_params=_cp("parallel"),
            name="mod_norm",
        )(*args)
    n_experts = router.shape[1]
    rpad = jnp.pad(router, ((0, 0), (0, LANES - n_experts)))
    in_specs.append(pl.BlockSpec((d, LANES), lambda i: (0, 0)))
    return pl.pallas_call(
        functools.partial(_norm_route_kernel, n_experts=n_experts),
        grid=(m // tm,),
        in_specs=in_specs,
        out_specs=[pl.BlockSpec((tm, d // LANES, LANES), lambda i: (i, 0, 0)),
                   pl.BlockSpec((tm, LANES), lambda i: (i, 0))],
        out_shape=[jax.ShapeDtypeStruct((m, d // LANES, LANES), BF16), jax.ShapeDtypeStruct((m, LANES), F32)],
        compiler_params=_cp("parallel"),
        name="mod_norm_route",
    )(*args, rpad)


def _mm_kernel(x_ref, w_ref, o_ref):
    o_ref[...] = jnp.dot(x_ref[...], w_ref[0], preferred_element_type=F32).astype(o_ref.dtype)


def _mm_batched_kernel(x_ref, w_ref, o_ref):
    o_ref[0] = jnp.dot(x_ref[...], w_ref[0], preferred_element_type=F32).astype(o_ref.dtype)


def matmul(x, w, out_dtype=BF16, x_col=0, tm=None, tn=None, layer=0):
    m = x.shape[0]
    if w.ndim == 2:
        w = w[None]
    _, k, n = w.shape
    tm = tm or _pick(m, (1024, 512, 256, 128))
    tn = tn or _pick(n, (1024, 512, 256, 128))
    return pl.pallas_call(
        _mm_kernel,
        grid=(m // tm, n // tn),
        in_specs=[
            pl.BlockSpec((tm, k), lambda i, j: (i, x_col)),
            pl.BlockSpec((1, k, tn), lambda i, j: (layer, 0, j)),
        ],
        out_specs=pl.BlockSpec((tm, tn), lambda i, j: (i, j)),
        out_shape=jax.ShapeDtypeStruct((m, n), out_dtype),
        compiler_params=_cp("parallel", "parallel"),
        name="matmul",
    )(x, w)


def _mm_res_kernel(x_ref, w_ref, res_ref, gate_ref, o_ref, *maybe_acc, nk):
    acc_ref = maybe_acc[0] if nk > 1 else None
    k = pl.program_id(2)
    sub = _pick(o_ref.shape[1], (SUB_N, LANES))
    if nk > 1:

        @pl.when(k == 0)
        def _():
            acc_ref[...] = jnp.zeros(acc_ref.shape, F32)

    for s in range(o_ref.shape[1] // sub):
        sl = slice(s * sub, (s + 1) * sub)
        part = jnp.dot(x_ref[...], w_ref[0, :, sl], preferred_element_type=F32)
        if nk > 1:
            part = acc_ref[:, sl] + part
            acc_ref[:, sl] = part
        o_ref[:, sl] = res_ref[:, sl] + gate_ref[0, :, sl] * part


def matmul_residual(x, w3, res, mod3, row_of_tile, gate_chunk, tm, tn, tk, first=0, count=None):
    m, n = res.shape
    kf = w3.shape[1]
    n_e = w3.shape[0] - first if count is None else count
    kpe = kf // tk
    nk = n_e * kpe
    return pl.pallas_call(
        functools.partial(_mm_res_kernel, nk=nk),
        grid=(m // tm, n // tn, nk),
        in_specs=[
            pl.BlockSpec((tm, tk), lambda i, j, k: (i, k)),
            pl.BlockSpec((1, tk, tn), lambda i, j, k: (first + k // kpe, k % kpe, j)),
            pl.BlockSpec((tm, tn), lambda i, j, k: (i, j)),
            pl.BlockSpec((1, 1, tn), lambda i, j, k: (row_of_tile(i), 0, gate_chunk * (n // tn) + j)),
        ],
        out_specs=pl.BlockSpec((tm, tn), lambda i, j, k: (i, j)),
        out_shape=jax.ShapeDtypeStruct((m, n), F32),
        scratch_shapes=[pltpu.VMEM((tm, tn), F32)] if nk > 1 else [],
        compiler_params=_cp("parallel", "parallel", "arbitrary"),
        name="matmul_residual",
    )(x, w3, res, mod3)


def _swiglu_kernel(z_ref, w1_ref, w3_ref, o_ref, *, f_valid, blocks_per_expert):
    tn = o_ref.shape[1]
    sub = _pick(tn, (SUB_N, LANES))
    col0 = (pl.program_id(1) % blocks_per_expert) * tn
    for s in range(tn // sub):
        sl = slice(s * sub, (s + 1) * sub)
        a = jnp.dot(z_ref[...], w1_ref[0, :, sl], preferred_element_type=F32)
        b = jnp.dot(z_ref[...], w3_ref[0, :, sl], preferred_element_type=F32)
        col = col0 + s * sub + lax.broadcasted_iota(jnp.int32, a.shape, 1)
        o_ref[:, sl] = jnp.where(col < f_valid, a * _sigmoid(a) * b, 0.0).astype(o_ref.dtype)


def _swiglu_comb_kernel(z_ref, w1_ref, w3_ref, comb_ref, o_ref, *, blocks_per_expert):
    z = z_ref[...]
    a = jnp.dot(z, w1_ref[0], preferred_element_type=F32)
    b = jnp.dot(z, w3_ref[0], preferred_element_type=F32)
    e = pl.program_id(1) // blocks_per_expert
    comb = comb_ref[...]
    lane = lax.broadcasted_iota(jnp.int32, comb.shape, 1)
    scale = jnp.sum(jnp.where(lane == e, comb, 0.0), axis=-1, keepdims=True)
    o_ref[...] = (a * _sigmoid(a) * b * scale).astype(o_ref.dtype)


def swiglu_up(z, w1, w3, tm, tn, comb=None, f_out=None):
    m, k = z.shape
    n_e, _, f_valid = w1.shape
    f = f_valid if f_out is None else f_out
    bpe = f // tn
    last_blk = (f_valid - 1) // tn

    def w_map(i, j):
        return (j // bpe, 0, jnp.minimum(j % bpe, last_blk))

    in_specs = [
        pl.BlockSpec((tm, k), lambda i, j: (i, 0)),
        pl.BlockSpec((1, k, tn), w_map),
        pl.BlockSpec((1, k, tn), w_map),
    ]
    args = [z, w1, w3]
    if comb is None:
        kern = functools.partial(_swiglu_kernel, f_valid=f_valid, blocks_per_expert=bpe)
    else:
        kern = functools.partial(_swiglu_comb_kernel, blocks_per_expert=bpe)
        in_specs.append(pl.BlockSpec((tm, LANES), lambda i, j: (i, 0)))
        args.append(comb)
    return pl.pallas_call(
        kern,
        grid=(m // tm, n_e * bpe),
        in_specs=in_specs,
        out_specs=pl.BlockSpec((tm, tn), lambda i, j: (i, j)),
        out_shape=jax.ShapeDtypeStruct((m, n_e * f), BF16),
        compiler_params=_cp("parallel", "parallel"),
        name="swiglu_up",
    )(*args)


MOE_TILE = 1024
TOP_K = 2
TOKEN_CHUNK = 128


def _gather_kernel(idx_ref, src_ref, o_ref, sem, *, rows):
    base = pl.program_id(0) * rows

    def issue(r, carry):
        pltpu.make_async_copy(src_ref.at[idx_ref[base + r]], o_ref.at[r], sem).start()
        return carry

    lax.fori_loop(0, rows, issue, 0)
    pltpu.make_async_copy(src_ref.at[pl.ds(0, rows)], o_ref, sem).wait()


def gather_rows(src3, idx, rows_per_step):
    n_src, chunks, _ = src3.shape
    n_out = idx.shape[0]
    return pl.pallas_call(
        functools.partial(_gather_kernel, rows=rows_per_step),
        grid_spec=pltpu.PrefetchScalarGridSpec(
            num_scalar_prefetch=1,
            grid=(n_out // rows_per_step,),
            in_specs=[pl.BlockSpec(memory_space=pl.ANY)],
            out_specs=pl.BlockSpec((rows_per_step, chunks, TOKEN_CHUNK), lambda i, idx_ref: (i, 0, 0)),
            scratch_shapes=[pltpu.SemaphoreType.DMA(())],
        ),
        out_shape=jax.ShapeDtypeStruct((n_out, chunks, TOKEN_CHUNK), src3.dtype),
        compiler_params=_cp("arbitrary"),
        name="gather_rows",
    )(idx, src3)


def _moe_up_kernel(te_ref, nu_ref, z3_ref, w1_ref, w3_ref, ws_ref, o_ref, z_ref, *, f_valid):
    used = pl.program_id(0) < nu_ref[0]
    tn = o_ref.shape[1]
    nt = (((1,), (1,)), ((), ()))

    @pl.when(jnp.logical_and(used, pl.program_id(1) == 0))
    def _():
        z_ref[...] = z3_ref[...].reshape(z_ref.shape)

    @pl.when(used)
    def _():
        z = z_ref[...]
        a = lax.dot_general(z, w1_ref[0].astype(BF16), nt, preferred_element_type=F32)
        b = lax.dot_general(z, w3_ref[0].astype(BF16), nt, preferred_element_type=F32)
        scale = jnp.concatenate([ws_ref[...]] * (tn // LANES), axis=1)
        col = pl.program_id(1) * tn + lax.broadcasted_iota(jnp.int32, a.shape, 1)
        val = jnp.where(col < f_valid, a * _sigmoid(a) * b * scale, 0.0)
        o_ref[...] = val.astype(o_ref.dtype)

    @pl.when(jnp.logical_not(used))
    def _():
        o_ref[...] = jnp.zeros(o_ref.shape, o_ref.dtype)


def _moe_down_kernel(te_ref, nu_ref, x_ref, w_ref, o_ref, *, f_valid):
    used = pl.program_id(0) < nu_ref[0]

    @pl.when(used)
    def _():
        rows, chunks, _ = o_ref.shape
        sub = _pick(chunks * LANES, (SUB_N, LANES))
        for s in range(chunks * LANES // sub):
            w = w_ref[0, :, s * sub:(s + 1) * sub].astype(BF16)
            if f_valid < w.shape[0]:
                row = lax.broadcasted_iota(jnp.int32, w.shape, 0)
                w = jnp.where(row < f_valid, w, jnp.zeros_like(w))
            part = jnp.dot(x_ref[...], w, preferred_element_type=F32)
            c0 = s * sub // LANES
            o_ref[:, c0:c0 + sub // LANES, :] = part.astype(o_ref.dtype).reshape(rows, sub // LANES, LANES)

    @pl.when(jnp.logical_not(used))
    def _():
        o_ref[...] = jnp.zeros(o_ref.shape, o_ref.dtype)


def _combine_kernel(h_ref, ya_ref, yb_ref, gate_ref, o_ref):
    y = ya_ref[0].astype(F32) + yb_ref[0].astype(F32)
    o_ref[...] = h_ref[...] + gate_ref[0] * y.reshape(o_ref.shape)


def moe_sparse(z3, comb, n_experts, w1t, w3t, w2, f, h2, mod3, row_of_tile, gate_chunk, tm):
    m, d = h2.shape
    f_valid = w1t.shape[1]
    n_tiles = (TOP_K * m) // MOE_TILE + n_experts
    n_rows = n_tiles * MOE_TILE
    gather_step = _pick(m, (512, 256, 128))

    cw = comb[:, :n_experts]
    sel = cw > 0.0
    seli = sel.astype(jnp.int32)
    rank = jnp.cumsum(seli, axis=0) - seli
    cnt = jnp.sum(seli, axis=0)
    padded = ((cnt + MOE_TILE - 1) // MOE_TILE) * MOE_TILE
    gend = jnp.cumsum(padded)
    slot = (gend - padded)[None, :] + rank
    top_w, top_e = lax.top_k(cw, TOP_K)
    slot_ab = jnp.take_along_axis(jnp.where(sel, slot, n_rows - 1), top_e, axis=1).astype(jnp.int32)
    pair_id = jnp.arange(m * TOP_K, dtype=jnp.int32)
    row_pair = jnp.full((n_rows,), -1, jnp.int32).at[slot_ab.reshape(-1)].set(pair_id)
    row_valid = row_pair >= 0
    src_row = jnp.where(row_valid, row_pair // TOP_K, jnp.arange(n_rows, dtype=jnp.int32) % m)
    w_slot = jnp.where(row_valid, top_w.reshape(-1)[jnp.maximum(row_pair, 0)], 0.0)
    w_slot = jnp.broadcast_to(w_slot[:, None], (n_rows, LANES))
    tile_start = jnp.arange(n_tiles, dtype=jnp.int32) * MOE_TILE
    tile_expert = jnp.minimum(jnp.sum((tile_start[:, None] >= gend[None, :]).astype(jnp.int32), axis=1),
                              n_experts - 1).astype(jnp.int32)
    n_used = (gend[-1:] // MOE_TILE).astype(jnp.int32)

    chunks = d // TOKEN_CHUNK
    zs = gather_rows(z3, src_row, gather_step)
    tn_up = _pick(f, (256, 128))
    hid = pl.pallas_call(
        functools.partial(_moe_up_kernel, f_valid=f_valid),
        grid_spec=pltpu.PrefetchScalarGridSpec(
            num_scalar_prefetch=2,
            grid=(n_tiles, f // tn_up),
            in_specs=[
                pl.BlockSpec((MOE_TILE, chunks, TOKEN_CHUNK), lambda i, j, te, nu: (i, 0, 0)),
                pl.BlockSpec((1, tn_up, d), lambda i, j, te, nu: (te[i], j, 0)),
                pl.BlockSpec((1, tn_up, d), lambda i, j, te, nu: (te[i], j, 0)),
                pl.BlockSpec((MOE_TILE, LANES), lambda i, j, te, nu: (i, 0)),
            ],
            out_specs=pl.BlockSpec((MOE_TILE, tn_up), lambda i, j, te, nu: (i, j)),
            scratch_shapes=[pltpu.VMEM((MOE_TILE, d), BF16)],
        ),
        out_shape=jax.ShapeDtypeStruct((n_rows, f), BF16),
        compiler_params=_cp("parallel", "arbitrary"),
        name="moe_up",
    )(tile_expert, n_used, zs, w1t, w3t, w_slot)
    tn_dn = _pick(d, (1024, 512, 256))
    ys = pl.pallas_call(
        functools.partial(_moe_down_kernel, f_valid=f_valid),
        grid_spec=pltpu.PrefetchScalarGridSpec(
            num_scalar_prefetch=2,
            grid=(n_tiles, d // tn_dn),
            in_specs=[
                pl.BlockSpec((MOE_TILE, f), lambda i, j, te, nu: (i, 0)),
                pl.BlockSpec((1, f, tn_dn), lambda i, j, te, nu: (te[i], 0, j)),
            ],
            out_specs=pl.BlockSpec((MOE_TILE, tn_dn // TOKEN_CHUNK, TOKEN_CHUNK), lambda i, j, te, nu: (i, j, 0)),
        ),
        out_shape=jax.ShapeDtypeStruct((n_rows, chunks, TOKEN_CHUNK), BF16),
        compiler_params=_cp("parallel", "parallel"),
        name="moe_down",
    )(tile_expert, n_used, hid, w2)
    yab = gather_rows(ys, slot_ab.T.reshape(-1), gather_step).reshape(TOP_K, m, chunks, TOKEN_CHUNK)
    return pl.pallas_call(
        _combine_kernel,
        grid=(m // tm,),
        in_specs=[
            pl.BlockSpec((tm, d), lambda i: (i, 0)),
            pl.BlockSpec((1, tm, chunks, TOKEN_CHUNK), lambda i: (0, i, 0, 0)),
            pl.BlockSpec((1, tm, chunks, TOKEN_CHUNK), lambda i: (1, i, 0, 0)),
            pl.BlockSpec((1, 1, d), lambda i: (row_of_tile(i), 0, gate_chunk)),
        ],
        out_specs=pl.BlockSpec((tm, d), lambda i: (i, 0)),
        out_shape=jax.ShapeDtypeStruct((m, d), F32),
        compiler_params=_cp("parallel"),
        name="moe_combine",
    )(h2, yab, yab, mod3)


def _merge_kernel(xm_ref, wg_ref, *rest, n_branch):
    y_refs = rest[:n_branch]
    wb_ref, o_ref, acc_ref, y_ref = rest[n_branch:]
    b = pl.program_id(2)

    @pl.when(b == 0)
    def _():
        acc_ref[...] = jnp.zeros(acc_ref.shape, F32)

    for bi in range(n_branch):

        @pl.when(b == bi)
        def _(bi=bi):
            y_ref[...] = y_refs[bi][...]

    sub = _pick(o_ref.shape[1], (SUB_N, LANES))
    for s in range(o_ref.shape[1] // sub):
        sl = slice(s * sub, (s + 1) * sub)
        gate = _sigmoid(jnp.dot(xm_ref[...], wg_ref[0, :, sl], preferred_element_type=F32))
        new = acc_ref[:, sl] + gate * jnp.dot(y_ref[...], wb_ref[0, 0, :, sl], preferred_element_type=F32)
        acc_ref[:, sl] = new
        o_ref[:, sl] = new.astype(o_ref.dtype)


def merge_branches(xm, w_gate, ys, w_branch, tm, tn, layer):
    m, d = xm.shape
    _, n_branch, bw, _ = w_branch.shape
    nj = d // tn
    return pl.pallas_call(
        functools.partial(_merge_kernel, n_branch=n_branch),
        grid=(m // tm, nj, n_branch),
        in_specs=[
            pl.BlockSpec((tm, d), lambda i, j, b: (i, 0)),
            pl.BlockSpec((1, d, tn), lambda i, j, b: (layer, 0, b * nj + j)),
        ] + [pl.BlockSpec((tm, bw), lambda i, j, b: (i, 0))] * n_branch + [
            pl.BlockSpec((1, 1, bw, tn), lambda i, j, b: (layer, b, 0, j)),
        ],
        out_specs=pl.BlockSpec((tm, tn), lambda i, j, b: (i, j)),
        out_shape=jax.ShapeDtypeStruct((m, d), BF16),
        scratch_shapes=[pltpu.VMEM((tm, tn), F32), pltpu.VMEM((tm, bw), BF16)],
        compiler_params=_cp("parallel", "parallel", "arbitrary"),
        name="merge_branches",
    )(xm, w_gate, *ys, w_branch)


def _qk_prep_kernel(pq_ref, pk_ref, gq_ref, gk_ref, cos_ref, sin_ref, grp_ref, qo_ref, ko_ref, *, rope, q_scale,
                    inv_dh):
    tm = pq_ref.shape[0]
    lane = lax.broadcasted_iota(jnp.int32, (tm, LANES), 1)
    first_half = (lane & 16) == 0
    for src, g_ref, dst, scale in ((pq_ref, gq_ref, qo_ref, q_scale), (pk_ref, gk_ref, ko_ref, 1.0)):
        for j in range(src.shape[1] // LANES):
            sl = slice(j * LANES, (j + 1) * LANES)
            x = src[:, sl].astype(F32)
            sq = x * x
            sq_hi = sq.astype(BF16)
            sq_lo = (sq - sq_hi.astype(F32)).astype(BF16)
            ms = (jnp.dot(sq_hi, grp_ref[...], preferred_element_type=F32)
                  + jnp.dot(sq_lo, grp_ref[...], preferred_element_type=F32)) * inv_dh
            y = x * lax.rsqrt(ms + EPS) * g_ref[...]
            if rope:
                partner = jnp.where(first_half, pltpu.roll(y, LANES - 16, 1), pltpu.roll(y, 16, 1))
                y = y * cos_ref[...] + partner * sin_ref[...]
            dst[:, sl] = (y * scale).astype(dst.dtype)


def qk_prep(p, n_seq, bw, gq, gk, cos_t, sin_t, rope, q_scale, tm):
    m = p.shape[0]
    dh = gq.shape[0]
    reps = LANES // dh
    idx = jnp.arange(LANES)
    grp = jnp.where((idx[:, None] // dh) == (idx[None, :] // dh), 1.0, 0.0).astype(BF16)
    tiles_per_seq = n_seq // tm
    return pl.pallas_call(
        functools.partial(_qk_prep_kernel, rope=rope, q_scale=q_scale, inv_dh=1.0 / dh),
        grid=(m // tm,),
        in_specs=[
            pl.BlockSpec((tm, bw), lambda i: (i, 0)),
            pl.BlockSpec((tm, bw), lambda i: (i, 1)),
            pl.BlockSpec((1, LANES), lambda i: (0, 0)),
            pl.BlockSpec((1, LANES), lambda i: (0, 0)),
            pl.BlockSpec((tm, LANES), lambda i: (i % tiles_per_seq, 0)),
            pl.BlockSpec((tm, LANES), lambda i: (i % tiles_per_seq, 0)),
            pl.BlockSpec((LANES, LANES), lambda i: (0, 0)),
        ],
        out_specs=[pl.BlockSpec((tm, bw), lambda i: (i, 0)), pl.BlockSpec((tm, bw), lambda i: (i, 0))],
        out_shape=[jax.ShapeDtypeStruct((m, bw), BF16), jax.ShapeDtypeStruct((m, bw), BF16)],
        compiler_params=_cp("parallel"),
        name="qk_prep",
    )(p, p, jnp.tile(gq, reps).reshape(1, LANES), jnp.tile(gk, reps).reshape(1, LANES), cos_t, sin_t, grp)


def rope_tables(n_seq, dh):
    nf = dh // 4
    pos = jnp.arange(n_seq)
    row = (pos // GRID_W).astype(F32)
    col = (pos % GRID_W).astype(F32)
    freqs = ROPE_THETA ** (-jnp.arange(nf, dtype=F32) / nf)
    lane = jnp.arange(LANES)
    d = lane % dh
    use_col = (d // (2 * nf)) == 1
    second = ((d // nf) % 2) == 1
    f = freqs[d % nf]
    ang = jnp.where(use_col[None, :], col[:, None], row[:, None]) * f[None, :]
    return jnp.cos(ang), jnp.where(second[None, :], 1.0, -1.0) * jnp.sin(ang)


def _att_kernel(lam_ref, q_ref, kt_ref, v_ref, g_ref, o_ref, m_ref, acc_ref, s_ref, mc_ref, *, n_chunks, ck,
                out_scale):
    tq = q_ref.shape[1]
    half = LANES // 2
    n_tiles = ck // LANES
    q = q_ref[0]
    lane = lax.broadcasted_iota(jnp.int32, (tq, LANES), 1)
    zero = jnp.zeros_like(q)
    qs = (jnp.where(lane < half, q, zero), jnp.where(lane >= half, q, zero))
    m_ref[...] = jnp.full(m_ref.shape, -jnp.inf, F32)
    acc_ref[...] = jnp.zeros(acc_ref.shape, F32)
    ones = jnp.ones((ck, LANES), BF16)

    def scores(c, slot):
        kt = kt_ref[0, 0, c]
        for ci in range(2):
            s = jnp.dot(qs[ci], kt, preferred_element_type=F32)
            s_ref[slot, ci] = s
            mc = s[:, :LANES]
            for j in range(1, n_tiles):
                mc = jnp.maximum(mc, s[:, j * LANES:(j + 1) * LANES])
            mc_ref[slot, ci] = mc

    def softmax_pv(c, slot):
        v = jnp.concatenate([v_ref[0, pl.ds(pl.multiple_of(c * ck, ck), ck), :], ones], axis=1)
        for ci in range(2):
            m_old = m_ref[ci]
            m_new = jnp.maximum(m_old, jnp.max(mc_ref[slot, ci], axis=-1, keepdims=True))
            alpha = jnp.exp2(m_old - m_new)
            p = jnp.concatenate(
                [jnp.exp2(s_ref[slot, ci, :, j * LANES:(j + 1) * LANES] - m_new).astype(BF16)
                 for j in range(n_tiles)], axis=1)
            pv = jnp.dot(p, v, preferred_element_type=F32)
            acc_ref[ci] = jnp.concatenate([alpha, alpha], axis=1) * acc_ref[ci] + pv
            m_ref[ci] = m_new

    scores(0, 0)
    n_pairs = (n_chunks - 1) // 2

    def body(i, carry):
        c = 2 * i
        scores(c + 1, 1)
        softmax_pv(c, 0)
        scores(c + 2, 0)
        softmax_pv(c + 1, 1)
        return carry

    if n_pairs > 0:
        lax.fori_loop(0, n_pairs, body, 0)
    if (n_chunks - 1) % 2 == 1:
        scores(n_chunks - 1, 1)
        softmax_pv(n_chunks - 2, 0)
        softmax_pv(n_chunks - 1, 1)
    else:
        softmax_pv(n_chunks - 1, 0)

    a1 = acc_ref[0]
    a2 = acc_ref[1]
    o = a1[:, :LANES] / a1[:, LANES:] - lam_ref[0, 0] * (a2[:, :LANES] / a2[:, LANES:])
    ms = jnp.mean(o * o, axis=-1, keepdims=True)
    o_ref[0] = (o * lax.rsqrt(ms + EPS) * g_ref[...] * out_scale).astype(o_ref.dtype)


def diff_attention(q, k, v, lam, g_sub, out_scale, tq, ck):
    b, nq, bw = q.shape
    nk = k.shape[1]
    heads = bw // LANES
    n_chunks = nk // ck
    kt = k.reshape(b, n_chunks, ck, heads, LANES).transpose(0, 3, 1, 4, 2)
    return pl.pallas_call(
        functools.partial(_att_kernel, n_chunks=n_chunks, ck=ck, out_scale=out_scale),
        grid=(b, heads, nq // tq),
        in_specs=[
            pl.BlockSpec(memory_space=pltpu.SMEM),
            pl.BlockSpec((1, tq, LANES), lambda bi, h, i: (bi, i, h)),
            pl.BlockSpec((1, 1, n_chunks, LANES, ck), lambda bi, h, i: (bi, h, 0, 0, 0)),
            pl.BlockSpec((1, nk, LANES), lambda bi, h, i: (bi, 0, h)),
            pl.BlockSpec((1, LANES), lambda bi, h, i: (0, 0)),
        ],
        out_specs=pl.BlockSpec((1, tq, LANES), lambda bi, h, i: (bi, i, h)),
        out_shape=jax.ShapeDtypeStruct((b, nq, bw), BF16),
        scratch_shapes=[
            pltpu.VMEM((2, tq, LANES), F32),
            pltpu.VMEM((2, tq, 2 * LANES), F32),
            pltpu.VMEM((2, 2, tq, ck), F32),
            pltpu.VMEM((2, 2, tq, LANES), F32),
        ],
        compiler_params=_cp("parallel", "parallel", "parallel"),
        name="diff_attention",
    )(lam.reshape(1, 1).astype(F32), q, kt, v, g_sub.reshape(1, LANES))


def _local_kernel(pm_ref, pp_ref, pn_ref, cb_ref, cc_ref, ch_ref, ccp_ref, chp_ref, ccn_ref, chn_ref,
                  pw_ref, ps_ref, cw_ref, yp_ref, yc_ref, scr_ref, *, n_seq, tiles_per_seq):
    ts, bw = pm_ref.shape
    it = pl.program_id(0) % tiles_per_seq
    has_prev = jnp.where(it > 0, 1.0, 0.0).astype(F32)
    has_next = jnp.where(it < tiles_per_seq - 1, 1.0, 0.0).astype(F32)
    gw = bw // len(POOL_WINDOWS)
    pos = it * ts + lax.broadcasted_iota(jnp.int32, (ts, 1), 0)

    scr_ref[0:HALO, :] = pp_ref[...].astype(F32) * has_prev
    scr_ref[HALO:HALO + ts, :] = pm_ref[...].astype(F32)
    scr_ref[HALO + ts:2 * HALO + ts, :] = pn_ref[...].astype(F32) * has_next
    for g, w in enumerate(POOL_WINDOWS):
        sl = slice(g * gw, (g + 1) * gw)
        lo, hi = w // 2, w - 1 - w // 2
        tot = None
        for off in range(-lo, hi + 1):
            part = scr_ref[HALO + off:HALO + off + ts, sl]
            tot = part if tot is None else tot + part
        cnt = (jnp.minimum(pos + hi, n_seq - 1) - jnp.maximum(pos - lo, 0) + 1).astype(F32)
        pooled = tot / cnt - scr_ref[HALO:HALO + ts, sl]
        y = jnp.dot(pooled.astype(BF16), pw_ref[g], preferred_element_type=F32)
        yp_ref[:, sl] = (y * ps_ref[:, sl]).astype(yp_ref.dtype)

    scr_ref[0:HALO, :] = ccp_ref[...].astype(F32) * chp_ref[...].astype(F32) * has_prev
    scr_ref[HALO:HALO + ts, :] = cc_ref[...].astype(F32) * ch_ref[...].astype(F32)
    scr_ref[HALO + ts:2 * HALO + ts, :] = ccn_ref[...].astype(F32) * chn_ref[...].astype(F32) * has_next
    conv = None
    for j in range(CONV_K):
        off = j - CONV_K // 2
        term = cw_ref[j:j + 1, :] * scr_ref[HALO + off:HALO + off + ts, :]
        conv = term if conv is None else conv + term
    yc_ref[...] = (cb_ref[...].astype(F32) * conv).astype(yc_ref.dtype)


def local_mixers(p, n_seq, bw, pool_w, pool_scale, conv_w, ts):
    m = p.shape[0]
    tps = n_seq // ts
    r = ts // HALO
    last_halo = m // HALO - 1

    def main(cb):
        return pl.BlockSpec((ts, bw), lambda i: (i, cb))

    def prev(cb):
        return pl.BlockSpec((HALO, bw), lambda i: (jnp.maximum(i * r - 1, 0), cb))

    def nxt(cb):
        return pl.BlockSpec((HALO, bw), lambda i: (jnp.minimum((i + 1) * r, last_halo), cb))

    n_g = len(POOL_WINDOWS)
    return pl.pallas_call(
        functools.partial(_local_kernel, n_seq=n_seq, tiles_per_seq=tps),
        grid=(m // ts,),
        in_specs=[main(3), prev(3), nxt(3), main(5), main(6), main(7), prev(6), prev(7), nxt(6), nxt(7),
                  pl.BlockSpec((n_g, bw // n_g, bw // n_g), lambda i: (0, 0, 0)),
                  pl.BlockSpec((1, bw), lambda i: (0, 0)),
                  pl.BlockSpec((CONV_K, bw), lambda i: (0, 0))],
        out_specs=[pl.BlockSpec((ts, bw), lambda i: (i, 0)), pl.BlockSpec((ts, bw), lambda i: (i, 0))],
        out_shape=[jax.ShapeDtypeStruct((m, bw), BF16), jax.ShapeDtypeStruct((m, bw), BF16)],
        scratch_shapes=[pltpu.VMEM((ts + 2 * HALO, bw), F32)],
        compiler_params=_cp("parallel"),
        name="local_mixers",
    )(p, p, p, p, p, p, p, p, p, p, pool_w.astype(BF16), pool_scale.reshape(1, bw), conv_w)


def _dft_cos_sin(n, scale):
    k = jnp.arange(n, dtype=jnp.int32)
    ang = ((k[:, None] * k[None, :]) % n).astype(F32) * (2.0 * math.pi / n)
    return jnp.cos(ang) * scale, jnp.sin(ang) * scale


def channel_dft_matrix(bw):
    gw = bw // FOURIER_GROUPS
    c, s = _dft_cos_sin(gw, gw ** -0.5)
    eye = jnp.eye(FOURIER_GROUPS, dtype=F32)
    return jnp.concatenate([jnp.kron(eye, c), jnp.kron(eye, s)], axis=1).astype(BF16)


def _fft_stage2_kernel(p_ref, q_ref, cw_ref, sw_ref, f2_ref, o_ref, b_ref):
    n2 = p_ref.shape[2]
    bw = o_ref.shape[2]
    cw = cw_ref[0]
    sw = sw_ref[0]
    for j in range(bw // LANES):
        lo = slice(j * LANES, (j + 1) * LANES)
        hi = slice(bw + j * LANES, bw + (j + 1) * LANES)
        ar = p_ref[0, 0, :, lo].astype(F32) - q_ref[0, 0, :, hi].astype(F32)
        ai = -(p_ref[0, 0, :, hi].astype(F32) + q_ref[0, 0, :, lo].astype(F32))
        b_ref[0:n2, lo] = (ar * cw + ai * sw).astype(BF16)
        b_ref[n2:2 * n2, lo] = (ai * cw - ar * sw).astype(BF16)
    o_ref[0] = jnp.dot(f2_ref[...], b_ref[...], preferred_element_type=F32).astype(o_ref.dtype)


def fourier_seq_two_stage(z, b, n_seq, bw, n1, n2):
    c1, s1 = _dft_cos_sin(n1, n1 ** -0.5)
    f1 = jnp.concatenate([c1, s1], axis=0).astype(BF16)
    c2, s2 = _dft_cos_sin(n2, n2 ** -0.5)
    f2 = jnp.concatenate([c2, s2], axis=1).astype(BF16)
    k1 = jnp.arange(n1, dtype=jnp.int32)
    t2 = jnp.arange(n2, dtype=jnp.int32)
    ang = (k1[:, None] * t2[None, :]).astype(F32) * (2.0 * math.pi / n_seq)
    cw = jnp.broadcast_to(jnp.cos(ang)[:, :, None], (n1, n2, LANES))
    sw = jnp.broadcast_to(jnp.sin(ang)[:, :, None], (n1, n2, LANES))
    zb = z.reshape(b, n1, n2 * 2 * bw)
    tn1 = _pick(n2 * 2 * bw, (8192, 4096, 2048))
    a = pl.pallas_call(
        _mm_batched_kernel,
        grid=(b, (n2 * 2 * bw) // tn1),
        in_specs=[
            pl.BlockSpec((2 * n1, n1), lambda bi, j: (0, 0)),
            pl.BlockSpec((1, n1, tn1), lambda bi, j: (bi, 0, j)),
        ],
        out_specs=pl.BlockSpec((1, 2 * n1, tn1), lambda bi, j: (bi, 0, j)),
        out_shape=jax.ShapeDtypeStruct((b, 2 * n1, n2 * 2 * bw), BF16),
        compiler_params=_cp("parallel", "parallel"),
        name="fft_stage1",
    )(f1, zb)
    a = a.reshape(b, 2 * n1, n2, 2 * bw)
    out = pl.pallas_call(
        _fft_stage2_kernel,
        grid=(b, n1),
        in_specs=[
            pl.BlockSpec((1, 1, n2, 2 * bw), lambda bi, k: (bi, k, 0, 0)),
            pl.BlockSpec((1, 1, n2, 2 * bw), lambda bi, k: (bi, n1 + k, 0, 0)),
            pl.BlockSpec((1, n2, LANES), lambda bi, k: (k, 0, 0)),
            pl.BlockSpec((1, n2, LANES), lambda bi, k: (k, 0, 0)),
            pl.BlockSpec((n2, 2 * n2), lambda bi, k: (0, 0)),
        ],
        out_specs=pl.BlockSpec((1, n2, bw), lambda bi, k: (bi, 0, k)),
        out_shape=jax.ShapeDtypeStruct((b, n2, n1 * bw), BF16),
        scratch_shapes=[pltpu.VMEM((2 * n2, bw), BF16)],
        compiler_params=_cp("parallel", "parallel"),
        name="fft_stage2",
    )(a, a, cw, sw, f2)
    return out.reshape(b * n_seq, bw)


def fourier_seq_dense(z, b, n_seq, bw):
    c, s = _dft_cos_sin(n_seq, n_seq ** -0.5)
    f = jnp.concatenate([c, -s], axis=1).astype(BF16)
    zb = z.reshape(b, n_seq, 2 * bw)
    outs = [matmul(f, jnp.concatenate([zb[i, :, :bw], zb[i, :, bw:]], axis=0)) for i in range(b)]
    return jnp.concatenate(outs, axis=0)


def _fft_factors(n_seq):
    n2 = LANES
    n1 = n_seq // n2
    return n1, n2


def _pad_last(w, mult):
    pad = (-w.shape[-1]) % mult
    return jnp.pad(w, [(0, 0)] * (w.ndim - 1) + [(0, pad)]) if pad else w


def _pad_rows(w, mult):
    pad = (-w.shape[-2]) % mult
    return jnp.pad(w, [(0, 0)] * (w.ndim - 2) + [(0, pad), (0, 0)]) if pad else w


def kernel(x, c, ctx, c_ctx, w_mod, b_mod, norm_mix, norm_ffn, w_in, w_gate, q_norm, k_norm, lambda_q1, lambda_k1,
           lambda_q2, lambda_k2, subln, pool_w, pool_scale, conv_w, w_branch, w_out, ffn_w1, ffn_w3, ffn_w2,
           router, moe_w1, moe_w3, moe_w2):
    b, n_lat, d = x.shape
    n_ctx = ctx.shape[1]
    depth = w_mod.shape[0]
    bw = d // 4
    dh = q_norm.shape[1]
    m_lat = b * n_lat
    m_ctx = b * n_ctx

    cond8 = jnp.zeros((8, d), F32).at[:b].set(c).at[b].set(c_ctx)
    mod3 = modulation_all(cond8, w_mod, b_mod).reshape(depth * 8, 1, N_MOD * d)

    rope_lat = rope_tables(n_lat, dh)
    rope_ctx = rope_tables(n_ctx, dh)
    w_cdft = channel_dft_matrix(bw)
    n1, n2 = _fft_factors(n_lat)

    h = x.reshape(m_lat, d)
    hc = ctx.reshape(m_ctx, d)

    w_in_b = w_in.astype(BF16)
    w_gate_b = w_gate.astype(BF16)
    w_branch_b = w_branch.astype(BF16)
    w_out_b = w_out.astype(BF16)

    for l in range(depth):
        last = l == depth - 1
        lam_init = 0.8 - 0.6 * math.exp(-0.3 * l)
        lam = (jnp.exp(jnp.sum(lambda_q1[l] * lambda_k1[l])) - jnp.exp(jnp.sum(lambda_q2[l] * lambda_k2[l]))
               + lam_init)

        def lat_row(tm, l=l):
            return lambda i: l * 8 + i // (n_lat // tm)

        def ctx_row(tm, l=l):
            return lambda i: l * 8 + b

        def mm_rows(h2, n_seq, row_fn):
            return _pick(h2.shape[0] if row_fn is ctx_row else n_seq, (1024, 512, 256))

        def mixer_inputs(h2, n_seq, row_fn, rope, tables):
            tm_norm = _pick(n_seq, (512, 256, 128))
            xm_ = mod_norm(h2, norm_mix[l], mod3, row_fn(tm_norm), 0, 1, tm_norm)
            p_ = matmul(xm_, w_in_b, tm=mm_rows(h2, n_seq, row_fn), layer=l)
            q_, k_ = qk_prep(p_, n_seq, bw, q_norm[l], k_norm[l], tables[0], tables[1], rope=rope,
                             q_scale=LOG2_E * dh ** -0.5, tm=_pick(n_seq, (512, 256)))
            return xm_, p_, q_, k_, p_[:, 2 * bw:3 * bw]

        def mixer_output(h2, n_seq, row_fn, xm_, p_, att_, four_):
            pool_, conv_ = local_mixers(p_, n_seq, bw, pool_w[l], pool_scale[l], conv_w[l],
                                        ts=_pick(n_seq, (512, 256)))
            ys = (att_, pool_, four_, conv_)
            tm = mm_rows(h2, n_seq, row_fn)
            merged = merge_branches(xm_, w_gate_b, ys, w_branch_b, tm=tm, tn=512, layer=l)
            return matmul_residual(merged, w_out_b, h2, mod3, row_fn(tm), 2, tm=tm, tn=1024, tk=d, first=l, count=1)

        xcm, pc, q_c, k_c, v_c = mixer_inputs(hc, n_ctx, ctx_row, False, rope_ctx)
        xm, p, q, k, v = mixer_inputs(h, n_lat, lat_row, True, rope_lat)

        k_all = jnp.concatenate([k_c.reshape(b, n_ctx, bw), k.reshape(b, n_lat, bw)], axis=1)
        v_all = jnp.concatenate([v_c.reshape(b, n_ctx, bw), v.reshape(b, n_lat, bw)], axis=1)
        nk = n_ctx + n_lat
        ck = _pick(nk, (1408, 768, 384, 256, 128))
        att = diff_attention(q.reshape(b, n_lat, bw), k_all, v_all, lam, subln[l], 1.0 - lam_init,
                             tq=_pick(n_lat, (1024, 512, 256, 128)), ck=ck).reshape(m_lat, bw)
        z_lat = matmul(p, w_cdft, x_col=4, tm=_pick(n_lat, (1024, 512, 256)))
        four = fourier_seq_two_stage(z_lat, b, n_lat, bw, n1, n2)
        h_new = mixer_output(h, n_lat, lat_row, xm, p, att, four)

        if not last:
            att_c = diff_attention(q_c.reshape(b, n_ctx, bw), k_c.reshape(b, n_ctx, bw), v_c.reshape(b, n_ctx, bw),
                                   lam, subln[l], 1.0 - lam_init, tq=_pick(n_ctx, (256, 128)),
                                   ck=_pick(n_ctx, (256, 128))).reshape(m_ctx, bw)
            z_ctx = matmul(pc, w_cdft, x_col=4, tm=_pick(n_ctx, (256, 128)))
            four_c = fourier_seq_dense(z_ctx, b, n_ctx, bw)
            hc = mixer_output(hc, n_ctx, ctx_row, xcm, pc, att_c, four_c)
        h = h_new

        if l % 2 == 0:
            w1 = ffn_w1[l // 2].astype(BF16)[None]
            w3 = ffn_w3[l // 2].astype(BF16)[None]
            w2 = _pad_rows(ffn_w2[l // 2].astype(BF16), 1024)[None]
            rt = None
            f_pad = w2.shape[1]
        else:
            w1 = w3 = w2 = None
            rt = router[l // 2]
            f_pad = -(-moe_w2.shape[2] // 256) * 256
        tn_up = _pick(f_pad, (1024, 512, 256))
        tk_dn = _pick(f_pad, (2816, 1024, 512, 256))

        def channel_mix(h2, n_seq, row_fn):
            tm_norm = _pick(n_seq, (512, 256, 128))
            zn = mod_norm(h2, norm_ffn[l], mod3, row_fn(tm_norm), 3, 4, tm_norm, router=rt)
            z_, comb = zn if rt is not None else (zn, None)
            tm = mm_rows(h2, n_seq, row_fn)
            if rt is not None and (TOP_K * h2.shape[0]) % MOE_TILE == 0:
                w1t = jnp.swapaxes(moe_w1[l // 2], 1, 2)
                w3t = jnp.swapaxes(moe_w3[l // 2], 1, 2)
                return moe_sparse(z_, comb, rt.shape[1], w1t, w3t, moe_w2[l // 2], f_pad, h2, mod3,
                                  row_fn(tm_norm), 5, tm_norm)
            if rt is not None:
                z_ = z_.reshape(h2.shape)
                w1_, w3_ = (_pad_last(w[l // 2].astype(BF16), 256) for w in (moe_w1, moe_w3))
                w2_ = _pad_rows(moe_w2[l // 2].astype(BF16), 256)
            else:
                w1_, w3_, w2_ = w1, w3, w2
            hid = swiglu_up(z_, w1_, w3_, tm=tm, tn=tn_up, comb=comb, f_out=f_pad)
            return matmul_residual(hid, w2_, h2, mod3, row_fn(tm), 5, tm=tm, tn=1024, tk=tk_dn)

        h = channel_mix(h, n_lat, lat_row)
        if not last:
            hc = channel_mix(hc, n_ctx, ctx_row)

    return h.reshape(b, n_lat, d)
```

```python
import functools
import math

import jax
import jax.numpy as jnp
from jax import lax
from jax.experimental import pallas as pl
from jax.experimental.pallas import tpu as pltpu

F32 = jnp.float32
BF16 = jnp.bfloat16

GRID_W = 64
ROPE_THETA = 10000.0
POOL_WINDOWS = (2, 4, 8, 16)
FOURIER_GROUPS = 4
CONV_K = 3
N_MOD = 6
EPS = 1e-6
LANES = 128
HALO = 16
VMEM_LIMIT = 56 * 1024 * 1024
HI = lax.Precision.HIGHEST
LOG2_E = 1.4426950408889634
SUB_N = 256


def _cp(*sem, vmem=VMEM_LIMIT):
    return pltpu.CompilerParams(dimension_semantics=sem, vmem_limit_bytes=vmem)


def _pick(n, prefs):
    for t in prefs:
        if n % t == 0:
            return t
    return n


def _sigmoid(x):
    return 1.0 / (1.0 + jnp.exp(-x))


def _mod_kernel(c_ref, w_ref, b_ref, o_ref):
    x = c_ref[...]
    s = x * _sigmoid(x)
    acc = jnp.dot(s.astype(BF16), w_ref[0].astype(BF16), preferred_element_type=F32)
    o_ref[0] = acc + b_ref[0]


def modulation_all(cond8, w_mod, b_mod):
    depth, d, cols = w_mod.shape
    tn = _pick(cols, (1024, 512, 256, 128))
    return pl.pallas_call(
        _mod_kernel,
        grid=(depth, cols // tn),
        in_specs=[
            pl.BlockSpec((8, d), lambda l, j: (0, 0)),
            pl.BlockSpec((1, d, tn), lambda l, j: (l, 0, j)),
            pl.BlockSpec((1, 1, tn), lambda l, j: (l, 0, j)),
        ],
        out_specs=pl.BlockSpec((1, 8, tn), lambda l, j: (l, 0, j)),
        out_shape=jax.ShapeDtypeStruct((depth, 8, cols), F32),
        compiler_params=_cp("parallel", "parallel"),
        name="modulation",
    )(cond8, w_mod, b_mod.reshape(depth, 1, cols))


def _norm_body(h_ref, g_ref, sh_ref, sc_ref):
    x = h_ref[...]
    ms = jnp.mean(x * x, axis=-1, keepdims=True)
    y = x * lax.rsqrt(ms + EPS) * g_ref[...]
    return y * (1.0 + sc_ref[0]) + sh_ref[0]


def _norm_kernel(h_ref, g_ref, sh_ref, sc_ref, o_ref):
    o_ref[...] = _norm_body(h_ref, g_ref, sh_ref, sc_ref).astype(o_ref.dtype)


def _norm_route_kernel(h_ref, g_ref, sh_ref, sc_ref, r_ref, o_ref, comb_ref, *, n_experts):
    z = _norm_body(h_ref, g_ref, sh_ref, sc_ref)
    o_ref[...] = z.astype(o_ref.dtype).reshape(o_ref.shape)
    logits = jnp.dot(z, r_ref[...], preferred_element_type=F32, precision=HI)
    lane = lax.broadcasted_iota(jnp.int32, logits.shape, 1).astype(F32)
    neg = jnp.float32(-jnp.inf)
    lg = jnp.where(lane < n_experts, logits, neg)
    m1 = jnp.max(lg, axis=-1, keepdims=True)
    i1 = jnp.min(jnp.where(lg == m1, lane, float(LANES)), axis=-1, keepdims=True)
    lg2 = jnp.where(lane == i1, neg, lg)
    m2 = jnp.max(lg2, axis=-1, keepdims=True)
    i2 = jnp.min(jnp.where(lg2 == m2, lane, float(LANES)), axis=-1, keepdims=True)
    e = jnp.exp(m2 - m1)
    w1 = 1.0 / (1.0 + e)
    w2 = e / (1.0 + e)
    comb_ref[...] = jnp.where(lane == i1, w1, 0.0) + jnp.where(lane == i2, w2, 0.0)


def mod_norm(h, g, mod3, row_of_tile, sh_chunk, sc_chunk, tm, router=None):
    m, d = h.shape
    in_specs = [
        pl.BlockSpec((tm, d), lambda i: (i, 0)),
        pl.BlockSpec((1, d), lambda i: (0, 0)),
        pl.BlockSpec((1, 1, d), lambda i: (row_of_tile(i), 0, sh_chunk)),
        pl.BlockSpec((1, 1, d), lambda i: (row_of_tile(i), 0, sc_chunk)),
    ]
    args = [h, g.reshape(1, d), mod3, mod3]
    if router is None:
        return pl.pallas_call(
            _norm_kernel,
            grid=(m // tm,),
            in_specs=in_specs,
            out_specs=pl.BlockSpec((tm, d), lambda i: (i, 0)),
            out_shape=jax.ShapeDtypeStruct((m, d), BF16),
            compiler_params=_cp("parallel"),
            name="mod_norm",
        )(*args)
    n_experts = router.shape[1]
    rpad = jnp.pad(router, ((0, 0), (0, LANES - n_experts)))
    in_specs.append(pl.BlockSpec((d, LANES), lambda i: (0, 0)))
    return pl.pallas_call(
        functools.partial(_norm_route_kernel, n_experts=n_experts),
        grid=(m // tm,),
        in_specs=in_specs,
        out_specs=[pl.BlockSpec((tm, d // LANES, LANES), lambda i: (i, 0, 0)),
                   pl.BlockSpec((tm, LANES), lambda i: (i, 0))],
        out_shape=[jax.ShapeDtypeStruct((m, d // LANES, LANES), BF16), jax.ShapeDtypeStruct((m, LANES), F32)],
        compiler_params=_cp("parallel"),
        name="mod_norm_route",
    )(*args, rpad)


def _mm_kernel(x_ref, w_ref, o_ref):
    o_ref[...] = jnp.dot(x_ref[...], w_ref[0], preferred_element_type=F32).astype(o_ref.dtype)


def _mm_batched_kernel(x_ref, w_ref, o_ref):
    o_ref[0] = jnp.dot(x_ref[...], w_ref[0], preferred_element_type=F32).astype(o_ref.dtype)


def matmul(x, w, out_dtype=BF16, x_col=0, tm=None, tn=None, layer=0):
    m = x.shape[0]
    if w.ndim == 2:
        w = w[None]
    _, k, n = w.shape
    tm = tm or _pick(m, (1024, 512, 256, 128))
    tn = tn or _pick(n, (1024, 512, 256, 128))
    return pl.pallas_call(
        _mm_kernel,
        grid=(m // tm, n // tn),
        in_specs=[
            pl.BlockSpec((tm, k), lambda i, j: (i, x_col)),
            pl.BlockSpec((1, k, tn), lambda i, j: (layer, 0, j)),
        ],
        out_specs=pl.BlockSpec((tm, tn), lambda i, j: (i, j)),
        out_shape=jax.ShapeDtypeStruct((m, n), out_dtype),
        compiler_params=_cp("parallel", "parallel"),
        name="matmul",
    )(x, w)


def _mm_res_kernel(x_ref, w_ref, res_ref, gate_ref, o_ref, *maybe_acc, nk):
    acc_ref = maybe_acc[0] if nk > 1 else None
    k = pl.program_id(2)
    sub = _pick(o_ref.shape[1], (SUB_N, LANES))
    if nk > 1:

        @pl.when(k == 0)
        def _():
            acc_ref[...] = jnp.zeros(acc_ref.shape, F32)

    for s in range(o_ref.shape[1] // sub):
        sl = slice(s * sub, (s + 1) * sub)
        part = jnp.dot(x_ref[...], w_ref[0, :, sl], preferred_element_type=F32)
        if nk > 1:
            part = acc_ref[:, sl] + part
            acc_ref[:, sl] = part
        o_ref[:, sl] = res_ref[:, sl] + gate_ref[0, :, sl] * part


def matmul_residual(x, w3, res, mod3, row_of_tile, gate_chunk, tm, tn, tk, first=0, count=None):
    m, n = res.shape
    kf = w3.shape[1]
    n_e = w3.shape[0] - first if count is None else count
    kpe = kf // tk
    nk = n_e * kpe
    return pl.pallas_call(
        functools.partial(_mm_res_kernel, nk=nk),
        grid=(m // tm, n // tn, nk),
        in_specs=[
            pl.BlockSpec((tm, tk), lambda i, j, k: (i, k)),
            pl.BlockSpec((1, tk, tn), lambda i, j, k: (first + k // kpe, k % kpe, j)),
            pl.BlockSpec((tm, tn), lambda i, j, k: (i, j)),
            pl.BlockSpec((1, 1, tn), lambda i, j, k: (row_of_tile(i), 0, gate_chunk * (n // tn) + j)),
        ],
        out_specs=pl.BlockSpec((tm, tn), lambda i, j, k: (i, j)),
        out_shape=jax.ShapeDtypeStruct((m, n), F32),
        scratch_shapes=[pltpu.VMEM((tm, tn), F32)] if nk > 1 else [],
        compiler_params=_cp("parallel", "parallel", "arbitrary"),
        name="matmul_residual",
    )(x, w3, res, mod3)


def _swiglu_kernel(z_ref, w1_ref, w3_ref, o_ref, *, f_valid, blocks_per_expert):
    tn = o_ref.shape[1]
    sub = _pick(tn, (SUB_N, LANES))
    col0 = (pl.program_id(1) % blocks_per_expert) * tn
    for s in range(tn // sub):
        sl = slice(s * sub, (s + 1) * sub)
        a = jnp.dot(z_ref[...], w1_ref[0, :, sl], preferred_element_type=F32)
        b = jnp.dot(z_ref[...], w3_ref[0, :, sl], preferred_element_type=F32)
        col = col0 + s * sub + lax.broadcasted_iota(jnp.int32, a.shape, 1)
        o_ref[:, sl] = jnp.where(col < f_valid, a * _sigmoid(a) * b, 0.0).astype(o_ref.dtype)


def _swiglu_comb_kernel(z_ref, w1_ref, w3_ref, comb_ref, o_ref, *, blocks_per_expert):
    z = z_ref[...]
    a = jnp.dot(z, w1_ref[0], preferred_element_type=F32)
    b = jnp.dot(z, w3_ref[0], preferred_element_type=F32)
    e = pl.program_id(1) // blocks_per_expert
    comb = comb_ref[...]
    lane = lax.broadcasted_iota(jnp.int32, comb.shape, 1)
    scale = jnp.sum(jnp.where(lane == e, comb, 0.0), axis=-1, keepdims=True)
    o_ref[...] = (a * _sigmoid(a) * b * scale).astype(o_ref.dtype)


def swiglu_up(z, w1, w3, tm, tn, comb=None, f_out=None):
    m, k = z.shape
    n_e, _, f_valid = w1.shape
    f = f_valid if f_out is None else f_out
    bpe = f // tn
    last_blk = (f_valid - 1) // tn

    def w_map(i, j):
        return (j // bpe, 0, jnp.minimum(j % bpe, last_blk))

    in_specs = [
        pl.BlockSpec((tm, k), lambda i, j: (i, 0)),
        pl.BlockSpec((1, k, tn), w_map),
        pl.BlockSpec((1, k, tn), w_map),
    ]
    args = [z, w1, w3]
    if comb is None:
        kern = functools.partial(_swiglu_kernel, f_valid=f_valid, blocks_per_expert=bpe)
    else:
        kern = functools.partial(_swiglu_comb_kernel, blocks_per_expert=bpe)
        in_specs.append(pl.BlockSpec((tm, LANES), lambda i, j: (i, 0)))
        args.append(comb)
    return pl.pallas_call(
        kern,
        grid=(m // tm, n_e * bpe),
        in_specs=in_specs,
        out_specs=pl.BlockSpec((tm, tn), lambda i, j: (i, j)),
        out_shape=jax.ShapeDtypeStruct((m, n_e * f), BF16),
        compiler_params=_cp("parallel", "parallel"),
        name="swiglu_up",
    )(*args)


MOE_TILE = 1024
TOP_K = 2
TOKEN_CHUNK = 128


def _gather_kernel(idx_ref, src_ref, o_ref, sem, *, rows):
    base = pl.program_id(0) * rows

    def issue(pair, carry):
        for prio in range(2):
            r = pair * 2 + prio
            pltpu.make_async_copy(src_ref.at[idx_ref[base + r]], o_ref.at[r], sem).start(priority=prio)
        return carry

    lax.fori_loop(0, rows // 2, issue, 0)
    pltpu.make_async_copy(src_ref.at[pl.ds(0, rows)], o_ref, sem).wait()


def gather_rows(src3, idx, rows_per_step):
    n_src, chunks, _ = src3.shape
    n_out = idx.shape[0]
    return pl.pallas_call(
        functools.partial(_gather_kernel, rows=rows_per_step),
        grid_spec=pltpu.PrefetchScalarGridSpec(
            num_scalar_prefetch=1,
            grid=(n_out // rows_per_step,),
            in_specs=[pl.BlockSpec(memory_space=pl.ANY)],
            out_specs=pl.BlockSpec((rows_per_step, chunks, TOKEN_CHUNK), lambda i, idx_ref: (i, 0, 0)),
            scratch_shapes=[pltpu.SemaphoreType.DMA(())],
        ),
        out_shape=jax.ShapeDtypeStruct((n_out, chunks, TOKEN_CHUNK), src3.dtype),
        compiler_params=_cp("arbitrary"),
        name="gather_rows",
    )(idx, src3)


def _moe_up_kernel(te_ref, nu_ref, z3_ref, w1_ref, w3_ref, ws_ref, o_ref, z_ref, *, f_valid):
    used = pl.program_id(0) < nu_ref[0]
    tn = o_ref.shape[1]
    nt = (((1,), (1,)), ((), ()))

    @pl.when(jnp.logical_and(used, pl.program_id(1) == 0))
    def _():
        z_ref[...] = z3_ref[...].reshape(z_ref.shape)

    @pl.when(used)
    def _():
        z = z_ref[...]
        a = lax.dot_general(z, w1_ref[0].astype(BF16), nt, preferred_element_type=F32)
        b = lax.dot_general(z, w3_ref[0].astype(BF16), nt, preferred_element_type=F32)
        scale = jnp.concatenate([ws_ref[...]] * (tn // LANES), axis=1)
        col = pl.program_id(1) * tn + lax.broadcasted_iota(jnp.int32, a.shape, 1)
        val = jnp.where(col < f_valid, a * _sigmoid(a) * b * scale, 0.0)
        o_ref[...] = val.astype(o_ref.dtype)

    @pl.when(jnp.logical_not(used))
    def _():
        o_ref[...] = jnp.zeros(o_ref.shape, o_ref.dtype)


def _moe_down_kernel(te_ref, nu_ref, x_ref, w_ref, o_ref, *, f_valid):
    used = pl.program_id(0) < nu_ref[0]

    @pl.when(used)
    def _():
        rows, chunks, _ = o_ref.shape
        sub = _pick(chunks * LANES, (SUB_N, LANES))
        for s in range(chunks * LANES // sub):
            w = w_ref[0, :, s * sub:(s + 1) * sub].astype(BF16)
            if f_valid < w.shape[0]:
                row = lax.broadcasted_iota(jnp.int32, w.shape, 0)
                w = jnp.where(row < f_valid, w, jnp.zeros_like(w))
            part = jnp.dot(x_ref[...], w, preferred_element_type=F32)
            c0 = s * sub // LANES
            o_ref[:, c0:c0 + sub // LANES, :] = part.astype(o_ref.dtype).reshape(rows, sub // LANES, LANES)

    @pl.when(jnp.logical_not(used))
    def _():
        o_ref[...] = jnp.zeros(o_ref.shape, o_ref.dtype)


def _combine_kernel(h_ref, ya_ref, yb_ref, gate_ref, o_ref):
    y = ya_ref[0].astype(F32) + yb_ref[0].astype(F32)
    o_ref[...] = h_ref[...] + gate_ref[0] * y.reshape(o_ref.shape)


def moe_sparse(z3, comb, n_experts, w1t, w3t, w2, f, h2, mod3, row_of_tile, gate_chunk, tm):
    m, d = h2.shape
    f_valid = w1t.shape[1]
    n_tiles = (TOP_K * m) // MOE_TILE + n_experts
    n_rows = n_tiles * MOE_TILE
    gather_step = _pick(m, (512, 256, 128))

    cw = comb[:, :n_experts]
    sel = cw > 0.0
    seli = sel.astype(jnp.int32)
    rank = jnp.cumsum(seli, axis=0) - seli
    cnt = jnp.sum(seli, axis=0)
    padded = ((cnt + MOE_TILE - 1) // MOE_TILE) * MOE_TILE
    gend = jnp.cumsum(padded)
    slot = (gend - padded)[None, :] + rank
    top_w, top_e = lax.top_k(cw, TOP_K)
    slot_ab = jnp.take_along_axis(jnp.where(sel, slot, n_rows - 1), top_e, axis=1).astype(jnp.int32)
    pair_id = jnp.arange(m * TOP_K, dtype=jnp.int32)
    row_pair = jnp.full((n_rows,), -1, jnp.int32).at[slot_ab.reshape(-1)].set(pair_id)
    row_valid = row_pair >= 0
    src_row = jnp.where(row_valid, row_pair // TOP_K, jnp.arange(n_rows, dtype=jnp.int32) % m)
    w_slot = jnp.where(row_valid, top_w.reshape(-1)[jnp.maximum(row_pair, 0)], 0.0)
    w_slot = jnp.broadcast_to(w_slot[:, None], (n_rows, LANES))
    tile_start = jnp.arange(n_tiles, dtype=jnp.int32) * MOE_TILE
    tile_expert = jnp.minimum(jnp.sum((tile_start[:, None] >= gend[None, :]).astype(jnp.int32), axis=1),
                              n_experts - 1).astype(jnp.int32)
    n_used = (gend[-1:] // MOE_TILE).astype(jnp.int32)

    chunks = d // TOKEN_CHUNK
    zs = gather_rows(z3, src_row, gather_step)
    tn_up = _pick(f, (256, 128))
    hid = pl.pallas_call(
        functools.partial(_moe_up_kernel, f_valid=f_valid),
        grid_spec=pltpu.PrefetchScalarGridSpec(
            num_scalar_prefetch=2,
            grid=(n_tiles, f // tn_up),
            in_specs=[
                pl.BlockSpec((MOE_TILE, chunks, TOKEN_CHUNK), lambda i, j, te, nu: (i, 0, 0)),
                pl.BlockSpec((1, tn_up, d), lambda i, j, te, nu: (te[i], j, 0)),
                pl.BlockSpec((1, tn_up, d), lambda i, j, te, nu: (te[i], j, 0)),
                pl.BlockSpec((MOE_TILE, LANES), lambda i, j, te, nu: (i, 0)),
            ],
            out_specs=pl.BlockSpec((MOE_TILE, tn_up), lambda i, j, te, nu: (i, j)),
            scratch_shapes=[pltpu.VMEM((MOE_TILE, d), BF16)],
        ),
        out_shape=jax.ShapeDtypeStruct((n_rows, f), BF16),
        compiler_params=_cp("parallel", "arbitrary"),
        name="moe_up",
    )(tile_expert, n_used, zs, w1t, w3t, w_slot)
    tn_dn = _pick(d, (1024, 512, 256))
    ys = pl.pallas_call(
        functools.partial(_moe_down_kernel, f_valid=f_valid),
        grid_spec=pltpu.PrefetchScalarGridSpec(
            num_scalar_prefetch=2,
            grid=(n_tiles, d // tn_dn),
            in_specs=[
                pl.BlockSpec((MOE_TILE, f), lambda i, j, te, nu: (i, 0)),
                pl.BlockSpec((1, f, tn_dn), lambda i, j, te, nu: (te[i], 0, j)),
            ],
            out_specs=pl.BlockSpec((MOE_TILE, tn_dn // TOKEN_CHUNK, TOKEN_CHUNK), lambda i, j, te, nu: (i, j, 0)),
        ),
        out_shape=jax.ShapeDtypeStruct((n_rows, chunks, TOKEN_CHUNK), BF16),
        compiler_params=_cp("parallel", "parallel"),
        name="moe_down",
    )(tile_expert, n_used, hid, w2)
    yab = gather_rows(ys, slot_ab.T.reshape(-1), gather_step).reshape(TOP_K, m, chunks, TOKEN_CHUNK)
    return pl.pallas_call(
        _combine_kernel,
        grid=(m // tm,),
        in_specs=[
            pl.BlockSpec((tm, d), lambda i: (i, 0)),
            pl.BlockSpec((1, tm, chunks, TOKEN_CHUNK), lambda i: (0, i, 0, 0)),
            pl.BlockSpec((1, tm, chunks, TOKEN_CHUNK), lambda i: (1, i, 0, 0)),
            pl.BlockSpec((1, 1, d), lambda i: (row_of_tile(i), 0, gate_chunk)),
        ],
        out_specs=pl.BlockSpec((tm, d), lambda i: (i, 0)),
        out_shape=jax.ShapeDtypeStruct((m, d), F32),
        compiler_params=_cp("parallel"),
        name="moe_combine",
    )(h2, yab, yab, mod3)


def _merge_kernel(xm_ref, wg_ref, *rest, n_branch):
    y_refs = rest[:n_branch]
    wb_ref, o_ref, acc_ref, y_ref = rest[n_branch:]
    b = pl.program_id(2)

    @pl.when(b == 0)
    def _():
        acc_ref[...] = jnp.zeros(acc_ref.shape, F32)

    for bi in range(n_branch):

        @pl.when(b == bi)
        def _(bi=bi):
            y_ref[...] = y_refs[bi][...]

    sub = _pick(o_ref.shape[1], (SUB_N, LANES))
    for s in range(o_ref.shape[1] // sub):
        sl = slice(s * sub, (s + 1) * sub)
        gate = _sigmoid(jnp.dot(xm_ref[...], wg_ref[0, :, sl], preferred_element_type=F32))
        new = acc_ref[:, sl] + gate * jnp.dot(y_ref[...], wb_ref[0, 0, :, sl], preferred_element_type=F32)
        acc_ref[:, sl] = new
        o_ref[:, sl] = new.astype(o_ref.dtype)


def merge_branches(xm, w_gate, ys, w_branch, tm, tn, layer):
    m, d = xm.shape
    _, n_branch, bw, _ = w_branch.shape
    nj = d // tn
    return pl.pallas_call(
        functools.partial(_merge_kernel, n_branch=n_branch),
        grid=(m // tm, nj, n_branch),
        in_specs=[
            pl.BlockSpec((tm, d), lambda i, j, b: (i, 0)),
            pl.BlockSpec((1, d, tn), lambda i, j, b: (layer, 0, b * nj + j)),
        ] + [pl.BlockSpec((tm, bw), lambda i, j, b: (i, 0))] * n_branch + [
            pl.BlockSpec((1, 1, bw, tn), lambda i, j, b: (layer, b, 0, j)),
        ],
        out_specs=pl.BlockSpec((tm, tn), lambda i, j, b: (i, j)),
        out_shape=jax.ShapeDtypeStruct((m, d), BF16),
        scratch_shapes=[pltpu.VMEM((tm, tn), F32), pltpu.VMEM((tm, bw), BF16)],
        compiler_params=_cp("parallel", "parallel", "arbitrary"),
        name="merge_branches",
    )(xm, w_gate, *ys, w_branch)


def _qk_prep_kernel(pq_ref, pk_ref, gq_ref, gk_ref, cos_ref, sin_ref, grp_ref, qo_ref, ko_ref, *, rope, q_scale,
                    inv_dh):
    tm = pq_ref.shape[0]
    lane = lax.broadcasted_iota(jnp.int32, (tm, LANES), 1)
    first_half = (lane & 16) == 0
    for src, g_ref, dst, scale in ((pq_ref, gq_ref, qo_ref, q_scale), (pk_ref, gk_ref, ko_ref, 1.0)):
        for j in range(src.shape[1] // LANES):
            sl = slice(j * LANES, (j + 1) * LANES)
            x = src[:, sl].astype(F32)
            sq = x * x
            sq_hi = sq.astype(BF16)
            sq_lo = (sq - sq_hi.astype(F32)).astype(BF16)
            ms = (jnp.dot(sq_hi, grp_ref[...], preferred_element_type=F32)
                  + jnp.dot(sq_lo, grp_ref[...], preferred_element_type=F32)) * inv_dh
            y = x * lax.rsqrt(ms + EPS) * g_ref[...]
            if rope:
                partner = jnp.where(first_half, pltpu.roll(y, LANES - 16, 1), pltpu.roll(y, 16, 1))
                y = y * cos_ref[...] + partner * sin_ref[...]
            dst[:, sl] = (y * scale).astype(dst.dtype)


def qk_prep(p, n_seq, bw, gq, gk, cos_t, sin_t, rope, q_scale, tm):
    m = p.shape[0]
    dh = gq.shape[0]
    reps = LANES // dh
    idx = jnp.arange(LANES)
    grp = jnp.where((idx[:, None] // dh) == (idx[None, :] // dh), 1.0, 0.0).astype(BF16)
    tiles_per_seq = n_seq // tm
    return pl.pallas_call(
        functools.partial(_qk_prep_kernel, rope=rope, q_scale=q_scale, inv_dh=1.0 / dh),
        grid=(m // tm,),
        in_specs=[
            pl.BlockSpec((tm, bw), lambda i: (i, 0)),
            pl.BlockSpec((tm, bw), lambda i: (i, 1)),
            pl.BlockSpec((1, LANES), lambda i: (0, 0)),
            pl.BlockSpec((1, LANES), lambda i: (0, 0)),
            pl.BlockSpec((tm, LANES), lambda i: (i % tiles_per_seq, 0)),
            pl.BlockSpec((tm, LANES), lambda i: (i % tiles_per_seq, 0)),
            pl.BlockSpec((LANES, LANES), lambda i: (0, 0)),
        ],
        out_specs=[pl.BlockSpec((tm, bw), lambda i: (i, 0)), pl.BlockSpec((tm, bw), lambda i: (i, 0))],
        out_shape=[jax.ShapeDtypeStruct((m, bw), BF16), jax.ShapeDtypeStruct((m, bw), BF16)],
        compiler_params=_cp("parallel"),
        name="qk_prep",
    )(p, p, jnp.tile(gq, reps).reshape(1, LANES), jnp.tile(gk, reps).reshape(1, LANES), cos_t, sin_t, grp)


def rope_tables(n_seq, dh):
    nf = dh // 4
    pos = jnp.arange(n_seq)
    row = (pos // GRID_W).astype(F32)
    col = (pos % GRID_W).astype(F32)
    freqs = ROPE_THETA ** (-jnp.arange(nf, dtype=F32) / nf)
    lane = jnp.arange(LANES)
    d = lane % dh
    use_col = (d // (2 * nf)) == 1
    second = ((d // nf) % 2) == 1
    f = freqs[d % nf]
    ang = jnp.where(use_col[None, :], col[:, None], row[:, None]) * f[None, :]
    return jnp.cos(ang), jnp.where(second[None, :], 1.0, -1.0) * jnp.sin(ang)


def _att_kernel(lam_ref, q_ref, kt_ref, v_ref, g_ref, o_ref, m_ref, acc_ref, s_ref, mc_ref, *, n_chunks, ck,
                out_scale):
    tq = q_ref.shape[1]
    half = LANES // 2
    n_tiles = ck // LANES
    q = q_ref[0]
    lane = lax.broadcasted_iota(jnp.int32, (tq, LANES), 1)
    zero = jnp.zeros_like(q)
    qs = (jnp.where(lane < half, q, zero), jnp.where(lane >= half, q, zero))
    m_ref[...] = jnp.full(m_ref.shape, -jnp.inf, F32)
    acc_ref[...] = jnp.zeros(acc_ref.shape, F32)
    ones = jnp.ones((ck, LANES), BF16)

    def scores(c, slot):
        kt = kt_ref[0, 0, c]
        for ci in range(2):
            s = jnp.dot(qs[ci], kt, preferred_element_type=F32)
            s_ref[slot, ci] = s
            mc = s[:, :LANES]
            for j in range(1, n_tiles):
                mc = jnp.maximum(mc, s[:, j * LANES:(j + 1) * LANES])
            mc_ref[slot, ci] = mc

    def softmax_pv(c, slot):
        v = jnp.concatenate([v_ref[0, pl.ds(pl.multiple_of(c * ck, ck), ck), :], ones], axis=1)
        for ci in range(2):
            m_old = m_ref[ci]
            m_new = jnp.maximum(m_old, jnp.max(mc_ref[slot, ci], axis=-1, keepdims=True))
            alpha = jnp.exp2(m_old - m_new)
            p = jnp.concatenate(
                [jnp.exp2(s_ref[slot, ci, :, j * LANES:(j + 1) * LANES] - m_new).astype(BF16)
                 for j in range(n_tiles)], axis=1)
            pv = jnp.dot(p, v, preferred_element_type=F32)
            acc_ref[ci] = jnp.concatenate([alpha, alpha], axis=1) * acc_ref[ci] + pv
            m_ref[ci] = m_new

    scores(0, 0)
    n_pairs = (n_chunks - 1) // 2

    def body(i, carry):
        c = 2 * i
        scores(c + 1, 1)
        softmax_pv(c, 0)
        scores(c + 2, 0)
        softmax_pv(c + 1, 1)
        return carry

    if n_pairs > 0:
        lax.fori_loop(0, n_pairs, body, 0)
    if (n_chunks - 1) % 2 == 1:
        scores(n_chunks - 1, 1)
        softmax_pv(n_chunks - 2, 0)
        softmax_pv(n_chunks - 1, 1)
    else:
        softmax_pv(n_chunks - 1, 0)

    a1 = acc_ref[0]
    a2 = acc_ref[1]
    o = a1[:, :LANES] / a1[:, LANES:] - lam_ref[0, 0] * (a2[:, :LANES] / a2[:, LANES:])
    ms = jnp.mean(o * o, axis=-1, keepdims=True)
    o_ref[0] = (o * lax.rsqrt(ms + EPS) * g_ref[...] * out_scale).astype(o_ref.dtype)


def diff_attention(q, k, v, lam, g_sub, out_scale, tq, ck):
    b, nq, bw = q.shape
    nk = k.shape[1]
    heads = bw // LANES
    n_chunks = nk // ck
    kt = k.reshape(b, n_chunks, ck, heads, LANES).transpose(0, 3, 1, 4, 2)
    return pl.pallas_call(
        functools.partial(_att_kernel, n_chunks=n_chunks, ck=ck, out_scale=out_scale),
        grid=(b, heads, nq // tq),
        in_specs=[
            pl.BlockSpec(memory_space=pltpu.SMEM),
            pl.BlockSpec((1, tq, LANES), lambda bi, h, i: (bi, i, h)),
            pl.BlockSpec((1, 1, n_chunks, LANES, ck), lambda bi, h, i: (bi, h, 0, 0, 0)),
            pl.BlockSpec((1, nk, LANES), lambda bi, h, i: (bi, 0, h)),
            pl.BlockSpec((1, LANES), lambda bi, h, i: (0, 0)),
        ],
        out_specs=pl.BlockSpec((1, tq, LANES), lambda bi, h, i: (bi, i, h)),
        out_shape=jax.ShapeDtypeStruct((b, nq, bw), BF16),
        scratch_shapes=[
            pltpu.VMEM((2, tq, LANES), F32),
            pltpu.VMEM((2, tq, 2 * LANES), F32),
            pltpu.VMEM((2, 2, tq, ck), F32),
            pltpu.VMEM((2, 2, tq, LANES), F32),
        ],
        compiler_params=_cp("parallel", "parallel", "parallel"),
        name="diff_attention",
    )(lam.reshape(1, 1).astype(F32), q, kt, v, g_sub.reshape(1, LANES))


def _local_kernel(pm_ref, pp_ref, pn_ref, cb_ref, cc_ref, ch_ref, ccp_ref, chp_ref, ccn_ref, chn_ref,
                  pw_ref, ps_ref, cw_ref, yp_ref, yc_ref, scr_ref, *, n_seq, tiles_per_seq):
    ts, bw = pm_ref.shape
    it = pl.program_id(0) % tiles_per_seq
    has_prev = jnp.where(it > 0, 1.0, 0.0).astype(F32)
    has_next = jnp.where(it < tiles_per_seq - 1, 1.0, 0.0).astype(F32)
    gw = bw // len(POOL_WINDOWS)
    pos = it * ts + lax.broadcasted_iota(jnp.int32, (ts, 1), 0)

    scr_ref[0:HALO, :] = pp_ref[...].astype(F32) * has_prev
    scr_ref[HALO:HALO + ts, :] = pm_ref[...].astype(F32)
    scr_ref[HALO + ts:2 * HALO + ts, :] = pn_ref[...].astype(F32) * has_next
    for g, w in enumerate(POOL_WINDOWS):
        sl = slice(g * gw, (g + 1) * gw)
        lo, hi = w // 2, w - 1 - w // 2
        tot = None
        for off in range(-lo, hi + 1):
            part = scr_ref[HALO + off:HALO + off + ts, sl]
            tot = part if tot is None else tot + part
        cnt = (jnp.minimum(pos + hi, n_seq - 1) - jnp.maximum(pos - lo, 0) + 1).astype(F32)
        pooled = tot / cnt - scr_ref[HALO:HALO + ts, sl]
        y = jnp.dot(pooled.astype(BF16), pw_ref[g], preferred_element_type=F32)
        yp_ref[:, sl] = (y * ps_ref[:, sl]).astype(yp_ref.dtype)

    scr_ref[0:HALO, :] = ccp_ref[...].astype(F32) * chp_ref[...].astype(F32) * has_prev
    scr_ref[HALO:HALO + ts, :] = cc_ref[...].astype(F32) * ch_ref[...].astype(F32)
    scr_ref[HALO + ts:2 * HALO + ts, :] = ccn_ref[...].astype(F32) * chn_ref[...].astype(F32) * has_next
    conv = None
    for j in range(CONV_K):
        off = j - CONV_K // 2
        term = cw_ref[j:j + 1, :] * scr_ref[HALO + off:HALO + off + ts, :]
        conv = term if conv is None else conv + term
    yc_ref[...] = (cb_ref[...].astype(F32) * conv).astype(yc_ref.dtype)


def local_mixers(p, n_seq, bw, pool_w, pool_scale, conv_w, ts):
    m = p.shape[0]
    tps = n_seq // ts
    r = ts // HALO
    last_halo = m // HALO - 1

    def main(cb):
        return pl.BlockSpec((ts, bw), lambda i: (i, cb))

    def prev(cb):
        return pl.BlockSpec((HALO, bw), lambda i: (jnp.maximum(i * r - 1, 0), cb))

    def nxt(cb):
        return pl.BlockSpec((HALO, bw), lambda i: (jnp.minimum((i + 1) * r, last_halo), cb))

    n_g = len(POOL_WINDOWS)
    return pl.pallas_call(
        functools.partial(_local_kernel, n_seq=n_seq, tiles_per_seq=tps),
        grid=(m // ts,),
        in_specs=[main(3), prev(3), nxt(3), main(5), main(6), main(7), prev(6), prev(7), nxt(6), nxt(7),
                  pl.BlockSpec((n_g, bw // n_g, bw // n_g), lambda i: (0, 0, 0)),
                  pl.BlockSpec((1, bw), lambda i: (0, 0)),
                  pl.BlockSpec((CONV_K, bw), lambda i: (0, 0))],
        out_specs=[pl.BlockSpec((ts, bw), lambda i: (i, 0)), pl.BlockSpec((ts, bw), lambda i: (i, 0))],
        out_shape=[jax.ShapeDtypeStruct((m, bw), BF16), jax.ShapeDtypeStruct((m, bw), BF16)],
        scratch_shapes=[pltpu.VMEM((ts + 2 * HALO, bw), F32)],
        compiler_params=_cp("parallel"),
        name="local_mixers",
    )(p, p, p, p, p, p, p, p, p, p, pool_w.astype(BF16), pool_scale.reshape(1, bw), conv_w)


def _dft_cos_sin(n, scale):
    k = jnp.arange(n, dtype=jnp.int32)
    ang = ((k[:, None] * k[None, :]) % n).astype(F32) * (2.0 * math.pi / n)
    return jnp.cos(ang) * scale, jnp.sin(ang) * scale


def channel_dft_matrix(bw):
    gw = bw // FOURIER_GROUPS
    c, s = _dft_cos_sin(gw, gw ** -0.5)
    eye = jnp.eye(FOURIER_GROUPS, dtype=F32)
    return jnp.concatenate([jnp.kron(eye, c), jnp.kron(eye, s)], axis=1).astype(BF16)


def _fft_stage2_kernel(p_ref, q_ref, cw_ref, sw_ref, f2_ref, o_ref, b_ref):
    n2 = p_ref.shape[2]
    bw = o_ref.shape[2]
    cw = cw_ref[0]
    sw = sw_ref[0]
    for j in range(bw // LANES):
        lo = slice(j * LANES, (j + 1) * LANES)
        hi = slice(bw + j * LANES, bw + (j + 1) * LANES)
        ar = p_ref[0, 0, :, lo].astype(F32) - q_ref[0, 0, :, hi].astype(F32)
        ai = -(p_ref[0, 0, :, hi].astype(F32) + q_ref[0, 0, :, lo].astype(F32))
        b_ref[0:n2, lo] = (ar * cw + ai * sw).astype(BF16)
        b_ref[n2:2 * n2, lo] = (ai * cw - ar * sw).astype(BF16)
    o_ref[0] = jnp.dot(f2_ref[...], b_ref[...], preferred_element_type=F32).astype(o_ref.dtype)


def fourier_seq_two_stage(z, b, n_seq, bw, n1, n2):
    c1, s1 = _dft_cos_sin(n1, n1 ** -0.5)
    f1 = jnp.concatenate([c1, s1], axis=0).astype(BF16)
    c2, s2 = _dft_cos_sin(n2, n2 ** -0.5)
    f2 = jnp.concatenate([c2, s2], axis=1).astype(BF16)
    k1 = jnp.arange(n1, dtype=jnp.int32)
    t2 = jnp.arange(n2, dtype=jnp.int32)
    ang = (k1[:, None] * t2[None, :]).astype(F32) * (2.0 * math.pi / n_seq)
    cw = jnp.broadcast_to(jnp.cos(ang)[:, :, None], (n1, n2, LANES))
    sw = jnp.broadcast_to(jnp.sin(ang)[:, :, None], (n1, n2, LANES))
    zb = z.reshape(b, n1, n2 * 2 * bw)
    tn1 = _pick(n2 * 2 * bw, (8192, 4096, 2048))
    a = pl.pallas_call(
        _mm_batched_kernel,
        grid=(b, (n2 * 2 * bw) // tn1),
        in_specs=[
            pl.BlockSpec((2 * n1, n1), lambda bi, j: (0, 0)),
            pl.BlockSpec((1, n1, tn1), lambda bi, j: (bi, 0, j)),
        ],
        out_specs=pl.BlockSpec((1, 2 * n1, tn1), lambda bi, j: (bi, 0, j)),
        out_shape=jax.ShapeDtypeStruct((b, 2 * n1, n2 * 2 * bw), BF16),
        compiler_params=_cp("parallel", "parallel"),
        name="fft_stage1",
    )(f1, zb)
    a = a.reshape(b, 2 * n1, n2, 2 * bw)
    out = pl.pallas_call(
        _fft_stage2_kernel,
        grid=(b, n1),
        in_specs=[
            pl.BlockSpec((1, 1, n2, 2 * bw), lambda bi, k: (bi, k, 0, 0)),
            pl.BlockSpec((1, 1, n2, 2 * bw), lambda bi, k: (bi, n1 + k, 0, 0)),
            pl.BlockSpec((1, n2, LANES), lambda bi, k: (k, 0, 0)),
            pl.BlockSpec((1, n2, LANES), lambda bi, k: (k, 0, 0)),
            pl.BlockSpec((n2, 2 * n2), lambda bi, k: (0, 0)),
        ],
        out_specs=pl.BlockSpec((1, n2, bw), lambda bi, k: (bi, 0, k)),
        out_shape=jax.ShapeDtypeStruct((b, n2, n1 * bw), BF16),
        scratch_shapes=[pltpu.VMEM((2 * n2, bw), BF16)],
        compiler_params=_cp("parallel", "parallel"),
        name="fft_stage2",
    )(a, a, cw, sw, f2)
    return out.reshape(b * n_seq, bw)


def fourier_seq_dense(z, b, n_seq, bw):
    c, s = _dft_cos_sin(n_seq, n_seq ** -0.5)
    f = jnp.concatenate([c, -s], axis=1).astype(BF16)
    zb = z.reshape(b, n_seq, 2 * bw)
    outs = [matmul(f, jnp.concatenate([zb[i, :, :bw], zb[i, :, bw:]], axis=0)) for i in range(b)]
    return jnp.concatenate(outs, axis=0)


def _fft_factors(n_seq):
    n2 = LANES
    n1 = n_seq // n2
    return n1, n2


def _pad_last(w, mult):
    pad = (-w.shape[-1]) % mult
    return jnp.pad(w, [(0, 0)] * (w.ndim - 1) + [(0, pad)]) if pad else w


def _pad_rows(w, mult):
    pad = (-w.shape[-2]) % mult
    return jnp.pad(w, [(0, 0)] * (w.ndim - 2) + [(0, pad), (0, 0)]) if pad else w


def kernel(x, c, ctx, c_ctx, w_mod, b_mod, norm_mix, norm_ffn, w_in, w_gate, q_norm, k_norm, lambda_q1, lambda_k1,
           lambda_q2, lambda_k2, subln, pool_w, pool_scale, conv_w, w_branch, w_out, ffn_w1, ffn_w3, ffn_w2,
           router, moe_w1, moe_w3, moe_w2):
    b, n_lat, d = x.shape
    n_ctx = ctx.shape[1]
    depth = w_mod.shape[0]
    bw = d // 4
    dh = q_norm.shape[1]
    m_lat = b * n_lat
    m_ctx = b * n_ctx

    cond8 = jnp.zeros((8, d), F32).at[:b].set(c).at[b].set(c_ctx)
    mod3 = modulation_all(cond8, w_mod, b_mod).reshape(depth * 8, 1, N_MOD * d)

    rope_lat = rope_tables(n_lat, dh)
    rope_ctx = rope_tables(n_ctx, dh)
    w_cdft = channel_dft_matrix(bw)
    n1, n2 = _fft_factors(n_lat)

    h = x.reshape(m_lat, d)
    hc = ctx.reshape(m_ctx, d)

    w_in_b = w_in.astype(BF16)
    w_gate_b = w_gate.astype(BF16)
    w_branch_b = w_branch.astype(BF16)
    w_out_b = w_out.astype(BF16)

    for l in range(depth):
        last = l == depth - 1
        lam_init = 0.8 - 0.6 * math.exp(-0.3 * l)
        lam = (jnp.exp(jnp.sum(lambda_q1[l] * lambda_k1[l])) - jnp.exp(jnp.sum(lambda_q2[l] * lambda_k2[l]))
               + lam_init)

        def lat_row(tm, l=l):
            return lambda i: l * 8 + i // (n_lat // tm)

        def ctx_row(tm, l=l):
            return lambda i: l * 8 + b

        def mm_rows(h2, n_seq, row_fn):
            return _pick(h2.shape[0] if row_fn is ctx_row else n_seq, (1024, 512, 256))

        def mixer_inputs(h2, n_seq, row_fn, rope, tables):
            tm_norm = _pick(n_seq, (512, 256, 128))
            xm_ = mod_norm(h2, norm_mix[l], mod3, row_fn(tm_norm), 0, 1, tm_norm)
            p_ = matmul(xm_, w_in_b, tm=mm_rows(h2, n_seq, row_fn), layer=l)
            q_, k_ = qk_prep(p_, n_seq, bw, q_norm[l], k_norm[l], tables[0], tables[1], rope=rope,
                             q_scale=LOG2_E * dh ** -0.5, tm=_pick(n_seq, (512, 256)))
            return xm_, p_, q_, k_, p_[:, 2 * bw:3 * bw]

        def mixer_output(h2, n_seq, row_fn, xm_, p_, att_, four_):
            pool_, conv_ = local_mixers(p_, n_seq, bw, pool_w[l], pool_scale[l], conv_w[l],
                                        ts=_pick(n_seq, (512, 256)))
            ys = (att_, pool_, four_, conv_)
            tm = mm_rows(h2, n_seq, row_fn)
            merged = merge_branches(xm_, w_gate_b, ys, w_branch_b, tm=tm, tn=512, layer=l)
            return matmul_residual(merged, w_out_b, h2, mod3, row_fn(tm), 2, tm=tm, tn=1024, tk=d, first=l, count=1)

        xcm, pc, q_c, k_c, v_c = mixer_inputs(hc, n_ctx, ctx_row, False, rope_ctx)
        xm, p, q, k, v = mixer_inputs(h, n_lat, lat_row, True, rope_lat)

        k_all = jnp.concatenate([k_c.reshape(b, n_ctx, bw), k.reshape(b, n_lat, bw)], axis=1)
        v_all = jnp.concatenate([v_c.reshape(b, n_ctx, bw), v.reshape(b, n_lat, bw)], axis=1)
        nk = n_ctx + n_lat
        ck = _pick(nk, (1408, 768, 384, 256, 128))
        att = diff_attention(q.reshape(b, n_lat, bw), k_all, v_all, lam, subln[l], 1.0 - lam_init,
                             tq=_pick(n_lat, (1024, 512, 256, 128)), ck=ck).reshape(m_lat, bw)
        z_lat = matmul(p, w_cdft, x_col=4, tm=_pick(n_lat, (1024, 512, 256)))
        four = fourier_seq_two_stage(z_lat, b, n_lat, bw, n1, n2)
        h_new = mixer_output(h, n_lat, lat_row, xm, p, att, four)

        if not last:
            att_c = diff_attention(q_c.reshape(b, n_ctx, bw), k_c.reshape(b, n_ctx, bw), v_c.reshape(b, n_ctx, bw),
                                   lam, subln[l], 1.0 - lam_init, tq=_pick(n_ctx, (256, 128)),
                                   ck=_pick(n_ctx, (256, 128))).reshape(m_ctx, bw)
            z_ctx = matmul(pc, w_cdft, x_col=4, tm=_pick(n_ctx, (256, 128)))
            four_c = fourier_seq_dense(z_ctx, b, n_ctx, bw)
            hc = mixer_output(hc, n_ctx, ctx_row, xcm, pc, att_c, four_c)
        h = h_new

        if l % 2 == 0:
            w1 = ffn_w1[l // 2].astype(BF16)[None]
            w3 = ffn_w3[l // 2].astype(BF16)[None]
            w2 = _pad_rows(ffn_w2[l // 2].astype(BF16), 1024)[None]
            rt = None
            f_pad = w2.shape[1]
        else:
            w1 = w3 = w2 = None
            rt = router[l // 2]
            f_pad = -(-moe_w2.shape[2] // 256) * 256
        tn_up = _pick(f_pad, (1024, 512, 256))
        tk_dn = _pick(f_pad, (2816, 1024, 512, 256))

        def channel_mix(h2, n_seq, row_fn):
            tm_norm = _pick(n_seq, (512, 256, 128))
            zn = mod_norm(h2, norm_ffn[l], mod3, row_fn(tm_norm), 3, 4, tm_norm, router=rt)
            z_, comb = zn if rt is not None else (zn, None)
            tm = mm_rows(h2, n_seq, row_fn)
            if rt is not None and (TOP_K * h2.shape[0]) % MOE_TILE == 0:
                w1t = jnp.swapaxes(moe_w1[l // 2], 1, 2)
                w3t = jnp.swapaxes(moe_w3[l // 2], 1, 2)
                return moe_sparse(z_, comb, rt.shape[1], w1t, w3t, moe_w2[l // 2], f_pad, h2, mod3,
                                  row_fn(tm_norm), 5, tm_norm)
            if rt is not None:
                z_ = z_.reshape(h2.shape)
                w1_, w3_ = (_pad_last(w[l // 2].astype(BF16), 256) for w in (moe_w1, moe_w3))
                w2_ = _pad_rows(moe_w2[l // 2].astype(BF16), 256)
            else:
                w1_, w3_, w2_ = w1, w3, w2
            hid = swiglu_up(z_, w1_, w3_, tm=tm, tn=tn_up, comb=comb, f_out=f_pad)
            return matmul_residual(hid, w2_, h2, mod3, row_fn(tm), 5, tm=tm, tn=1024, tk=tk_dn)

        h = channel_mix(h, n_lat, lat_row)
        if not last:
            hc = channel_mix(hc, n_ctx, ctx_row)

    return h.reshape(b, n_lat, d)
```
